```python
import math
import jax, jax.numpy as jnp
from jax import lax
import numpy as np

D_MODEL = 2048
BATCH = 2
SEQ = 16384
DEPTH = 2

RET_HEADS = 6
RET_QK_DIM = 64
RET_V_DIM = 128
RET_CHUNK = 128
ATT_HEADS = 6
ATT_HEAD_DIM = 128
IDX_HEADS = 4
IDX_DIM = 64
TOPK_MAX = 256
Q_BLOCK = 128
SG_GROUPS = 4
SG_GROUP_DIM = 128
SG_CHUNK = 128
REL_BUCKETS = 32
REL_MAX_DIST = 128
D_FF = 5632
ROPE_BASE = 10000.0
EPS = 1e-6

RET_QK_W = RET_HEADS * RET_QK_DIM
RET_V_W = RET_HEADS * RET_V_DIM
ATT_W = ATT_HEADS * ATT_HEAD_DIM
IDX_Q_W = IDX_HEADS * IDX_DIM
SG_W = SG_GROUPS * SG_GROUP_DIM
SPLIT_SIZES = (RET_QK_W, RET_QK_W, RET_V_W, RET_V_W, ATT_W, ATT_W, ATT_W,
               IDX_Q_W, IDX_DIM, IDX_HEADS, SG_W, SG_W, D_MODEL, D_MODEL, D_MODEL)
D_IN = sum(SPLIT_SIZES)
SPLIT_POINTS = tuple(int(p) for p in np.cumsum(SPLIT_SIZES)[:-1])

kernel_name = 'hybrid_retention_dsa_sgmlp_block'


def rmsnorm(x, g):
    xf = x.astype(jnp.float32)
    y = xf * lax.rsqrt(jnp.mean(xf * xf, axis=-1, keepdims=True) + EPS)
    return (y * g.astype(jnp.float32)).astype(x.dtype)


def head_groupnorm(o, g):
    B, S, H, dv = o.shape
    of = o.astype(jnp.float32)
    mu = jnp.mean(of, axis=-1, keepdims=True)
    var = jnp.mean(jnp.square(of - mu), axis=-1, keepdims=True)
    y = ((of - mu) * lax.rsqrt(var + EPS)).reshape(B, S, H * dv)
    return (y * g.astype(jnp.float32)).astype(o.dtype)


def swiglu(x, w_gate, w_up, w_down):
    return (jax.nn.silu(x @ w_gate) * (x @ w_up)) @ w_down


def rope(x, positions):
    d = x.shape[-1]
    inv_freq = ROPE_BASE ** (-jnp.arange(0, d, 2, dtype=jnp.float32) / d)
    ang = positions.astype(jnp.float32)[..., None] * inv_freq
    cos = jnp.cos(ang)[:, :, None, :]
    sin = jnp.sin(ang)[:, :, None, :]
    xf = x.astype(jnp.float32)
    x1, x2 = xf[..., : d // 2], xf[..., d // 2:]
    return jnp.concatenate([x1 * cos - x2 * sin, x1 * sin + x2 * cos], axis=-1).astype(x.dtype)


def retention(q, k, v):
    B, S, H, dk = q.shape
    dv = v.shape[-1]
    C = RET_CHUNK
    n = S // C
    dt = q.dtype
    log_g = jnp.log(1.0 - 2.0 ** (-5.0 - jnp.arange(H, dtype=jnp.float32)))
    idx = jnp.arange(C, dtype=jnp.float32)
    rel = idx[:, None] - idx[None, :]
    decay_intra = jnp.where(rel >= 0, jnp.exp(jnp.maximum(rel, 0.0) * log_g[:, None, None]), 0.0).astype(dt)
    k_decay = jnp.exp((C - 1 - idx)[None, :] * log_g[:, None]).astype(dt)
    q_decay = jnp.exp((idx + 1)[None, :] * log_g[:, None]).astype(dt)
    chunk_decay = jnp.exp(C * log_g).astype(dt)

    qc = q.reshape(B, n, C, H, dk) * (dk ** -0.5)
    kc = k.reshape(B, n, C, H, dk)
    vc = v.reshape(B, n, C, H, dv)
    scores = jnp.einsum('bnthd,bnshd->bnhts', qc, kc) * decay_intra
    o_intra = jnp.einsum('bnhts,bnshv->bnthv', scores, vc)
    chunk_kv = jnp.einsum('bnshd,hs,bnshv->bnhdv', kc, k_decay, vc)

    def step(state, kv):
        return state * chunk_decay[None, :, None, None] + kv, state

    init = jnp.zeros((B, H, dk, dv), chunk_kv.dtype)
    _, prev = lax.scan(step, init, jnp.moveaxis(chunk_kv, 1, 0))
    prev = jnp.moveaxis(prev, 0, 1)
    o_cross = jnp.einsum('bnthd,bnhdv,ht->bnthv', qc, prev, q_decay)
    return (o_intra + o_cross).reshape(B, S, H, dv)


def t5_bucket(dist):
    max_exact = REL_BUCKETS // 2
    dist = jnp.maximum(dist, 0)
    df = jnp.maximum(dist, 1).astype(jnp.float32)
    large = max_exact + (jnp.log(df / max_exact) / math.log(REL_MAX_DIST / max_exact)
                         * (REL_BUCKETS - max_exact)).astype(jnp.int32)
    large = jnp.minimum(large, REL_BUCKETS - 1)
    return jnp.where(dist < max_exact, dist, large)


def sparse_attention(q, k, v, q_idx, k_idx, w_idx, positions, rel_table):
    B, S, H, dh = q.shape
    n_keep = min(TOPK_MAX, S // 4)
    nb = S // Q_BLOCK
    key_pos = jnp.arange(S)
    w_scaled = w_idx * (IDX_HEADS ** -0.5)

    def block(bi):
        t0 = bi * Q_BLOCK
        qb = lax.dynamic_slice_in_dim(q, t0, Q_BLOCK, axis=1)
        qib = lax.dynamic_slice_in_dim(q_idx, t0, Q_BLOCK, axis=1)
        wib = lax.dynamic_slice_in_dim(w_scaled, t0, Q_BLOCK, axis=1)
        posb = lax.dynamic_slice_in_dim(positions, t0, Q_BLOCK, axis=1)
        tq = t0 + jnp.arange(Q_BLOCK)
        s_h = jax.nn.relu(jnp.einsum('bthd,bsd->bths', qib, k_idx).astype(jnp.float32) * (IDX_DIM ** -0.5))
        score = jnp.einsum('bths,bth->bts', s_h, wib.astype(jnp.float32))
        causal = key_pos[None, :] <= tq[:, None]
        score = jnp.where(causal[None], score, -jnp.inf)
        _, sel = lax.top_k(score, n_keep)
        k_sel = jax.vmap(lambda kb, ib: kb[ib])(k, sel)
        v_sel = jax.vmap(lambda vb, ib: vb[ib])(v, sel)
        pos_sel = jax.vmap(lambda pb, ib: pb[ib])(positions, sel)
        logits = jnp.einsum('bthd,btkhd->bthk', qb, k_sel).astype(jnp.float32) * (dh ** -0.5)
        bias = rel_table[t5_bucket(posb[:, :, None] - pos_sel)].astype(jnp.float32)
        logits = logits + jnp.transpose(bias, (0, 1, 3, 2))
        valid = sel <= tq[None, :, None]
        logits = jnp.where(valid[:, :, None, :], logits, -jnp.inf)
        p = jax.nn.softmax(logits, axis=-1).astype(v.dtype)
        return jnp.einsum('bthk,btkhd->bthd', p, v_sel)

    out = lax.map(block, jnp.arange(nb))
    return jnp.moveaxis(out, 0, 1).reshape(B, S, H * dh)


def spatial_gating(u, v, w_s, b_s):
    B, S, _ = u.shape
    n = S // SG_CHUNK
    vr = v.reshape(B, n, SG_CHUNK, SG_GROUPS, SG_GROUP_DIM)
    mask = jnp.tril(jnp.ones((SG_CHUNK, SG_CHUNK), w_s.dtype))
    mixed = jnp.einsum('gts,bnsgd->bntgd', w_s * mask, vr) + b_s.T[None, None, :, :, None]
    return u * mixed.reshape(B, S, SG_W)


def setup_inputs(seed: int = 0) -> dict:
    key = jax.random.key(seed)
    ks = jax.random.split(key, 24)
    f32 = jnp.float32

    def nrm(k, shape, fan_in, scale=1.0):
        return jax.random.normal(k, shape, f32) * (scale * fan_in ** -0.5)

    def gain(k, shape):
        return 1.0 + 0.02 * jax.random.normal(k, shape, f32)

    res_scale = (2.0 * DEPTH) ** -0.5
    return {
        'x': jax.random.normal(ks[0], (BATCH, SEQ, D_MODEL), f32),
        'positions': jnp.tile(jnp.arange(SEQ, dtype=jnp.int32)[None, :], (BATCH, 1)),
        'rel_table': 0.5 * jax.random.normal(ks[1], (REL_BUCKETS, ATT_HEADS), f32),
        'ffn1_norm': gain(ks[2], (DEPTH, D_MODEL)),
        'ffn1_w_gate': nrm(ks[3], (DEPTH, D_MODEL, D_FF), D_MODEL),
        'ffn1_w_up': nrm(ks[4], (DEPTH, D_MODEL, D_FF), D_MODEL),
        'ffn1_w_down': nrm(ks[5], (DEPTH, D_FF, D_MODEL), D_FF, res_scale),
        'mix_norm': gain(ks[6], (DEPTH, D_MODEL)),
        'w_in': nrm(ks[7], (DEPTH, D_MODEL, D_IN), D_MODEL),
        'ret_norm': gain(ks[8], (DEPTH, RET_V_W)),
        'sg_norm': gain(ks[9], (DEPTH, SG_W)),
        'sg_w': nrm(ks[10], (DEPTH, SG_GROUPS, SG_CHUNK, SG_CHUNK), SG_CHUNK, 0.5),
        'sg_b': 1.0 + 0.01 * jax.random.normal(ks[11], (DEPTH, SG_GROUPS, SG_CHUNK), f32),
        'w_br_ret': nrm(ks[12], (DEPTH, RET_V_W, D_MODEL), RET_V_W),
        'w_br_att': nrm(ks[13], (DEPTH, ATT_W, D_MODEL), ATT_W),
        'w_br_sg': nrm(ks[14], (DEPTH, SG_W, D_MODEL), SG_W),
        'w_out': nrm(ks[15], (DEPTH, D_MODEL, D_MODEL), D_MODEL, res_scale),
        'ffn2_norm': gain(ks[16], (DEPTH, D_MODEL)),
        'ffn2_w_gate': nrm(ks[17], (DEPTH, D_MODEL, D_FF), D_MODEL),
        'ffn2_w_up': nrm(ks[18], (DEPTH, D_MODEL, D_FF), D_MODEL),
        'ffn2_w_down': nrm(ks[19], (DEPTH, D_FF, D_MODEL), D_FF, res_scale),
        'final_norm': gain(ks[20], (D_MODEL,)),
    }


def reference(x, positions, rel_table, ffn1_norm, ffn1_w_gate, ffn1_w_up, ffn1_w_down,
              mix_norm, w_in, ret_norm, sg_norm, sg_w, sg_b, w_br_ret, w_br_att, w_br_sg,
              w_out, ffn2_norm, ffn2_w_gate, ffn2_w_up, ffn2_w_down, final_norm):
    B, S, _ = x.shape
    for l in range(DEPTH):
        x = x + 0.5 * swiglu(rmsnorm(x, ffn1_norm[l]), ffn1_w_gate[l], ffn1_w_up[l], ffn1_w_down[l])

        h = rmsnorm(x, mix_norm[l])
        proj = h @ w_in[l]
        (q_r, k_r, v_r, g_r, q_a, k_a, v_a, q_i, k_i, w_i,
         u_s, v_s, gate_r, gate_a, gate_s) = jnp.split(proj, SPLIT_POINTS, axis=-1)

        q_r = rope(q_r.reshape(B, S, RET_HEADS, RET_QK_DIM), positions)
        k_r = rope(k_r.reshape(B, S, RET_HEADS, RET_QK_DIM), positions)
        o_r = retention(q_r, k_r, v_r.reshape(B, S, RET_HEADS, RET_V_DIM))
        o_r = head_groupnorm(o_r, ret_norm[l]) * jax.nn.silu(g_r)

        o_a = sparse_attention(q_a.reshape(B, S, ATT_HEADS, ATT_HEAD_DIM),
                               k_a.reshape(B, S, ATT_HEADS, ATT_HEAD_DIM),
                               v_a.reshape(B, S, ATT_HEADS, ATT_HEAD_DIM),
                               q_i.reshape(B, S, IDX_HEADS, IDX_DIM), k_i, w_i,
                               positions, rel_table)

        u_s = jax.nn.gelu(u_s)
        v_s = rmsnorm(jax.nn.gelu(v_s), sg_norm[l])
        o_s = spatial_gating(u_s, v_s, sg_w[l], sg_b[l])

        merged = (jax.nn.sigmoid(gate_r) * (o_r @ w_br_ret[l])
                  + jax.nn.sigmoid(gate_a) * (o_a @ w_br_att[l])
                  + jax.nn.sigmoid(gate_s) * (o_s @ w_br_sg[l]))
        x = x + merged @ w_out[l]

        x = x + 0.5 * swiglu(rmsnorm(x, ffn2_norm[l]), ffn2_w_gate[l], ffn2_w_up[l], ffn2_w_down[l])
    return rmsnorm(x, final_norm)
```

```python
import functools
import math

import numpy as np
import jax
import jax.numpy as jnp
from jax import lax
from jax.experimental import pallas as pl
from jax.experimental.pallas import tpu as pltpu

F32 = jnp.float32
BF16 = jnp.bfloat16
I32 = jnp.int32

RET_HEADS, RET_QK_DIM, RET_V_DIM, RET_CHUNK = 6, 64, 128, 128
ATT_HEADS, ATT_HEAD_DIM = 6, 128
IDX_HEADS, IDX_DIM = 4, 64
TOPK_MAX = 256
SG_GROUPS, SG_GROUP_DIM, SG_CHUNK = 4, 128, 128
REL_BUCKETS, REL_MAX_DIST = 32, 128
ROPE_BASE = 10000.0
EPS = 1e-6

RET_QK_W = RET_HEADS * RET_QK_DIM
RET_V_W = RET_HEADS * RET_V_DIM
ATT_W = ATT_HEADS * ATT_HEAD_DIM
IDX_Q_W = IDX_HEADS * IDX_DIM
SG_W = SG_GROUPS * SG_GROUP_DIM

LANES = 128
VMEM_LIMIT_BYTES = 56 * 1024 * 1024

COL_QR, COL_KR, COL_VR, COL_GR = 0, 384, 768, 1536
COL_QA, COL_KA, COL_VA = 2304, 3072, 3840
COL_US, COL_VS = 4608, 5120
COL_QI, COL_KIW = 5632, 5888
COL_GATES = 6144


def _proj_width(d_model):
    return COL_GATES + 3 * d_model


INT_MIN = np.int32(-2 ** 31)
INT_MAX = np.int32(2 ** 31 - 1)
NEG_INF = float("-inf")


def _cparams(sem):
    return pltpu.CompilerParams(dimension_semantics=sem, vmem_limit_bytes=VMEM_LIMIT_BYTES)


def _rms(x, g):
    return x * lax.rsqrt(jnp.mean(x * x, axis=-1, keepdims=True) + EPS) * g


def _ffn_kernel(x_ref, g_ref, wg_ref, wu_ref, wd_ref, *rest, final_norm):
    if final_norm:
        fg_ref, o_ref, xn_ref = rest
    else:
        o_ref, xn_ref = rest
    j = pl.program_id(1)

    @pl.when(j == 0)
    def _():
        xn_ref[...] = _rms(x_ref[...], g_ref[...]).astype(BF16)
        o_ref[...] = jnp.zeros_like(o_ref)

    xn = xn_ref[...]
    a = jnp.dot(xn, wg_ref[...], preferred_element_type=F32)
    b = jnp.dot(xn, wu_ref[...], preferred_element_type=F32)
    h = (a * jax.nn.sigmoid(a) * b).astype(BF16)
    o_ref[...] += jnp.dot(h, wd_ref[...], preferred_element_type=F32)

    @pl.when(j == pl.num_programs(1) - 1)
    def _():
        y = x_ref[...] + 0.5 * o_ref[...]
        if final_norm:
            y = _rms(y, fg_ref[...])
        o_ref[...] = y


def _ffn(x2, gain, wg, wu, wd, final_gain=None, *, tm, tf):
    t, d = x2.shape
    f = wg.shape[1]
    final_norm = final_gain is not None
    in_specs = [
        pl.BlockSpec((tm, d), lambda i, j: (i, 0)),
        pl.BlockSpec((1, d), lambda i, j: (0, 0)),
        pl.BlockSpec((d, tf), lambda i, j: (0, j)),
        pl.BlockSpec((d, tf), lambda i, j: (0, j)),
        pl.BlockSpec((tf, d), lambda i, j: (j, 0)),
    ]
    args = [x2, gain.reshape(1, d), wg, wu, wd]
    if final_norm:
        in_specs.append(pl.BlockSpec((1, d), lambda i, j: (0, 0)))
        args.append(final_gain.reshape(1, d))
    return pl.pallas_call(
        functools.partial(_ffn_kernel, final_norm=final_norm),
        out_shape=jax.ShapeDtypeStruct((t, d), F32),
        grid=(t // tm, f // tf),
        in_specs=in_specs,
        out_specs=pl.BlockSpec((tm, d), lambda i, j: (i, 0)),
        scratch_shapes=[pltpu.VMEM((tm, d), BF16)],
        compiler_params=_cparams(("parallel", "arbitrary")),
        name="ffn",
    )(*args)


def _proj_kernel(x_ref, g_ref, w_ref, o_ref, xn_ref):
    @pl.when(pl.program_id(1) == 0)
    def _():
        xn_ref[...] = _rms(x_ref[...], g_ref[...]).astype(BF16)

    o_ref[...] = jnp.dot(xn_ref[...], w_ref[...], preferred_element_type=F32).astype(BF16)


def _proj(x2, gain, w, *, tm, tn):
    t, d = x2.shape
    n = w.shape[1]
    return pl.pallas_call(
        _proj_kernel,
        out_shape=jax.ShapeDtypeStruct((t, n), BF16),
        grid=(t // tm, n // tn),
        in_specs=[
            pl.BlockSpec((tm, d), lambda i, j: (i, 0)),
            pl.BlockSpec((1, d), lambda i, j: (0, 0)),
            pl.BlockSpec((d, tn), lambda i, j: (0, j)),
        ],
        out_specs=pl.BlockSpec((tm, tn), lambda i, j: (i, j)),
        scratch_shapes=[pltpu.VMEM((tm, d), BF16)],
        compiler_params=_cparams(("parallel", "arbitrary")),
        name="proj",
    )(x2, gain.reshape(1, d), w)


def _rope_table_kernel(pos_ref, invf_ref, sign_ref, c_ref, s_ref):
    ang = pos_ref[0].astype(F32) * invf_ref[...]
    c_ref[0] = jnp.cos(ang)
    s_ref[0] = jnp.sin(ang) * sign_ref[...]


def _rope_tables(positions, *, tm):
    b, s = positions.shape
    half = RET_QK_DIM // 2
    inv_freq = ROPE_BASE ** (-jnp.arange(0, RET_QK_DIM, 2, dtype=F32) / RET_QK_DIM)
    lane = np.arange(LANES)
    invf = inv_freq[lane % half].reshape(1, LANES)
    sign = jnp.asarray(np.where(lane % RET_QK_DIM < half, -1.0, 1.0), F32).reshape(1, LANES)
    out = jax.ShapeDtypeStruct((b, s, LANES), F32)
    return pl.pallas_call(
        _rope_table_kernel,
        out_shape=(out, out),
        grid=(b, s // tm),
        in_specs=[
            pl.BlockSpec((1, tm, 1), lambda i, j: (i, j, 0)),
            pl.BlockSpec((1, LANES), lambda i, j: (0, 0)),
            pl.BlockSpec((1, LANES), lambda i, j: (0, 0)),
        ],
        out_specs=(pl.BlockSpec((1, tm, LANES), lambda i, j: (i, j, 0)),
                   pl.BlockSpec((1, tm, LANES), lambda i, j: (i, j, 0))),
        compiler_params=_cparams(("parallel", "parallel")),
        name="rope_tables",
    )(positions.reshape(b, s, 1), invf, sign)


def _retention_constants():
    c = RET_CHUNK
    h = np.arange(RET_HEADS, dtype=np.float64)
    log_g = np.log(1.0 - 2.0 ** (-5.0 - h))
    idx = np.arange(c, dtype=np.float64)
    rel = idx[:, None] - idx[None, :]
    dmat = np.where(rel >= 0, np.exp(np.maximum(rel, 0.0) * log_g[:, None, None]), 0.0)
    k_decay = np.exp((c - 1 - idx)[None, :] * log_g[:, None])
    q_decay = np.exp((idx + 1)[None, :] * log_g[:, None])
    chunk_decay = np.exp(c * log_g)
    kd = np.repeat(k_decay.T, RET_QK_DIM, axis=1)
    qd = np.repeat(q_decay.T, RET_QK_DIM, axis=1) * RET_QK_DIM ** -0.5
    lane = np.arange(RET_QK_W)
    src = (lane // RET_QK_DIM) * RET_QK_DIM + (lane % RET_QK_DIM + RET_QK_DIM // 2) % RET_QK_DIM
    perm = np.zeros((RET_QK_W, RET_QK_W), np.float32)
    perm[src, lane] = 1.0
    return (jnp.asarray(dmat, F32), jnp.asarray(qd, F32), jnp.asarray(kd, F32),
            [float(np.float32(v)) for v in chunk_decay], jnp.asarray(perm, BF16))


def _gelu_tanh(x):
    return 0.5 * x * (1.0 + jnp.tanh(math.sqrt(2.0 / math.pi) * (x + 0.044715 * (x * x * x))))


def _local_kernel(q_ref, k_ref, v_ref, g_ref, u_ref, vs_ref, cos_ref, sin_ref,
                  perm_ref, dmat_ref, qd_ref, kd_ref, rn_ref, sn_ref, sgw_ref, sgb_ref,
                  or_ref, os_ref, state_ref, *, chunk_decay):
    @pl.when(pl.program_id(1) == 0)
    def _():
        state_ref[...] = jnp.zeros_like(state_ref)

    n_rep = RET_QK_W // LANES
    cos3 = jnp.concatenate([cos_ref[0]] * n_rep, axis=1)
    sin3 = jnp.concatenate([sin_ref[0]] * n_rep, axis=1)
    q = q_ref[0]
    k = k_ref[0]
    perm = perm_ref[...]
    qr = q.astype(F32) * cos3 + jnp.dot(q, perm, preferred_element_type=F32) * sin3
    kr = k.astype(F32) * cos3 + jnp.dot(k, perm, preferred_element_type=F32) * sin3
    q_in = (qr * (RET_QK_DIM ** -0.5)).astype(BF16)
    q_cr = (qr * qd_ref[...]).astype(BF16)
    k_b = kr.astype(BF16)
    k_dec_t = jnp.transpose(kr * kd_ref[...]).astype(BF16)
    v = v_ref[0]
    outs = []
    for h in range(RET_HEADS):
        qs = slice(h * RET_QK_DIM, (h + 1) * RET_QK_DIM)
        vh = v[:, h * RET_V_DIM:(h + 1) * RET_V_DIM]
        a = lax.dot_general(q_in[:, qs], k_b[:, qs], (((1,), (1,)), ((), ())),
                            preferred_element_type=F32) * dmat_ref[h]
        prev = state_ref[h]
        o = (jnp.dot(a.astype(BF16), vh, preferred_element_type=F32)
             + jnp.dot(q_cr[:, qs], prev.astype(BF16), preferred_element_type=F32))
        state_ref[h] = prev * chunk_decay[h] + jnp.dot(k_dec_t[qs, :], vh, preferred_element_type=F32)
        mu = jnp.mean(o, axis=-1, keepdims=True)
        d = o - mu
        var = jnp.mean(d * d, axis=-1, keepdims=True)
        outs.append(d * lax.rsqrt(var + EPS))
    g = g_ref[0].astype(F32)
    y = jnp.concatenate(outs, axis=1) * rn_ref[...]
    or_ref[0] = (y * (g * jax.nn.sigmoid(g))).astype(BF16)

    u = _gelu_tanh(u_ref[0].astype(F32))
    vn = _rms(_gelu_tanh(vs_ref[0].astype(F32)), sn_ref[...])
    vn_b = vn.astype(BF16)
    c = SG_CHUNK
    tril = lax.broadcasted_iota(I32, (c, c), 0) >= lax.broadcasted_iota(I32, (c, c), 1)
    bias = sgb_ref[...]
    mixed = []
    for gi in range(SG_GROUPS):
        w = jnp.where(tril, sgw_ref[gi], jnp.zeros((), BF16))
        m = jnp.dot(w, vn_b[:, gi * SG_GROUP_DIM:(gi + 1) * SG_GROUP_DIM], preferred_element_type=F32)
        mixed.append(m + bias[:, gi:gi + 1])
    os_ref[0] = (u * jnp.concatenate(mixed, axis=1)).astype(BF16)


def _local(p3, cos_t, sin_t, ret_norm, sg_norm, sg_w, sg_b, consts):
    b, s, _ = p3.shape
    c = RET_CHUNK
    dmat, qd, kd, chunk_decay, perm = consts

    def pspec(width, col):
        blk = col // width
        return pl.BlockSpec((1, c, width), lambda i, j: (i, j, blk))

    def whole(arr):
        nd = arr.ndim
        return pl.BlockSpec(arr.shape, lambda i, j: (0,) * nd)

    rn = ret_norm.reshape(1, RET_V_W)
    sn = sg_norm.reshape(1, SG_W)
    sgw = sg_w.astype(BF16)
    sgb_t = jnp.transpose(sg_b)
    tab = pl.BlockSpec((1, c, LANES), lambda i, j: (i, j, 0))
    return pl.pallas_call(
        functools.partial(_local_kernel, chunk_decay=chunk_decay),
        out_shape=(jax.ShapeDtypeStruct((b, s, RET_V_W), BF16),
                   jax.ShapeDtypeStruct((b, s, SG_W), BF16)),
        grid=(b, s // c),
        in_specs=[
            pspec(RET_QK_W, COL_QR), pspec(RET_QK_W, COL_KR), pspec(RET_V_W, COL_VR),
            pspec(RET_V_W, COL_GR), pspec(SG_W, COL_US), pspec(SG_W, COL_VS),
            tab, tab,
            whole(perm), whole(dmat), whole(qd), whole(kd), whole(rn), whole(sn), whole(sgw), whole(sgb_t),
        ],
        out_specs=(pl.BlockSpec((1, c, RET_V_W), lambda i, j: (i, j, 0)),
                   pl.BlockSpec((1, c, SG_W), lambda i, j: (i, j, 0))),
        scratch_shapes=[pltpu.VMEM((RET_HEADS, RET_QK_DIM, RET_V_DIM), F32)],
        compiler_params=_cparams(("parallel", "arbitrary")),
        name="local_mixers",
    )(p3, p3, p3, p3, p3, p3, cos_t, sin_t, perm, dmat, qd, kd, rn, sn, sgw, sgb_t)


def _t5_bucket_table():
    max_exact = REL_BUCKETS // 2
    d = np.arange(REL_MAX_DIST)
    df = np.maximum(d, 1).astype(np.float32)
    large = max_exact + (np.log(df / max_exact) / np.float32(math.log(REL_MAX_DIST / max_exact))
                         * (REL_BUCKETS - max_exact)).astype(np.int32)
    large = np.minimum(large, REL_BUCKETS - 1)
    return np.where(d < max_exact, d, large)


def _bucket_starts():
    bk = _t5_bucket_table()
    assert np.all(np.diff(bk) >= 0) and bk[-1] == REL_BUCKETS - 1
    return [(int(np.argmax(bk == v)), int(v)) for v in sorted(set(bk.tolist())) if v > 0]


def _attn_kernel(tq_ref, tc_ref, qmin_ref, kmax_ref, rel_ref,
                 qa_ref, qi_ref, kiwq_ref, kiw_ref, ka_ref, va_ref, posq_ref, posk_ref,
                 o_ref, keys_ref, tau_ref, jst_ref, m_ref, l_ref, acc_ref,
                 *, bq, bk, rb, n_keep, seq, far_dist, bucket_starts):
    b = pl.program_id(0)
    step = pl.program_id(1)
    qi = tq_ref[step]
    c = tc_ref[step]
    n_chunks = (qi * bq + bq - 1) // bk + 1
    nlc = bk // LANES
    k_f = float(n_keep)

    def tile_lanes(x):
        return jnp.concatenate([x] * nlc, axis=1)

    @pl.when(c == 0)
    def _prologue():
        qiv = qi_ref[0]
        zpad = jnp.zeros((bq, LANES - IDX_DIM), BF16)
        q_heads = [jnp.concatenate([qiv[:, h * IDX_DIM:(h + 1) * IDX_DIM], zpad], axis=1)
                   for h in range(IDX_HEADS)]
        w = kiwq_ref[0][:, IDX_DIM:IDX_DIM + IDX_HEADS].astype(F32)
        w_bc = [jnp.broadcast_to(w[:, h:h + 1], (bq, bk)) for h in range(IDX_HEADS)]
        row_t = qi * bq + lax.broadcasted_iota(I32, (bq, bk), 0)
        lane_s = lax.broadcasted_iota(I32, (bq, bk), 1)

        def score_body(cc, carry):
            off = pl.multiple_of(cc * bk, bk)
            kc = kiw_ref[0, pl.ds(off, bk), :]
            sc = jnp.zeros((bq, bk), F32)
            for h in range(IDX_HEADS):
                sh = lax.dot_general(q_heads[h], kc, (((1,), (1,)), ((), ())), preferred_element_type=F32)
                sc = sc + w_bc[h] * jnp.maximum(sh, 0.0)
            sc = jnp.where(sc == 0.0, 0.0, sc)
            bits = lax.bitcast_convert_type(sc, I32)
            key = jnp.where(bits < 0, bits ^ INT_MAX, bits)
            keys_ref[cc] = jnp.where(cc * bk + lane_s <= row_t, key, INT_MIN)
            return carry

        lax.fori_loop(0, n_chunks, score_body, 0)

        def count_ge(r0, thr):
            def body(cc, acc):
                kt = keys_ref[cc, pl.ds(r0, rb), :]
                for j in range(nlc):
                    acc = acc + jnp.where(kt[:, j * LANES:(j + 1) * LANES] >= thr, 1.0, 0.0)
                return acc
            acc = lax.fori_loop(0, n_chunks, body, jnp.zeros((rb, LANES), F32))
            return jnp.broadcast_to(jnp.sum(acc, axis=1, keepdims=True), (rb, LANES))

        def row_block(r, any_excess):
            r0 = pl.multiple_of(r * rb, rb)

            def bis(_, lohi):
                lo, hi = lohi
                mid = (lo & hi) + ((lo ^ hi) >> 1)
                ok = count_ge(r0, mid) >= k_f
                return jnp.where(ok, mid, lo), jnp.where(ok, hi, mid)

            lo, _ = lax.fori_loop(0, 32, bis, (jnp.full((rb, LANES), INT_MIN, I32),
                                               jnp.full((rb, LANES), INT_MAX, I32)))
            tau_ref[pl.ds(r0, rb), :] = lo
            c_ge = count_ge(r0, lo)
            c_gt = count_ge(r0, lo + 1)
            need = k_f - c_gt
            excess = jnp.logical_and(c_ge > k_f, lo != INT_MIN)
            jst_ref[pl.ds(r0, rb), :] = jnp.full((rb, LANES), seq, I32)
            blk_excess = jnp.max(jnp.where(excess, 1.0, 0.0)) > 0.0

            @pl.when(blk_excess)
            def _ties():
                lane = lax.broadcasted_iota(I32, (rb, LANES), 1)

                def count_tie_le(jmax):
                    def body(cc, acc):
                        kt = keys_ref[cc, pl.ds(r0, rb), :]
                        for j in range(nlc):
                            hit = jnp.logical_and(kt[:, j * LANES:(j + 1) * LANES] == lo,
                                                  lane <= jmax - (cc * bk + j * LANES))
                            acc = acc + jnp.where(hit, 1.0, 0.0)
                        return acc
                    acc = lax.fori_loop(0, n_chunks, body, jnp.zeros((rb, LANES), F32))
                    return jnp.broadcast_to(jnp.sum(acc, axis=1, keepdims=True), (rb, LANES))

                def bis_j(_, lohi):
                    jl, jh = lohi
                    mid = (jl + jh) >> 1
                    ok = count_tie_le(mid) >= need
                    return jnp.where(ok, jl, mid), jnp.where(ok, mid, jh)

                n_it = max(1, int(math.ceil(math.log2(seq + 1))))
                _, jh = lax.fori_loop(0, n_it, bis_j, (jnp.full((rb, LANES), -1, I32),
                                                       jnp.full((rb, LANES), seq - 1, I32)))
                jst_ref[pl.ds(r0, rb), :] = jnp.where(excess, jh, seq)

            return any_excess

        lax.fori_loop(0, bq // rb, row_block, 0)

        m_ref[...] = jnp.full(m_ref.shape, NEG_INF, F32)
        l_ref[...] = jnp.zeros_like(l_ref)
        acc_ref[...] = jnp.zeros_like(acc_ref)

    kt = keys_ref[c]
    tau = tile_lanes(tau_ref[...])
    jst = tile_lanes(jst_ref[...])
    kidx = c * bk + lax.broadcasted_iota(I32, (bq, bk), 1)
    sel = jnp.logical_and(
        jnp.logical_or(kt > tau, jnp.logical_and(kt == tau, kidx <= jst)),
        kt != INT_MIN)
    scale = ATT_HEAD_DIM ** -0.5
    q = qa_ref[0]
    kk = ka_ref[0]
    vv = va_ref[0]
    last_bucket = bucket_starts[-1][1]

    def heads(near):
        if near:
            dist = posq_ref[0] - posk_ref[0, 0]
            ge = [dist >= start for start, _ in bucket_starts]
        for h in range(ATT_HEADS):
            hs = slice(h * ATT_HEAD_DIM, (h + 1) * ATT_HEAD_DIM)
            s = lax.dot_general(q[:, hs], kk[:, hs], (((1,), (1,)), ((), ())),
                                preferred_element_type=F32) * scale
            if near:
                bias = jnp.full((bq, bk), rel_ref[0, h], F32)
                for g, (_, bucket) in zip(ge, bucket_starts):
                    bias = jnp.where(g, rel_ref[bucket, h], bias)
                s = s + bias
            else:
                s = s + rel_ref[last_bucket, h]
            s = jnp.where(sel, s, NEG_INF)
            m_old = m_ref[h]
            m_new = jnp.maximum(m_old, jnp.max(s, axis=1, keepdims=True))
            m_safe = jnp.where(m_new == NEG_INF, 0.0, m_new)
            alpha = jnp.exp(m_old - m_safe)
            p = jnp.exp(s - tile_lanes(m_safe))
            l_ref[h] = alpha * l_ref[h] + jnp.sum(p, axis=1, keepdims=True)
            acc_ref[h] = alpha * acc_ref[h] + jnp.dot(p.astype(BF16), vv[:, hs], preferred_element_type=F32)
            m_ref[h] = m_new

    far = qmin_ref[b * (seq // bq) + qi] - kmax_ref[b * (seq // bk) + c] >= far_dist

    @pl.when(far)
    def _():
        heads(False)

    @pl.when(jnp.logical_not(far))
    def _():
        heads(True)

    @pl.when(c == n_chunks - 1)
    def _():
        o_ref[0] = jnp.concatenate([acc_ref[h] / l_ref[h] for h in range(ATT_HEADS)], axis=1).astype(BF16)


def _attn(p3, positions, rel_table, *, bq, bk, rb):
    b, s, _ = p3.shape
    n_keep = min(TOPK_MAX, s // 4)
    nq, nkc = s // bq, s // bk
    tq, tc = [], []
    for qi in range(nq):
        for c in range((qi * bq + bq - 1) // bk + 1):
            tq.append(qi)
            tc.append(c)
    n_steps = len(tq)
    tq = jnp.asarray(np.asarray(tq, np.int32))
    tc = jnp.asarray(np.asarray(tc, np.int32))
    qmin = jnp.min(positions.reshape(b, nq, bq), axis=-1).reshape(-1)
    kmax = jnp.max(positions.reshape(b, nkc, bk), axis=-1).reshape(-1)
    starts = _bucket_starts()
    far_dist = REL_MAX_DIST - 1
    assert starts[-1][0] <= far_dist

    def qspec(width, col):
        blk = col // width
        return pl.BlockSpec((1, bq, width), lambda i, t, tq, tc, *_: (i, tq[t], blk))

    def kspec(width, col):
        blk = col // width
        return pl.BlockSpec((1, bk, width), lambda i, t, tq, tc, *_: (i, tc[t], blk))

    grid_spec = pltpu.PrefetchScalarGridSpec(
        num_scalar_prefetch=5,
        grid=(b, n_steps),
        in_specs=[
            qspec(ATT_W, COL_QA), qspec(IDX_Q_W, COL_QI), qspec(LANES, COL_KIW),
            pl.BlockSpec((1, s, LANES), lambda i, t, *_: (i, 0, COL_KIW // LANES)),
            kspec(ATT_W, COL_KA), kspec(ATT_W, COL_VA),
            pl.BlockSpec((1, bq, 1), lambda i, t, tq, tc, *_: (i, tq[t], 0)),
            pl.BlockSpec((1, 1, 1, bk), lambda i, t, tq, tc, *_: (i, tc[t], 0, 0)),
        ],
        out_specs=pl.BlockSpec((1, bq, ATT_W), lambda i, t, tq, tc, *_: (i, tq[t], 0)),
        scratch_shapes=[
            pltpu.VMEM((nkc, bq, bk), I32),
            pltpu.VMEM((bq, LANES), I32),
            pltpu.VMEM((bq, LANES), I32),
            pltpu.VMEM((ATT_HEADS, bq, LANES), F32),
            pltpu.VMEM((ATT_HEADS, bq, LANES), F32),
            pltpu.VMEM((ATT_HEADS, bq, ATT_HEAD_DIM), F32),
        ],
    )
    kern = functools.partial(_attn_kernel, bq=bq, bk=bk, rb=rb, n_keep=n_keep, seq=s,
                             far_dist=far_dist, bucket_starts=starts)
    return pl.pallas_call(
        kern,
        out_shape=jax.ShapeDtypeStruct((b, s, ATT_W), BF16),
        grid_spec=grid_spec,
        compiler_params=_cparams(("parallel", "arbitrary")),
        name="sparse_attn",
    )(tq, tc, qmin, kmax, rel_table,
      p3, p3, p3, p3, p3, p3, positions.reshape(b, s, 1), positions.reshape(b, nkc, 1, bk))


def _merge_kernel(or_ref, oa_ref, os_ref, gr_ref, ga_ref, gs_ref, wr_ref, wa_ref, ws_ref, wo_ref, x_ref, o_ref):
    j = pl.program_id(1)

    @pl.when(j == 0)
    def _():
        o_ref[...] = jnp.zeros_like(o_ref)

    def branch(o, w, g):
        return jax.nn.sigmoid(g[...].astype(F32)) * jnp.dot(o[...], w[...], preferred_element_type=F32)

    merged = branch(or_ref, wr_ref, gr_ref) + branch(oa_ref, wa_ref, ga_ref) + branch(os_ref, ws_ref, gs_ref)
    o_ref[...] += jnp.dot(merged.astype(BF16), wo_ref[...], preferred_element_type=F32)

    @pl.when(j == pl.num_programs(1) - 1)
    def _():
        o_ref[...] = x_ref[...] + o_ref[...]


def _merge(o_r, o_a, o_s, p2, wr, wa, ws, wo, x2, *, tm, tn):
    t, d = x2.shape
    gate_blk = COL_GATES // tn
    per_gate = d // tn

    def gspec(g):
        return pl.BlockSpec((tm, tn), lambda i, j: (i, gate_blk + g * per_gate + j))

    return pl.pallas_call(
        _merge_kernel,
        out_shape=jax.ShapeDtypeStruct((t, d), F32),
        grid=(t // tm, d // tn),
        in_specs=[
            pl.BlockSpec((tm, RET_V_W), lambda i, j: (i, 0)),
            pl.BlockSpec((tm, ATT_W), lambda i, j: (i, 0)),
            pl.BlockSpec((tm, SG_W), lambda i, j: (i, 0)),
            gspec(0), gspec(1), gspec(2),
            pl.BlockSpec((RET_V_W, tn), lambda i, j: (0, j)),
            pl.BlockSpec((ATT_W, tn), lambda i, j: (0, j)),
            pl.BlockSpec((SG_W, tn), lambda i, j: (0, j)),
            pl.BlockSpec((tn, d), lambda i, j: (j, 0)),
            pl.BlockSpec((tm, d), lambda i, j: (i, 0)),
        ],
        out_specs=pl.BlockSpec((tm, d), lambda i, j: (i, 0)),
        compiler_params=_cparams(("parallel", "arbitrary")),
        name="merge",
    )(o_r, o_a, o_s, p2, p2, p2, wr, wa, ws, wo, x2)


def _relayout_w_in(w, d_model):
    sizes = (RET_QK_W, RET_QK_W, RET_V_W, RET_V_W, ATT_W, ATT_W, ATT_W, IDX_Q_W, IDX_DIM, IDX_HEADS,
             SG_W, SG_W, d_model, d_model, d_model)
    offs = np.concatenate([[0], np.cumsum(sizes)])
    (q_r, k_r, v_r, g_r, q_a, k_a, v_a, q_i, k_i, w_i, u_s, v_s, gt_r, gt_a, gt_s) = [
        w[:, offs[n]:offs[n + 1]] for n in range(len(sizes))]
    rows = w.shape[0]
    pad_kiw = jnp.zeros((rows, LANES - IDX_DIM - IDX_HEADS), w.dtype)
    pad_blk = jnp.zeros((rows, COL_GATES - COL_KIW - LANES), w.dtype)
    out = jnp.concatenate([q_r, k_r, v_r, g_r, q_a, k_a, v_a, u_s, v_s, q_i, k_i, w_i, pad_kiw, pad_blk,
                           gt_r, gt_a, gt_s], axis=1).astype(BF16)
    assert out.shape[1] == _proj_width(d_model)
    return out


def _pick(n, prefs):
    for p in prefs:
        if n % p == 0:
            return p
    return n


def kernel(x, positions, rel_table, ffn1_norm, ffn1_w_gate, ffn1_w_up, ffn1_w_down, mix_norm, w_in, ret_norm, sg_norm, sg_w, sg_b, w_br_ret, w_br_att, w_br_sg, w_out, ffn2_norm, ffn2_w_gate, ffn2_w_up, ffn2_w_down, final_norm):
    b, s, d = x.shape
    t = b * s
    depth = w_in.shape[0]
    d_ff = ffn1_w_gate.shape[2]
    assert s % RET_CHUNK == 0 and d == COL_GATES // 3 and _proj_width(d) % 1024 == 0

    tm = _pick(t, (512, 256, 128))
    tf = _pick(d_ff, (512, 256, 128))
    tn_proj = 1024
    tn_merge = 512
    bq = _pick(s, (256, 128))
    bk = _pick(s, (512, 256, 128))
    rb = 64

    cos_t, sin_t = _rope_tables(positions, tm=_pick(s, (2048, 1024, 512, 256, 128)))
    consts = _retention_constants()
    x2 = x.reshape(t, d)
    for l in range(depth):
        x2 = _ffn(x2, ffn1_norm[l], ffn1_w_gate[l].astype(BF16), ffn1_w_up[l].astype(BF16),
                  ffn1_w_down[l].astype(BF16), tm=tm, tf=tf)
        p2 = _proj(x2, mix_norm[l], _relayout_w_in(w_in[l], d), tm=tm, tn=tn_proj)
        p3 = p2.reshape(b, s, -1)
        o_r, o_s = _local(p3, cos_t, sin_t, ret_norm[l], sg_norm[l], sg_w[l], sg_b[l], consts)
        o_a = _attn(p3, positions, rel_table, bq=bq, bk=bk, rb=rb)
        x2 = _merge(o_r.reshape(t, -1), o_a.reshape(t, -1), o_s.reshape(t, -1), p2,
                    w_br_ret[l].astype(BF16), w_br_att[l].astype(BF16), w_br_sg[l].astype(BF16),
                    w_out[l].astype(BF16), x2, tm=tm, tn=tn_merge)
        x2 = _ffn(x2, ffn2_norm[l], ffn2_w_gate[l].astype(BF16), ffn2_w_up[l].astype(BF16),
                  ffn2_w_down[l].astype(BF16), final_norm if l == depth - 1 else None, tm=tm, tf=tf)
    return x2.reshape(b, s, d)
```

```python
import functools
import math

import numpy as np
import jax
import jax.numpy as jnp
from jax import lax
from jax.experimental import pallas as pl
from jax.experimental.pallas import tpu as pltpu

F32 = jnp.float32
BF16 = jnp.bfloat16
I32 = jnp.int32
I16 = jnp.int16

RET_HEADS, RET_QK_DIM, RET_V_DIM, RET_CHUNK = 6, 64, 128, 128
ATT_HEADS, ATT_HEAD_DIM = 6, 128
IDX_HEADS, IDX_DIM = 4, 64
TOPK_MAX = 256
SG_GROUPS, SG_GROUP_DIM, SG_CHUNK = 4, 128, 128
REL_BUCKETS, REL_MAX_DIST = 32, 128
ROPE_BASE = 10000.0
EPS = 1e-6

RET_QK_W = RET_HEADS * RET_QK_DIM
RET_V_W = RET_HEADS * RET_V_DIM
ATT_W = ATT_HEADS * ATT_HEAD_DIM
IDX_Q_W = IDX_HEADS * IDX_DIM
SG_W = SG_GROUPS * SG_GROUP_DIM

LANES = 128
VMEM_LIMIT_BYTES = 56 * 1024 * 1024

PROJ_TILE = 512
LOC_QR, LOC_KR, LOC_VR, LOC_GR, LOC_US, LOC_VS = 0, 384, 768, 1536, 2304, 2816
LOC_W = 3584
PQ_QA, PQ_QI, PQ_KIW = 0, 768, 1024
PQ_W = 1536
KV_K, KV_V = 0, 768
KV_W = 1536

I16_MIN = -32768
I16_MAX = 32767
INT_MIN = np.int32(-2 ** 31)
INT_MAX = np.int32(2 ** 31 - 1)
NEG_INF = float("-inf")
LOG2E = math.log2(math.e)


def _cparams(sem):
    return pltpu.CompilerParams(dimension_semantics=sem, vmem_limit_bytes=VMEM_LIMIT_BYTES)


def _rms(x, g):
    return x * lax.rsqrt(jnp.mean(x * x, axis=-1, keepdims=True) + EPS) * g


def _ffn_kernel(x_ref, g_ref, wg_ref, wu_ref, wd_ref, *rest, final_norm):
    if final_norm:
        fg_ref, o_ref, xn_ref = rest
    else:
        o_ref, xn_ref = rest
    j = pl.program_id(1)

    @pl.when(j == 0)
    def _():
        xn_ref[...] = _rms(x_ref[...], g_ref[...]).astype(BF16)
        o_ref[...] = jnp.zeros_like(o_ref)

    xn = xn_ref[...]
    a = jnp.dot(xn, wg_ref[...], preferred_element_type=F32)
    b = jnp.dot(xn, wu_ref[...], preferred_element_type=F32)
    h = (a * jax.nn.sigmoid(a) * b).astype(BF16)
    o_ref[...] += jnp.dot(h, wd_ref[...], preferred_element_type=F32)

    @pl.when(j == pl.num_programs(1) - 1)
    def _():
        y = x_ref[...] + 0.5 * o_ref[...]
        if final_norm:
            y = _rms(y, fg_ref[...])
        o_ref[...] = y


def _ffn(x2, gain, wg, wu, wd, final_gain=None, *, tm, tf):
    t, d = x2.shape
    f = wg.shape[1]
    final_norm = final_gain is not None
    in_specs = [
        pl.BlockSpec((tm, d), lambda i, j: (i, 0)),
        pl.BlockSpec((1, d), lambda i, j: (0, 0)),
        pl.BlockSpec((d, tf), lambda i, j: (0, j)),
        pl.BlockSpec((d, tf), lambda i, j: (0, j)),
        pl.BlockSpec((tf, d), lambda i, j: (j, 0)),
    ]
    args = [x2, gain.reshape(1, d), wg, wu, wd]
    if final_norm:
        in_specs.append(pl.BlockSpec((1, d), lambda i, j: (0, 0)))
        args.append(final_gain.reshape(1, d))
    return pl.pallas_call(
        functools.partial(_ffn_kernel, final_norm=final_norm),
        out_shape=jax.ShapeDtypeStruct((t, d), F32),
        grid=(t // tm, f // tf),
        in_specs=in_specs,
        out_specs=pl.BlockSpec((tm, d), lambda i, j: (i, 0)),
        scratch_shapes=[pltpu.VMEM((tm, d), BF16)],
        compiler_params=_cparams(("parallel", "arbitrary")),
        name="ffn",
    )(*args)


def _proj_kernel(x_ref, g_ref, w_ref, ol_ref, oq_ref, okv_ref, og_ref, xn_ref, *, t_loc, t_q, t_kv):
    j = pl.program_id(1)

    @pl.when(j == 0)
    def _():
        xn_ref[...] = _rms(x_ref[...], g_ref[...]).astype(BF16)

    y = jnp.dot(xn_ref[...], w_ref[...], preferred_element_type=F32).astype(BF16)

    @pl.when(j < t_loc)
    def _():
        ol_ref[...] = y

    @pl.when(jnp.logical_and(j >= t_loc, j < t_loc + t_q))
    def _():
        oq_ref[...] = y

    @pl.when(jnp.logical_and(j >= t_loc + t_q, j < t_loc + t_q + t_kv))
    def _():
        okv_ref[...] = y

    @pl.when(j >= t_loc + t_q + t_kv)
    def _():
        og_ref[0] = y


def _proj(x2, gain, w, *, tm):
    t, d = x2.shape
    tn = PROJ_TILE
    n_tiles = w.shape[1] // tn
    t_loc, t_q, t_kv = LOC_W // tn, PQ_W // tn, KV_W // tn
    t_g = n_tiles - t_loc - t_q - t_kv

    def clampspec(first, count):
        return pl.BlockSpec((tm, tn), lambda i, j: (i, jnp.clip(j - first, 0, count - 1)))

    return pl.pallas_call(
        functools.partial(_proj_kernel, t_loc=t_loc, t_q=t_q, t_kv=t_kv),
        out_shape=(jax.ShapeDtypeStruct((t, LOC_W), BF16),
                   jax.ShapeDtypeStruct((t, PQ_W), BF16),
                   jax.ShapeDtypeStruct((t, KV_W), BF16),
                   jax.ShapeDtypeStruct((t_g, t, tn), BF16)),
        grid=(t // tm, n_tiles),
        in_specs=[
            pl.BlockSpec((tm, d), lambda i, j: (i, 0)),
            pl.BlockSpec((1, d), lambda i, j: (0, 0)),
            pl.BlockSpec((d, tn), lambda i, j: (0, j)),
        ],
        out_specs=(clampspec(0, t_loc), clampspec(t_loc, t_q), clampspec(t_loc + t_q, t_kv),
                   pl.BlockSpec((1, tm, tn),
                                lambda i, j: (jnp.clip(j - (t_loc + t_q + t_kv), 0, t_g - 1), i, 0))),
        scratch_shapes=[pltpu.VMEM((tm, d), BF16)],
        compiler_params=_cparams(("arbitrary", "arbitrary")),
        name="proj",
    )(x2, gain.reshape(1, d), w)


def _rope_table_kernel(pos_ref, invf_ref, sign_ref, c_ref, s_ref):
    ang = pos_ref[0].astype(F32) * invf_ref[...]
    c_ref[0] = jnp.cos(ang)
    s_ref[0] = jnp.sin(ang) * sign_ref[...]


def _rope_tables(positions, *, tm):
    b, s = positions.shape
    half = RET_QK_DIM // 2
    inv_freq = ROPE_BASE ** (-jnp.arange(0, RET_QK_DIM, 2, dtype=F32) / RET_QK_DIM)
    lane = np.arange(LANES)
    invf = inv_freq[lane % half].reshape(1, LANES)
    sign = jnp.asarray(np.where(lane % RET_QK_DIM < half, -1.0, 1.0), F32).reshape(1, LANES)
    out = jax.ShapeDtypeStruct((b, s, LANES), F32)
    return pl.pallas_call(
        _rope_table_kernel,
        out_shape=(out, out),
        grid=(b, s // tm),
        in_specs=[
            pl.BlockSpec((1, tm, 1), lambda i, j: (i, j, 0)),
            pl.BlockSpec((1, LANES), lambda i, j: (0, 0)),
            pl.BlockSpec((1, LANES), lambda i, j: (0, 0)),
        ],
        out_specs=(pl.BlockSpec((1, tm, LANES), lambda i, j: (i, j, 0)),
                   pl.BlockSpec((1, tm, LANES), lambda i, j: (i, j, 0))),
        compiler_params=_cparams(("parallel", "parallel")),
        name="rope_tables",
    )(positions.reshape(b, s, 1), invf, sign)


def _retention_constants():
    c = RET_CHUNK
    h = np.arange(RET_HEADS, dtype=np.float64)
    log_g = np.log(1.0 - 2.0 ** (-5.0 - h))
    idx = np.arange(c, dtype=np.float64)
    rel = idx[:, None] - idx[None, :]
    dmat = np.where(rel >= 0, np.exp(np.maximum(rel, 0.0) * log_g[:, None, None]), 0.0)
    k_decay = np.exp((c - 1 - idx)[None, :] * log_g[:, None])
    q_decay = np.exp((idx + 1)[None, :] * log_g[:, None])
    chunk_decay = np.exp(c * log_g)
    kd = np.repeat(k_decay.T, RET_QK_DIM, axis=1)
    qd = np.repeat(q_decay.T, RET_QK_DIM, axis=1) * RET_QK_DIM ** -0.5
    lane = np.arange(RET_QK_W)
    src = (lane // RET_QK_DIM) * RET_QK_DIM + (lane % RET_QK_DIM + RET_QK_DIM // 2) % RET_QK_DIM
    perm = np.zeros((RET_QK_W, RET_QK_W), np.float32)
    perm[src, lane] = 1.0
    return (jnp.asarray(dmat, F32), jnp.asarray(qd, F32), jnp.asarray(kd, F32),
            [float(np.float32(v)) for v in chunk_decay], jnp.asarray(perm, BF16))


def _gelu_tanh(x):
    return 0.5 * x * (1.0 + jnp.tanh(math.sqrt(2.0 / math.pi) * (x + 0.044715 * (x * x * x))))


def _local_kernel(p_ref, cos_ref, sin_ref,
                  perm_ref, dmat_ref, qd_ref, kd_ref, rn_ref, sn_ref, sgw_ref, sgb_ref,
                  or_ref, os_ref, state_ref, *, chunk_decay):
    @pl.when(pl.program_id(1) == 0)
    def _():
        state_ref[...] = jnp.zeros_like(state_ref)

    n_rep = RET_QK_W // LANES
    cos3 = jnp.concatenate([cos_ref[0]] * n_rep, axis=1)
    sin3 = jnp.concatenate([sin_ref[0]] * n_rep, axis=1)
    q = p_ref[0, :, LOC_QR:LOC_QR + RET_QK_W]
    k = p_ref[0, :, LOC_KR:LOC_KR + RET_QK_W]
    perm = perm_ref[...]
    qr = q.astype(F32) * cos3 + jnp.dot(q, perm, preferred_element_type=F32) * sin3
    kr = k.astype(F32) * cos3 + jnp.dot(k, perm, preferred_element_type=F32) * sin3
    q_in = (qr * (RET_QK_DIM ** -0.5)).astype(BF16)
    q_cr = (qr * qd_ref[...]).astype(BF16)
    k_b = kr.astype(BF16)
    k_dec_t = jnp.transpose(kr * kd_ref[...]).astype(BF16)
    v = p_ref[0, :, LOC_VR:LOC_VR + RET_V_W]
    outs = []
    for h in range(RET_HEADS):
        qs = slice(h * RET_QK_DIM, (h + 1) * RET_QK_DIM)
        vh = v[:, h * RET_V_DIM:(h + 1) * RET_V_DIM]
        a = lax.dot_general(q_in[:, qs], k_b[:, qs], (((1,), (1,)), ((), ())),
                            preferred_element_type=F32) * dmat_ref[h]
        prev = state_ref[h]
        o = (jnp.dot(a.astype(BF16), vh, preferred_element_type=F32)
             + jnp.dot(q_cr[:, qs], prev.astype(BF16), preferred_element_type=F32))
        state_ref[h] = prev * chunk_decay[h] + jnp.dot(k_dec_t[qs, :], vh, preferred_element_type=F32)
        mu = jnp.mean(o, axis=-1, keepdims=True)
        d = o - mu
        var = jnp.mean(d * d, axis=-1, keepdims=True)
        outs.append(d * lax.rsqrt(var + EPS))
    g = p_ref[0, :, LOC_GR:LOC_GR + RET_V_W].astype(F32)
    y = jnp.concatenate(outs, axis=1) * rn_ref[...]
    or_ref[0] = (y * (g * jax.nn.sigmoid(g))).astype(BF16)

    u = _gelu_tanh(p_ref[0, :, LOC_US:LOC_US + SG_W].astype(F32))
    vn = _rms(_gelu_tanh(p_ref[0, :, LOC_VS:LOC_VS + SG_W].astype(F32)), sn_ref[...])
    vn_b = vn.astype(BF16)
    c = SG_CHUNK
    tril = lax.broadcasted_iota(I32, (c, c), 0) >= lax.broadcasted_iota(I32, (c, c), 1)
    bias = sgb_ref[...]
    mixed = []
    for gi in range(SG_GROUPS):
        w = jnp.where(tril, sgw_ref[gi], jnp.zeros((), BF16))
        m = jnp.dot(w, vn_b[:, gi * SG_GROUP_DIM:(gi + 1) * SG_GROUP_DIM], preferred_element_type=F32)
        mixed.append(m + bias[:, gi:gi + 1])
    os_ref[0] = (u * jnp.concatenate(mixed, axis=1)).astype(BF16)


def _local(p_loc3, cos_t, sin_t, ret_norm, sg_norm, sg_w, sg_b, consts):
    b, s, _ = p_loc3.shape
    c = RET_CHUNK
    dmat, qd, kd, chunk_decay, perm = consts

    def whole(arr):
        nd = arr.ndim
        return pl.BlockSpec(arr.shape, lambda i, j: (0,) * nd)

    rn = ret_norm.reshape(1, RET_V_W)
    sn = sg_norm.reshape(1, SG_W)
    sgw = sg_w.astype(BF16)
    sgb_t = jnp.transpose(sg_b)
    tab = pl.BlockSpec((1, c, LANES), lambda i, j: (i, j, 0))
    return pl.pallas_call(
        functools.partial(_local_kernel, chunk_decay=chunk_decay),
        out_shape=(jax.ShapeDtypeStruct((b, s, RET_V_W), BF16),
                   jax.ShapeDtypeStruct((b, s, SG_W), BF16)),
        grid=(b, s // c),
        in_specs=[
            pl.BlockSpec((1, c, LOC_W), lambda i, j: (i, j, 0)),
            tab, tab,
            whole(perm), whole(dmat), whole(qd), whole(kd), whole(rn), whole(sn), whole(sgw), whole(sgb_t),
        ],
        out_specs=(pl.BlockSpec((1, c, RET_V_W), lambda i, j: (i, j, 0)),
                   pl.BlockSpec((1, c, SG_W), lambda i, j: (i, j, 0))),
        scratch_shapes=[pltpu.VMEM((RET_HEADS, RET_QK_DIM, RET_V_DIM), F32)],
        compiler_params=_cparams(("parallel", "arbitrary")),
        name="local_mixers",
    )(p_loc3, cos_t, sin_t, perm, dmat, qd, kd, rn, sn, sgw, sgb_t)


def _t5_bucket_table():
    max_exact = REL_BUCKETS // 2
    d = np.arange(REL_MAX_DIST)
    df = np.maximum(d, 1).astype(np.float32)
    large = max_exact + (np.log(df / max_exact) / np.float32(math.log(REL_MAX_DIST / max_exact))
                         * (REL_BUCKETS - max_exact)).astype(np.int32)
    large = np.minimum(large, REL_BUCKETS - 1)
    return np.where(d < max_exact, d, large)


def _bucket_starts():
    bk = _t5_bucket_table()
    assert np.all(np.diff(bk) >= 0) and bk[-1] == REL_BUCKETS - 1
    return [(int(np.argmax(bk == v)), int(v)) for v in sorted(set(bk.tolist())) if v > 0]


def _attn_kernel(tq_ref, tc_ref, qmin_ref, kmax_ref, qmin_sub_ref, kmax_sub_ref, relb_ref,
                 pq_ref, kiw_ref, kv_ref, posq_ref, posk_ref,
                 o_ref, hi_ref, lo_ref, jst_ref, m_ref, l_ref, acc_ref, bias_ref,
                 *, bq, bk, n_keep, seq, far_dist, bucket_starts):
    b = pl.program_id(0)
    step = pl.program_id(1)
    qi = tq_ref[step]
    c = tc_ref[step]
    n_chunks = (qi * bq + bq - 1) // bk + 1
    nlc = bk // LANES
    nsub_q, nsub_k = bq // LANES, bk // LANES
    k_f = float(n_keep)
    one16 = jnp.ones((), I16)
    zero16 = jnp.zeros((), I16)

    def tile_lanes(x):
        return jnp.concatenate([x] * nlc, axis=1)

    def lane_chunk(x, j):
        return x[:, j * LANES:(j + 1) * LANES]

    @pl.when(c == 0)
    def _prologue():
        qiv = pq_ref[0, :, PQ_QI:PQ_QI + IDX_Q_W]
        zpad = jnp.zeros((bq, LANES - IDX_DIM), BF16)
        q_heads = [jnp.concatenate([qiv[:, h * IDX_DIM:(h + 1) * IDX_DIM], zpad], axis=1)
                   for h in range(IDX_HEADS)]
        kiw_q = pq_ref[0, :, PQ_KIW:PQ_KIW + LANES]
        w = kiw_q[:, IDX_DIM:IDX_DIM + IDX_HEADS].astype(F32)
        w_bc = [jnp.broadcast_to(w[:, h:h + 1], (bq, bk)) for h in range(IDX_HEADS)]
        row_t = qi * bq + lax.broadcasted_iota(I32, (bq, bk), 0)
        lane_s = lax.broadcasted_iota(I32, (bq, bk), 1)

        def score_body(cc, carry):
            off = pl.multiple_of(cc * bk, bk)
            kc = kiw_ref[0, pl.ds(off, bk), :]
            sc = jnp.zeros((bq, bk), F32)
            for h in range(IDX_HEADS):
                sh = lax.dot_general(q_heads[h], kc, (((1,), (1,)), ((), ())), preferred_element_type=F32)
                sc = sc + w_bc[h] * jnp.maximum(sh, 0.0)
            sc = jnp.where(sc == 0.0, 0.0, sc)
            bits = lax.bitcast_convert_type(sc, I32)
            key = jnp.where(bits < 0, bits ^ INT_MAX, bits)
            key = jnp.where(cc * bk + lane_s <= row_t, key, INT_MIN)
            hi_ref[cc] = (key >> 16).astype(I16)
            lo_ref[cc] = ((key & 0xFFFF) - 32768).astype(I16)
            return carry

        lax.fori_loop(0, n_chunks, score_body, 0)

        def count(ref, pred):
            def body(cc, acc):
                t = ref[cc]
                for j in range(nlc):
                    acc = acc + jnp.where(pred(lane_chunk(t, j)), one16, zero16)
                return acc
            acc = lax.fori_loop(0, n_chunks, body, jnp.zeros((bq, LANES), I16))
            return jnp.broadcast_to(jnp.sum(acc.astype(F32), axis=1, keepdims=True), (bq, LANES))

        def digit_select(ref, want):
            def bis(_, lohi):
                lo, hi = lohi
                mid = (lo + hi) >> 1
                mid16 = mid.astype(I16)
                ok = count(ref, lambda t: t >= mid16) >= want
                return jnp.where(ok, mid, lo), jnp.where(ok, hi, mid)
            lo, _ = lax.fori_loop(0, 16, bis, (jnp.full((bq, LANES), I16_MIN, I32),
                                               jnp.full((bq, LANES), I16_MAX + 1, I32)))
            return lo

        tau_h = digit_select(hi_ref, k_f)
        th16 = tau_h.astype(I16)
        c_above = count(hi_ref, lambda t: t > th16)
        c_hge = count(hi_ref, lambda t: t >= th16)
        th_t = tile_lanes(th16)

        def mask_low(cc, carry):
            lo_ref[cc] = jnp.where(hi_ref[cc] == th_t, lo_ref[cc], jnp.full((), I16_MIN, I16))
            return carry

        lax.fori_loop(0, n_chunks, mask_low, 0)
        tau_l = digit_select(lo_ref, k_f - c_above)
        tl16 = tau_l.astype(I16)
        c_lgt = count(lo_ref, lambda t: t > tl16)
        c_lge = count(lo_ref, lambda t: t >= tl16)
        bucket_ge = jnp.where(tau_l == I16_MIN, c_hge - c_above, c_lge)
        c_gt = c_above + c_lgt
        c_ge = c_above + bucket_ge
        is_min = jnp.logical_and(tau_h == I16_MIN, tau_l == I16_MIN)
        need = k_f - c_gt
        excess = jnp.logical_and(c_ge > k_f, jnp.logical_not(is_min))
        jst_ref[...] = jnp.where(is_min, -1, seq).astype(I32)
        lane16 = lax.broadcasted_iota(I16, (bq, LANES), 1)

        def tie_limit(jmax, cc, j):
            return jnp.clip(jmax - (cc * bk + j * LANES), -1, I16_MAX).astype(I16)

        @pl.when(jnp.max(jnp.where(excess, 1.0, 0.0)) > 0.0)
        def _ties():
            def count_tie_le(jmax):
                def body(cc, acc):
                    h = hi_ref[cc]
                    l = lo_ref[cc]
                    for j in range(nlc):
                        hit = jnp.logical_and(
                            jnp.logical_and(lane_chunk(h, j) == th16, lane_chunk(l, j) == tl16),
                            lane16 <= tie_limit(jmax, cc, j))
                        acc = acc + jnp.where(hit, one16, zero16)
                    return acc
                acc = lax.fori_loop(0, n_chunks, body, jnp.zeros((bq, LANES), I16))
                return jnp.broadcast_to(jnp.sum(acc.astype(F32), axis=1, keepdims=True), (bq, LANES))

            def bis_j(_, lohi):
                jl, jh = lohi
                mid = (jl + jh) >> 1
                ok = count_tie_le(mid) >= need
                return jnp.where(ok, jl, mid), jnp.where(ok, mid, jh)

            n_it = max(1, int(math.ceil(math.log2(seq + 1))))
            _, jh = lax.fori_loop(0, n_it, bis_j, (jnp.full((bq, LANES), -1, I32),
                                                   jnp.full((bq, LANES), seq - 1, I32)))
            jst_ref[...] = jnp.where(excess, jh, jst_ref[...])

        jst = jst_ref[...]

        def build_mask(cc, carry):
            h = hi_ref[cc]
            l = lo_ref[cc]
            cols = []
            for j in range(nlc):
                hj, lj = lane_chunk(h, j), lane_chunk(l, j)
                tie_ok = lane16 <= tie_limit(jst, cc, j)
                in_bucket = jnp.logical_or(lj > tl16, jnp.logical_and(lj == tl16, tie_ok))
                sel = jnp.logical_or(hj > th16, jnp.logical_and(hj == th16, in_bucket))
                cols.append(jnp.where(sel, one16, zero16))
            hi_ref[cc] = jnp.concatenate(cols, axis=1)
            return carry

        lax.fori_loop(0, n_chunks, build_mask, 0)

        m_ref[...] = jnp.full(m_ref.shape, NEG_INF, F32)
        l_ref[...] = jnp.zeros_like(l_ref)
        acc_ref[...] = jnp.zeros_like(acc_ref)

    sel = hi_ref[c].astype(I32) != 0
    c1 = (ATT_HEAD_DIM ** -0.5) * LOG2E
    q = pq_ref[0, :, PQ_QA:PQ_QA + ATT_W]
    kk = kv_ref[0, :, KV_K:KV_K + ATT_W]
    vv = kv_ref[0, :, KV_V:KV_V + ATT_W]

    def heads(near):
        for h in range(ATT_HEADS):
            hs = slice(h * ATT_HEAD_DIM, (h + 1) * ATT_HEAD_DIM)
            s = lax.dot_general(q[:, hs], kk[:, hs], (((1,), (1,)), ((), ())), preferred_element_type=F32)
            if near:
                s = jnp.where(sel, s * c1 + bias_ref[h], NEG_INF)
                mx = jnp.max(s, axis=1, keepdims=True)
            else:
                s = jnp.where(sel, s, NEG_INF)
                mx = jnp.max(s, axis=1, keepdims=True) * c1
            m_old = m_ref[h]
            m_new = jnp.maximum(m_old, mx)
            m_safe = jnp.where(m_new == NEG_INF, 0.0, m_new)
            alpha = jnp.exp2(m_old - m_safe)
            if near:
                p = jnp.exp2(s - tile_lanes(m_safe))
            else:
                p = jnp.exp2(s * c1 - tile_lanes(m_safe))
            l_ref[h] = alpha * l_ref[h] + jnp.sum(p, axis=1, keepdims=True)
            acc_ref[h] = alpha * acc_ref[h] + jnp.dot(p.astype(BF16), vv[:, hs], preferred_element_type=F32)
            m_ref[h] = m_new

    far = qmin_ref[b * (seq // bq) + qi] - kmax_ref[b * (seq // bk) + c] >= far_dist

    @pl.when(far)
    def _():
        heads(False)

    @pl.when(jnp.logical_not(far))
    def _():
        nsub = seq // LANES
        for r in range(nsub_q):
            for j in range(nsub_k):
                rows = slice(r * LANES, (r + 1) * LANES)
                cols = slice(j * LANES, (j + 1) * LANES)
                sub_far = (qmin_sub_ref[b * nsub + qi * nsub_q + r]
                           - kmax_sub_ref[b * nsub + c * nsub_k + j]) >= far_dist

                @pl.when(sub_far)
                def _():
                    for h in range(ATT_HEADS):
                        bias_ref[h, rows, cols] = jnp.zeros((LANES, LANES), F32)

                @pl.when(jnp.logical_not(sub_far))
                def _():
                    half = LANES // 2
                    for hr in range(2):
                        rr = slice(r * LANES + hr * half, r * LANES + (hr + 1) * half)
                        dist = posq_ref[0, rr, :] - posk_ref[0, 0, :, cols]
                        bias = [jnp.full((half, LANES), relb_ref[h], F32) for h in range(ATT_HEADS)]
                        for start, bucket in bucket_starts:
                            ge = dist >= start
                            for h in range(ATT_HEADS):
                                bias[h] = jnp.where(ge, relb_ref[bucket * ATT_HEADS + h], bias[h])
                        for h in range(ATT_HEADS):
                            bias_ref[h, rr, cols] = bias[h]

        heads(True)

    @pl.when(c == n_chunks - 1)
    def _():
        o_ref[0] = jnp.concatenate([acc_ref[h] / l_ref[h] for h in range(ATT_HEADS)], axis=1).astype(BF16)


def _attn(p_q3, p_kv3, positions, rel_table, *, bq, bk):
    b, s, _ = p_q3.shape
    assert s <= I16_MAX and bq % LANES == 0 and bk % LANES == 0
    n_keep = min(TOPK_MAX, s // 4)
    nq, nkc, nsub = s // bq, s // bk, s // LANES
    tq, tc = [], []
    for qi in range(nq):
        for c in range((qi * bq + bq - 1) // bk + 1):
            tq.append(qi)
            tc.append(c)
    n_steps = len(tq)
    tq = jnp.asarray(np.asarray(tq, np.int32))
    tc = jnp.asarray(np.asarray(tc, np.int32))
    qmin = jnp.min(positions.reshape(b, nq, bq), axis=-1).reshape(-1)
    kmax = jnp.max(positions.reshape(b, nkc, bk), axis=-1).reshape(-1)
    qmin_sub = jnp.min(positions.reshape(b, nsub, LANES), axis=-1).reshape(-1)
    kmax_sub = jnp.max(positions.reshape(b, nsub, LANES), axis=-1).reshape(-1)
    starts = _bucket_starts()
    far_dist = REL_MAX_DIST - 1
    assert starts[-1][0] <= far_dist and starts[-1][1] == REL_BUCKETS - 1
    relb = ((rel_table - rel_table[REL_BUCKETS - 1:REL_BUCKETS, :]) * LOG2E).astype(F32).reshape(-1)

    grid_spec = pltpu.PrefetchScalarGridSpec(
        num_scalar_prefetch=7,
        grid=(b, n_steps),
        in_specs=[
            pl.BlockSpec((1, bq, PQ_W), lambda i, t, tq, tc, *_: (i, tq[t], 0)),
            pl.BlockSpec((1, s, LANES), lambda i, t, *_: (i, 0, PQ_KIW // LANES)),
            pl.BlockSpec((1, bk, KV_W), lambda i, t, tq, tc, *_: (i, tc[t], 0)),
            pl.BlockSpec((1, bq, 1), lambda i, t, tq, tc, *_: (i, tq[t], 0)),
            pl.BlockSpec((1, 1, 1, bk), lambda i, t, tq, tc, *_: (i, tc[t], 0, 0)),
        ],
        out_specs=pl.BlockSpec((1, bq, ATT_W), lambda i, t, tq, tc, *_: (i, tq[t], 0)),
        scratch_shapes=[
            pltpu.VMEM((nkc, bq, bk), I16),
            pltpu.VMEM((nkc, bq, bk), I16),
            pltpu.VMEM((bq, LANES), I32),
            pltpu.VMEM((ATT_HEADS, bq, LANES), F32),
            pltpu.VMEM((ATT_HEADS, bq, LANES), F32),
            pltpu.VMEM((ATT_HEADS, bq, ATT_HEAD_DIM), F32),
            pltpu.VMEM((ATT_HEADS, bq, bk), F32),
        ],
    )
    kern = functools.partial(_attn_kernel, bq=bq, bk=bk, n_keep=n_keep, seq=s,
                             far_dist=far_dist, bucket_starts=starts)
    return pl.pallas_call(
        kern,
        out_shape=jax.ShapeDtypeStruct((b, s, ATT_W), BF16),
        grid_spec=grid_spec,
        compiler_params=_cparams(("parallel", "arbitrary")),
        name="sparse_attn",
    )(tq, tc, qmin, kmax, qmin_sub, kmax_sub, relb,
      p_q3, p_q3, p_kv3, positions.reshape(b, s, 1), positions.reshape(b, nkc, 1, bk))


def _merge_kernel(or_ref, oa_ref, os_ref, gr_ref, ga_ref, gs_ref, wr_ref, wa_ref, ws_ref, wo_ref, x_ref, o_ref):
    j = pl.program_id(1)

    @pl.when(j == 0)
    def _():
        o_ref[...] = jnp.zeros_like(o_ref)

    def branch(o, w, g):
        return jax.nn.sigmoid(g[0].astype(F32)) * jnp.dot(o[...], w[...], preferred_element_type=F32)

    merged = branch(or_ref, wr_ref, gr_ref) + branch(oa_ref, wa_ref, ga_ref) + branch(os_ref, ws_ref, gs_ref)
    o_ref[...] += jnp.dot(merged.astype(BF16), wo_ref[...], preferred_element_type=F32)

    @pl.when(j == pl.num_programs(1) - 1)
    def _():
        o_ref[...] = x_ref[...] + o_ref[...]


def _merge(o_r, o_a, o_s, p_g, wr, wa, ws, wo, x2, *, tm):
    t, d = x2.shape
    tn = PROJ_TILE
    per_gate = d // tn

    def gspec(g):
        return pl.BlockSpec((1, tm, tn), lambda i, j: (g * per_gate + j, i, 0))

    return pl.pallas_call(
        _merge_kernel,
        out_shape=jax.ShapeDtypeStruct((t, d), F32),
        grid=(t // tm, per_gate),
        in_specs=[
            pl.BlockSpec((tm, RET_V_W), lambda i, j: (i, 0)),
            pl.BlockSpec((tm, ATT_W), lambda i, j: (i, 0)),
            pl.BlockSpec((tm, SG_W), lambda i, j: (i, 0)),
            gspec(0), gspec(1), gspec(2),
            pl.BlockSpec((RET_V_W, tn), lambda i, j: (0, j)),
            pl.BlockSpec((ATT_W, tn), lambda i, j: (0, j)),
            pl.BlockSpec((SG_W, tn), lambda i, j: (0, j)),
            pl.BlockSpec((tn, d), lambda i, j: (j, 0)),
            pl.BlockSpec((tm, d), lambda i, j: (i, 0)),
        ],
        out_specs=pl.BlockSpec((tm, d), lambda i, j: (i, 0)),
        compiler_params=_cparams(("parallel", "arbitrary")),
        name="merge",
    )(o_r, o_a, o_s, p_g, p_g, p_g, wr, wa, ws, wo, x2)


def _relayout_w_in(w, d_model):
    sizes = (RET_QK_W, RET_QK_W, RET_V_W, RET_V_W, ATT_W, ATT_W, ATT_W, IDX_Q_W, IDX_DIM, IDX_HEADS,
             SG_W, SG_W, d_model, d_model, d_model)
    offs = np.concatenate([[0], np.cumsum(sizes)])
    (q_r, k_r, v_r, g_r, q_a, k_a, v_a, q_i, k_i, w_i, u_s, v_s, gt_r, gt_a, gt_s) = [
        w[:, offs[n]:offs[n + 1]] for n in range(len(sizes))]
    rows = w.shape[0]

    def zeros(n):
        return jnp.zeros((rows, n), w.dtype)

    loc = [q_r, k_r, v_r, g_r, u_s, v_s]
    loc.append(zeros(LOC_W - sum(a.shape[1] for a in loc)))
    pq = [q_a, q_i, k_i, w_i]
    pq.append(zeros(PQ_W - sum(a.shape[1] for a in pq)))
    out = jnp.concatenate(loc + pq + [k_a, v_a, gt_r, gt_a, gt_s], axis=1).astype(BF16)
    assert out.shape[1] == LOC_W + PQ_W + KV_W + 3 * d_model
    return out


def _pick(n, prefs):
    for p in prefs:
        if n % p == 0:
            return p
    return n


def kernel(x, positions, rel_table, ffn1_norm, ffn1_w_gate, ffn1_w_up, ffn1_w_down, mix_norm, w_in, ret_norm, sg_norm, sg_w, sg_b, w_br_ret, w_br_att, w_br_sg, w_out, ffn2_norm, ffn2_w_gate, ffn2_w_up, ffn2_w_down, final_norm):
    b, s, d = x.shape
    t = b * s
    depth = w_in.shape[0]
    d_ff = ffn1_w_gate.shape[2]
    assert s % RET_CHUNK == 0 and d % PROJ_TILE == 0

    tm = _pick(t, (512, 256, 128))
    tm_proj = _pick(t, (1024, 512, 256, 128))
    tf = _pick(d_ff, (512, 256, 128))
    bq = _pick(s, (256, 128))
    bk = _pick(s, (512, 256, 128))

    cos_t, sin_t = _rope_tables(positions, tm=_pick(s, (2048, 1024, 512, 256, 128)))
    consts = _retention_constants()
    x2 = x.reshape(t, d)
    for l in range(depth):
        x2 = _ffn(x2, ffn1_norm[l], ffn1_w_gate[l].astype(BF16), ffn1_w_up[l].astype(BF16),
                  ffn1_w_down[l].astype(BF16), tm=tm, tf=tf)
        p_loc, p_q, p_kv, p_g = _proj(x2, mix_norm[l], _relayout_w_in(w_in[l], d), tm=tm_proj)
        o_r, o_s = _local(p_loc.reshape(b, s, -1), cos_t, sin_t, ret_norm[l], sg_norm[l], sg_w[l], sg_b[l], consts)
        o_a = _attn(p_q.reshape(b, s, -1), p_kv.reshape(b, s, -1), positions, rel_table, bq=bq, bk=bk)
        x2 = _merge(o_r.reshape(t, -1), o_a.reshape(t, -1), o_s.reshape(t, -1), p_g,
                    w_br_ret[l].astype(BF16), w_br_att[l].astype(BF16), w_br_sg[l].astype(BF16),
                    w_out[l].astype(BF16), x2, tm=tm)
        x2 = _ffn(x2, ffn2_norm[l], ffn2_w_gate[l].astype(BF16), ffn2_w_up[l].astype(BF16),
                  ffn2_w_down[l].astype(BF16), final_norm if l == depth - 1 else None, tm=tm, tf=tf)
    return x2.reshape(b, s, d)
```

```python
import functools
import math

import numpy as np
import jax
import jax.numpy as jnp
from jax import lax
from jax.experimental import pallas as pl
from jax.experimental.pallas import tpu as pltpu

F32 = jnp.float32
BF16 = jnp.bfloat16
I32 = jnp.int32

RET_HEADS, RET_QK_DIM, RET_V_DIM, RET_CHUNK = 6, 64, 128, 128
ATT_HEADS, ATT_HEAD_DIM = 6, 128
IDX_HEADS, IDX_DIM = 4, 64
TOPK_MAX = 256
SG_GROUPS, SG_GROUP_DIM, SG_CHUNK = 4, 128, 128
REL_BUCKETS, REL_MAX_DIST = 32, 128
ROPE_BASE = 10000.0
EPS = 1e-6

RET_QK_W = RET_HEADS * RET_QK_DIM
RET_V_W = RET_HEADS * RET_V_DIM
ATT_W = ATT_HEADS * ATT_HEAD_DIM
IDX_Q_W = IDX_HEADS * IDX_DIM
SG_W = SG_GROUPS * SG_GROUP_DIM

LANES = 128
COUNT_ROWS = 64
VMEM_LIMIT_BYTES = 56 * 1024 * 1024

PROJ_TILE = 512
LOC_QR, LOC_KR, LOC_VR, LOC_GR, LOC_US, LOC_VS = 0, 384, 768, 1536, 2304, 2816
LOC_W = 3584
PQ_QA, PQ_QI, PQ_KIW = 0, 768, 1024
PQ_W = 1536
KV_K, KV_V = 0, 768
KV_W = 1536

INT_MIN = np.int32(-2 ** 31)
INT_MAX = np.int32(2 ** 31 - 1)
NEG_INF = float("-inf")
LOG2E = math.log2(math.e)


def _cparams(sem):
    return pltpu.CompilerParams(dimension_semantics=sem, vmem_limit_bytes=VMEM_LIMIT_BYTES)


def _rms(x, g):
    return x * lax.rsqrt(jnp.mean(x * x, axis=-1, keepdims=True) + EPS) * g


def _ffn_kernel(x_ref, g_ref, wg_ref, wu_ref, wd_ref, *rest, final_norm):
    if final_norm:
        fg_ref, o_ref, xn_ref = rest
    else:
        o_ref, xn_ref = rest
    j = pl.program_id(1)

    @pl.when(j == 0)
    def _():
        xn_ref[...] = _rms(x_ref[...], g_ref[...]).astype(BF16)
        o_ref[...] = jnp.zeros_like(o_ref)

    xn = xn_ref[...]
    a = jnp.dot(xn, wg_ref[...], preferred_element_type=F32)
    b = jnp.dot(xn, wu_ref[...], preferred_element_type=F32)
    h = (a * jax.nn.sigmoid(a) * b).astype(BF16)
    o_ref[...] += jnp.dot(h, wd_ref[...], preferred_element_type=F32)

    @pl.when(j == pl.num_programs(1) - 1)
    def _():
        y = x_ref[...] + 0.5 * o_ref[...]
        if final_norm:
            y = _rms(y, fg_ref[...])
        o_ref[...] = y


def _ffn(x2, gain, wg, wu, wd, final_gain=None, *, tm, tf):
    t, d = x2.shape
    f = wg.shape[1]
    final_norm = final_gain is not None
    in_specs = [
        pl.BlockSpec((tm, d), lambda i, j: (i, 0)),
        pl.BlockSpec((1, d), lambda i, j: (0, 0)),
        pl.BlockSpec((d, tf), lambda i, j: (0, j)),
        pl.BlockSpec((d, tf), lambda i, j: (0, j)),
        pl.BlockSpec((tf, d), lambda i, j: (j, 0)),
    ]
    args = [x2, gain.reshape(1, d), wg, wu, wd]
    if final_norm:
        in_specs.append(pl.BlockSpec((1, d), lambda i, j: (0, 0)))
        args.append(final_gain.reshape(1, d))
    return pl.pallas_call(
        functools.partial(_ffn_kernel, final_norm=final_norm),
        out_shape=jax.ShapeDtypeStruct((t, d), F32),
        grid=(t // tm, f // tf),
        in_specs=in_specs,
        out_specs=pl.BlockSpec((tm, d), lambda i, j: (i, 0)),
        scratch_shapes=[pltpu.VMEM((tm, d), BF16)],
        compiler_params=_cparams(("parallel", "arbitrary")),
        name="ffn",
    )(*args)


def _proj_kernel(x_ref, g_ref, w_ref, ol_ref, oq_ref, okv_ref, og_ref, xn_ref, *, t_loc, t_q, t_kv):
    j = pl.program_id(1)

    @pl.when(j == 0)
    def _():
        xn_ref[...] = _rms(x_ref[...], g_ref[...]).astype(BF16)

    y = jnp.dot(xn_ref[...], w_ref[...], preferred_element_type=F32).astype(BF16)

    @pl.when(j < t_loc)
    def _():
        ol_ref[...] = y

    @pl.when(jnp.logical_and(j >= t_loc, j < t_loc + t_q))
    def _():
        oq_ref[...] = y

    @pl.when(jnp.logical_and(j >= t_loc + t_q, j < t_loc + t_q + t_kv))
    def _():
        okv_ref[...] = y

    @pl.when(j >= t_loc + t_q + t_kv)
    def _():
        og_ref[0] = y


def _proj(x2, gain, w, *, tm):
    t, d = x2.shape
    tn = PROJ_TILE
    n_tiles = w.shape[1] // tn
    t_loc, t_q, t_kv = LOC_W // tn, PQ_W // tn, KV_W // tn
    t_g = n_tiles - t_loc - t_q - t_kv

    def clampspec(first, count):
        return pl.BlockSpec((tm, tn), lambda i, j: (i, jnp.clip(j - first, 0, count - 1)))

    return pl.pallas_call(
        functools.partial(_proj_kernel, t_loc=t_loc, t_q=t_q, t_kv=t_kv),
        out_shape=(jax.ShapeDtypeStruct((t, LOC_W), BF16),
                   jax.ShapeDtypeStruct((t, PQ_W), BF16),
                   jax.ShapeDtypeStruct((t, KV_W), BF16),
                   jax.ShapeDtypeStruct((t_g, t, tn), BF16)),
        grid=(t // tm, n_tiles),
        in_specs=[
            pl.BlockSpec((tm, d), lambda i, j: (i, 0)),
            pl.BlockSpec((1, d), lambda i, j: (0, 0)),
            pl.BlockSpec((d, tn), lambda i, j: (0, j)),
        ],
        out_specs=(clampspec(0, t_loc), clampspec(t_loc, t_q), clampspec(t_loc + t_q, t_kv),
                   pl.BlockSpec((1, tm, tn),
                                lambda i, j: (jnp.clip(j - (t_loc + t_q + t_kv), 0, t_g - 1), i, 0))),
        scratch_shapes=[pltpu.VMEM((tm, d), BF16)],
        compiler_params=_cparams(("arbitrary", "arbitrary")),
        name="proj",
    )(x2, gain.reshape(1, d), w)


def _rope_table_kernel(pos_ref, invf_ref, sign_ref, c_ref, s_ref):
    ang = pos_ref[0].astype(F32) * invf_ref[...]
    c_ref[0] = jnp.cos(ang)
    s_ref[0] = jnp.sin(ang) * sign_ref[...]


def _rope_tables(positions, *, tm):
    b, s = positions.shape
    half = RET_QK_DIM // 2
    inv_freq = ROPE_BASE ** (-jnp.arange(0, RET_QK_DIM, 2, dtype=F32) / RET_QK_DIM)
    lane = np.arange(LANES)
    invf = inv_freq[lane % half].reshape(1, LANES)
    sign = jnp.asarray(np.where(lane % RET_QK_DIM < half, -1.0, 1.0), F32).reshape(1, LANES)
    out = jax.ShapeDtypeStruct((b, s, LANES), F32)
    return pl.pallas_call(
        _rope_table_kernel,
        out_shape=(out, out),
        grid=(b, s // tm),
        in_specs=[
            pl.BlockSpec((1, tm, 1), lambda i, j: (i, j, 0)),
            pl.BlockSpec((1, LANES), lambda i, j: (0, 0)),
            pl.BlockSpec((1, LANES), lambda i, j: (0, 0)),
        ],
        out_specs=(pl.BlockSpec((1, tm, LANES), lambda i, j: (i, j, 0)),
                   pl.BlockSpec((1, tm, LANES), lambda i, j: (i, j, 0))),
        compiler_params=_cparams(("parallel", "parallel")),
        name="rope_tables",
    )(positions.reshape(b, s, 1), invf, sign)


def _retention_constants():
    c = RET_CHUNK
    h = np.arange(RET_HEADS, dtype=np.float64)
    log_g = np.log(1.0 - 2.0 ** (-5.0 - h))
    idx = np.arange(c, dtype=np.float64)
    rel = idx[:, None] - idx[None, :]
    dmat = np.where(rel >= 0, np.exp(np.maximum(rel, 0.0) * log_g[:, None, None]), 0.0)
    k_decay = np.exp((c - 1 - idx)[None, :] * log_g[:, None])
    q_decay = np.exp((idx + 1)[None, :] * log_g[:, None])
    chunk_decay = np.exp(c * log_g)
    kd = np.repeat(k_decay.T, RET_QK_DIM, axis=1)
    qd = np.repeat(q_decay.T, RET_QK_DIM, axis=1) * RET_QK_DIM ** -0.5
    lane = np.arange(RET_QK_W)
    src = (lane // RET_QK_DIM) * RET_QK_DIM + (lane % RET_QK_DIM + RET_QK_DIM // 2) % RET_QK_DIM
    perm = np.zeros((RET_QK_W, RET_QK_W), np.float32)
    perm[src, lane] = 1.0
    return (jnp.asarray(dmat, F32), jnp.asarray(qd, F32), jnp.asarray(kd, F32),
            [float(np.float32(v)) for v in chunk_decay], jnp.asarray(perm, BF16))


def _gelu_tanh(x):
    return 0.5 * x * (1.0 + jnp.tanh(math.sqrt(2.0 / math.pi) * (x + 0.044715 * (x * x * x))))


def _local_kernel(p_ref, cos_ref, sin_ref,
                  perm_ref, dmat_ref, qd_ref, kd_ref, rn_ref, sn_ref, sgw_ref, sgb_ref,
                  or_ref, os_ref, state_ref, *, chunk_decay):
    @pl.when(pl.program_id(1) == 0)
    def _():
        state_ref[...] = jnp.zeros_like(state_ref)

    n_rep = RET_QK_W // LANES
    cos3 = jnp.concatenate([cos_ref[0]] * n_rep, axis=1)
    sin3 = jnp.concatenate([sin_ref[0]] * n_rep, axis=1)
    q = p_ref[0, :, LOC_QR:LOC_QR + RET_QK_W]
    k = p_ref[0, :, LOC_KR:LOC_KR + RET_QK_W]
    perm = perm_ref[...]
    qr = q.astype(F32) * cos3 + jnp.dot(q, perm, preferred_element_type=F32) * sin3
    kr = k.astype(F32) * cos3 + jnp.dot(k, perm, preferred_element_type=F32) * sin3
    q_in = (qr * (RET_QK_DIM ** -0.5)).astype(BF16)
    q_cr = (qr * qd_ref[...]).astype(BF16)
    k_b = kr.astype(BF16)
    k_dec_t = jnp.transpose(kr * kd_ref[...]).astype(BF16)
    v = p_ref[0, :, LOC_VR:LOC_VR + RET_V_W]
    outs = []
    for h in range(RET_HEADS):
        qs = slice(h * RET_QK_DIM, (h + 1) * RET_QK_DIM)
        vh = v[:, h * RET_V_DIM:(h + 1) * RET_V_DIM]
        a = lax.dot_general(q_in[:, qs], k_b[:, qs], (((1,), (1,)), ((), ())),
                            preferred_element_type=F32) * dmat_ref[h]
        prev = state_ref[h]
        o = (jnp.dot(a.astype(BF16), vh, preferred_element_type=F32)
             + jnp.dot(q_cr[:, qs], prev.astype(BF16), preferred_element_type=F32))
        state_ref[h] = prev * chunk_decay[h] + jnp.dot(k_dec_t[qs, :], vh, preferred_element_type=F32)
        mu = jnp.mean(o, axis=-1, keepdims=True)
        d = o - mu
        var = jnp.mean(d * d, axis=-1, keepdims=True)
        outs.append(d * lax.rsqrt(var + EPS))
    g = p_ref[0, :, LOC_GR:LOC_GR + RET_V_W].astype(F32)
    y = jnp.concatenate(outs, axis=1) * rn_ref[...]
    or_ref[0] = (y * (g * jax.nn.sigmoid(g))).astype(BF16)

    u = _gelu_tanh(p_ref[0, :, LOC_US:LOC_US + SG_W].astype(F32))
    vn = _rms(_gelu_tanh(p_ref[0, :, LOC_VS:LOC_VS + SG_W].astype(F32)), sn_ref[...])
    vn_b = vn.astype(BF16)
    c = SG_CHUNK
    tril = lax.broadcasted_iota(I32, (c, c), 0) >= lax.broadcasted_iota(I32, (c, c), 1)
    bias = sgb_ref[...]
    mixed = []
    for gi in range(SG_GROUPS):
        w = jnp.where(tril, sgw_ref[gi], jnp.zeros((), BF16))
        m = jnp.dot(w, vn_b[:, gi * SG_GROUP_DIM:(gi + 1) * SG_GROUP_DIM], preferred_element_type=F32)
        mixed.append(m + bias[:, gi:gi + 1])
    os_ref[0] = (u * jnp.concatenate(mixed, axis=1)).astype(BF16)


def _local(p_loc3, cos_t, sin_t, ret_norm, sg_norm, sg_w, sg_b, consts):
    b, s, _ = p_loc3.shape
    c = RET_CHUNK
    dmat, qd, kd, chunk_decay, perm = consts

    def whole(arr):
        nd = arr.ndim
        return pl.BlockSpec(arr.shape, lambda i, j: (0,) * nd)

    rn = ret_norm.reshape(1, RET_V_W)
    sn = sg_norm.reshape(1, SG_W)
    sgw = sg_w.astype(BF16)
    sgb_t = jnp.transpose(sg_b)
    tab = pl.BlockSpec((1, c, LANES), lambda i, j: (i, j, 0))
    return pl.pallas_call(
        functools.partial(_local_kernel, chunk_decay=chunk_decay),
        out_shape=(jax.ShapeDtypeStruct((b, s, RET_V_W), BF16),
                   jax.ShapeDtypeStruct((b, s, SG_W), BF16)),
        grid=(b, s // c),
        in_specs=[
            pl.BlockSpec((1, c, LOC_W), lambda i, j: (i, j, 0)),
            tab, tab,
            whole(perm), whole(dmat), whole(qd), whole(kd), whole(rn), whole(sn), whole(sgw), whole(sgb_t),
        ],
        out_specs=(pl.BlockSpec((1, c, RET_V_W), lambda i, j: (i, j, 0)),
                   pl.BlockSpec((1, c, SG_W), lambda i, j: (i, j, 0))),
        scratch_shapes=[pltpu.VMEM((RET_HEADS, RET_QK_DIM, RET_V_DIM), F32)],
        compiler_params=_cparams(("parallel", "arbitrary")),
        name="local_mixers",
    )(p_loc3, cos_t, sin_t, perm, dmat, qd, kd, rn, sn, sgw, sgb_t)


def _t5_bucket_table():
    max_exact = REL_BUCKETS // 2
    d = np.arange(REL_MAX_DIST)
    df = np.maximum(d, 1).astype(np.float32)
    large = max_exact + (np.log(df / max_exact) / np.float32(math.log(REL_MAX_DIST / max_exact))
                         * (REL_BUCKETS - max_exact)).astype(np.int32)
    large = np.minimum(large, REL_BUCKETS - 1)
    return np.where(d < max_exact, d, large)


def _bucket_starts():
    bk = _t5_bucket_table()
    assert np.all(np.diff(bk) >= 0) and bk[-1] == REL_BUCKETS - 1
    return [(int(np.argmax(bk == v)), int(v)) for v in sorted(set(bk.tolist())) if v > 0]


def _attn_kernel(tq_ref, tc_ref, qmin_ref, kmax_ref, qmin_sub_ref, kmax_sub_ref, relb_ref,
                 pq_ref, kiw_ref, kv_ref, posq_ref, posk_ref,
                 o_ref, keys_ref, jst_ref, thr_ref, cnt_ref, m_ref, acc_ref, bias_ref,
                 *, bq, bk, n_keep, seq, far_dist, bucket_starts):
    b = pl.program_id(0)
    step = pl.program_id(1)
    qi = tq_ref[step]
    c = tc_ref[step]
    n_chunks = (qi * bq + bq - 1) // bk + 1
    nlc = bk // LANES
    nsub_q, nsub_k = bq // LANES, bk // LANES
    k_f = float(n_keep)

    def tile_lanes(x, n=None):
        return jnp.concatenate([x] * (nlc if n is None else n), axis=1)

    def lane_chunk(x, j):
        return x[:, j * LANES:(j + 1) * LANES]

    @pl.when(c == 0)
    def _prologue():
        qiv = pq_ref[0, :, PQ_QI:PQ_QI + IDX_Q_W]
        zpad = jnp.zeros((bq, LANES - IDX_DIM), BF16)
        q_heads = [jnp.concatenate([qiv[:, h * IDX_DIM:(h + 1) * IDX_DIM], zpad], axis=1)
                   for h in range(IDX_HEADS)]
        kiw_q = pq_ref[0, :, PQ_KIW:PQ_KIW + LANES]
        w = kiw_q[:, IDX_DIM:IDX_DIM + IDX_HEADS].astype(F32)
        w_bc = [jnp.broadcast_to(w[:, h:h + 1], (bq, bk)) for h in range(IDX_HEADS)]
        row_t = qi * bq + lax.broadcasted_iota(I32, (bq, bk), 0)
        lane_s = lax.broadcasted_iota(I32, (bq, bk), 1)

        def score_body(cc, carry):
            off = pl.multiple_of(cc * bk, bk)
            kc = kiw_ref[0, pl.ds(off, bk), :]
            sc = jnp.zeros((bq, bk), F32)
            for h in range(IDX_HEADS):
                sh = lax.dot_general(q_heads[h], kc, (((1,), (1,)), ((), ())), preferred_element_type=F32)
                sc = sc + w_bc[h] * jnp.maximum(sh, 0.0)
            sc = jnp.where(sc == 0.0, 0.0, sc)
            bits = lax.bitcast_convert_type(sc, I32)
            key = jnp.where(bits < 0, bits ^ INT_MAX, bits)
            keys_ref[cc] = jnp.where(cc * bk + lane_s <= row_t, key, INT_MIN)
            return carry

        lax.fori_loop(0, n_chunks, score_body, 0)

        def count_ge(thr):
            thr_ref[...] = thr
            cnt_ref[...] = jnp.zeros((bq, LANES), F32)

            def body(cc, carry):
                for r in range(bq // COUNT_ROWS):
                    rows = pl.ds(r * COUNT_ROWS, COUNT_ROWS)
                    t = keys_ref[cc, rows, :]
                    th = thr_ref[rows, :]
                    acc = cnt_ref[rows, :]
                    for j in range(nlc):
                        acc = acc + jnp.where(lane_chunk(t, j) >= th, 1.0, 0.0)
                    cnt_ref[rows, :] = acc
                return carry

            lax.fori_loop(0, n_chunks, body, 0)
            return jnp.broadcast_to(jnp.sum(cnt_ref[...], axis=1, keepdims=True), (bq, LANES))

        def bis(_, lohi):
            lo, hi = lohi
            mid = (lo & hi) + ((lo ^ hi) >> 1)
            ok = count_ge(mid) >= k_f
            return jnp.where(ok, mid, lo), jnp.where(ok, hi, mid)

        tau, _ = lax.fori_loop(0, 32, bis, (jnp.full((bq, LANES), INT_MIN, I32),
                                            jnp.full((bq, LANES), INT_MAX, I32)))
        c_ge = count_ge(tau)
        c_gt = count_ge(tau + 1)
        need = k_f - c_gt
        excess = jnp.logical_and(c_ge > k_f, tau != INT_MIN)
        jst_ref[...] = jnp.full((bq, LANES), seq, I32)
        lane = lax.broadcasted_iota(I32, (bq, LANES), 1)

        @pl.when(jnp.max(jnp.where(excess, 1.0, 0.0)) > 0.0)
        def _ties():
            def count_tie_le(jmax):
                def body(cc, acc):
                    t = keys_ref[cc]
                    for j in range(nlc):
                        hit = jnp.logical_and(lane_chunk(t, j) == tau, lane <= jmax - (cc * bk + j * LANES))
                        acc = acc + jnp.where(hit, 1.0, 0.0)
                    return acc
                acc = lax.fori_loop(0, n_chunks, body, jnp.zeros((bq, LANES), F32))
                return jnp.broadcast_to(jnp.sum(acc, axis=1, keepdims=True), (bq, LANES))

            def bis_j(_, lohi):
                jl, jh = lohi
                mid = (jl + jh) >> 1
                ok = count_tie_le(mid) >= need
                return jnp.where(ok, jl, mid), jnp.where(ok, mid, jh)

            n_it = max(1, int(math.ceil(math.log2(seq + 1))))
            _, jh = lax.fori_loop(0, n_it, bis_j, (jnp.full((bq, LANES), -1, I32),
                                                   jnp.full((bq, LANES), seq - 1, I32)))
            jst_ref[...] = jnp.where(excess, jh, seq)

        jst = jst_ref[...]

        def build_mask(cc, carry):
            t = keys_ref[cc]
            cols = []
            for j in range(nlc):
                tj = lane_chunk(t, j)
                tie_ok = lane <= jst - (cc * bk + j * LANES)
                sel = jnp.logical_and(jnp.logical_or(tj > tau, jnp.logical_and(tj == tau, tie_ok)), tj != INT_MIN)
                cols.append(jnp.where(sel, 0.0, NEG_INF))
            keys_ref[cc] = lax.bitcast_convert_type(jnp.concatenate(cols, axis=1), I32)
            return carry

        lax.fori_loop(0, n_chunks, build_mask, 0)

        m_ref[...] = jnp.full(m_ref.shape, NEG_INF, F32)
        acc_ref[...] = jnp.zeros_like(acc_ref)

    maskf = lax.bitcast_convert_type(keys_ref[c], F32)
    c1 = (ATT_HEAD_DIM ** -0.5) * LOG2E
    q = pq_ref[0, :, PQ_QA:PQ_QA + ATT_W]
    kk = kv_ref[0, :, KV_K:KV_K + ATT_W]
    vv = kv_ref[0, :, KV_V:KV_V + ATT_W]
    ones_v = jnp.ones((bk, ATT_HEAD_DIM), BF16)

    def heads(near):
        for h in range(ATT_HEADS):
            hs = slice(h * ATT_HEAD_DIM, (h + 1) * ATT_HEAD_DIM)
            s = lax.dot_general(q[:, hs], kk[:, hs], (((1,), (1,)), ((), ())), preferred_element_type=F32)
            t = s * c1 + maskf
            if near:
                t = t + bias_ref[h]
            m_old = m_ref[h]
            m_new = jnp.maximum(m_old, jnp.max(t, axis=1, keepdims=True))
            m_safe = jnp.where(m_new == NEG_INF, 0.0, m_new)
            alpha = jnp.exp2(m_old - m_safe)
            p = jnp.exp2(t - tile_lanes(m_safe)).astype(BF16)
            v_aug = jnp.concatenate([vv[:, hs], ones_v], axis=1)
            acc_ref[h] = tile_lanes(alpha, 2) * acc_ref[h] + jnp.dot(p, v_aug, preferred_element_type=F32)
            m_ref[h] = m_new

    far = qmin_ref[b * (seq // bq) + qi] - kmax_ref[b * (seq // bk) + c] >= far_dist

    @pl.when(far)
    def _():
        heads(False)

    @pl.when(jnp.logical_not(far))
    def _():
        nsub = seq // LANES
        for r in range(nsub_q):
            for j in range(nsub_k):
                rows = slice(r * LANES, (r + 1) * LANES)
                cols = slice(j * LANES, (j + 1) * LANES)
                sub_far = (qmin_sub_ref[b * nsub + qi * nsub_q + r]
                           - kmax_sub_ref[b * nsub + c * nsub_k + j]) >= far_dist

                @pl.when(sub_far)
                def _():
                    for h in range(ATT_HEADS):
                        bias_ref[h, rows, cols] = jnp.zeros((LANES, LANES), F32)

                @pl.when(jnp.logical_not(sub_far))
                def _():
                    half = LANES // 2
                    for hr in range(2):
                        rr = slice(r * LANES + hr * half, r * LANES + (hr + 1) * half)
                        dist = posq_ref[0, rr, :] - posk_ref[0, 0, :, cols]
                        bias = [jnp.full((half, LANES), relb_ref[h], F32) for h in range(ATT_HEADS)]
                        for start, bucket in bucket_starts:
                            ge = dist >= start
                            for h in range(ATT_HEADS):
                                bias[h] = jnp.where(ge, relb_ref[bucket * ATT_HEADS + h], bias[h])
                        for h in range(ATT_HEADS):
                            bias_ref[h, rr, cols] = bias[h]

        heads(True)

    @pl.when(c == n_chunks - 1)
    def _():
        outs = []
        for h in range(ATT_HEADS):
            a = acc_ref[h]
            outs.append(a[:, :ATT_HEAD_DIM] / a[:, ATT_HEAD_DIM:])
        o_ref[0] = jnp.concatenate(outs, axis=1).astype(BF16)


def _attn(p_q3, p_kv3, positions, rel_table, *, bq, bk):
    b, s, _ = p_q3.shape
    assert bq % LANES == 0 and bk % LANES == 0
    n_keep = min(TOPK_MAX, s // 4)
    nq, nkc, nsub = s // bq, s // bk, s // LANES
    tq, tc = [], []
    for qi in range(nq):
        for c in range((qi * bq + bq - 1) // bk + 1):
            tq.append(qi)
            tc.append(c)
    n_steps = len(tq)
    tq = jnp.asarray(np.asarray(tq, np.int32))
    tc = jnp.asarray(np.asarray(tc, np.int32))
    qmin = jnp.min(positions.reshape(b, nq, bq), axis=-1).reshape(-1)
    kmax = jnp.max(positions.reshape(b, nkc, bk), axis=-1).reshape(-1)
    qmin_sub = jnp.min(positions.reshape(b, nsub, LANES), axis=-1).reshape(-1)
    kmax_sub = jnp.max(positions.reshape(b, nsub, LANES), axis=-1).reshape(-1)
    starts = _bucket_starts()
    far_dist = REL_MAX_DIST - 1
    assert starts[-1][0] <= far_dist and starts[-1][1] == REL_BUCKETS - 1
    relb = ((rel_table - rel_table[REL_BUCKETS - 1:REL_BUCKETS, :]) * LOG2E).astype(F32).reshape(-1)

    grid_spec = pltpu.PrefetchScalarGridSpec(
        num_scalar_prefetch=7,
        grid=(b, n_steps),
        in_specs=[
            pl.BlockSpec((1, bq, PQ_W), lambda i, t, tq, tc, *_: (i, tq[t], 0)),
            pl.BlockSpec((1, s, LANES), lambda i, t, *_: (i, 0, PQ_KIW // LANES)),
            pl.BlockSpec((1, bk, KV_W), lambda i, t, tq, tc, *_: (i, tc[t], 0)),
            pl.BlockSpec((1, bq, 1), lambda i, t, tq, tc, *_: (i, tq[t], 0)),
            pl.BlockSpec((1, 1, 1, bk), lambda i, t, tq, tc, *_: (i, tc[t], 0, 0)),
        ],
        out_specs=pl.BlockSpec((1, bq, ATT_W), lambda i, t, tq, tc, *_: (i, tq[t], 0)),
        scratch_shapes=[
            pltpu.VMEM((nkc, bq, bk), I32),
            pltpu.VMEM((bq, LANES), I32),
            pltpu.VMEM((bq, LANES), I32),
            pltpu.VMEM((bq, LANES), F32),
            pltpu.VMEM((ATT_HEADS, bq, LANES), F32),
            pltpu.VMEM((ATT_HEADS, bq, 2 * ATT_HEAD_DIM), F32),
            pltpu.VMEM((ATT_HEADS, bq, bk), F32),
        ],
    )
    kern = functools.partial(_attn_kernel, bq=bq, bk=bk, n_keep=n_keep, seq=s,
                             far_dist=far_dist, bucket_starts=starts)
    return pl.pallas_call(
        kern,
        out_shape=jax.ShapeDtypeStruct((b, s, ATT_W), BF16),
        grid_spec=grid_spec,
        compiler_params=_cparams(("parallel", "arbitrary")),
        name="sparse_attn",
    )(tq, tc, qmin, kmax, qmin_sub, kmax_sub, relb,
      p_q3, p_q3, p_kv3, positions.reshape(b, s, 1), positions.reshape(b, nkc, 1, bk))


def _merge_kernel(or_ref, oa_ref, os_ref, gr_ref, ga_ref, gs_ref, wr_ref, wa_ref, ws_ref, wo_ref, x_ref, o_ref):
    j = pl.program_id(1)

    @pl.when(j == 0)
    def _():
        o_ref[...] = jnp.zeros_like(o_ref)

    def branch(o, w, g):
        return jax.nn.sigmoid(g[0].astype(F32)) * jnp.dot(o[...], w[...], preferred_element_type=F32)

    merged = branch(or_ref, wr_ref, gr_ref) + branch(oa_ref, wa_ref, ga_ref) + branch(os_ref, ws_ref, gs_ref)
    o_ref[...] += jnp.dot(merged.astype(BF16), wo_ref[...], preferred_element_type=F32)

    @pl.when(j == pl.num_programs(1) - 1)
    def _():
        o_ref[...] = x_ref[...] + o_ref[...]


def _merge(o_r, o_a, o_s, p_g, wr, wa, ws, wo, x2, *, tm):
    t, d = x2.shape
    tn = PROJ_TILE
    per_gate = d // tn

    def gspec(g):
        return pl.BlockSpec((1, tm, tn), lambda i, j: (g * per_gate + j, i, 0))

    return pl.pallas_call(
        _merge_kernel,
        out_shape=jax.ShapeDtypeStruct((t, d), F32),
        grid=(t // tm, per_gate),
        in_specs=[
            pl.BlockSpec((tm, RET_V_W), lambda i, j: (i, 0)),
            pl.BlockSpec((tm, ATT_W), lambda i, j: (i, 0)),
            pl.BlockSpec((tm, SG_W), lambda i, j: (i, 0)),
            gspec(0), gspec(1), gspec(2),
            pl.BlockSpec((RET_V_W, tn), lambda i, j: (0, j)),
            pl.BlockSpec((ATT_W, tn), lambda i, j: (0, j)),
            pl.BlockSpec((SG_W, tn), lambda i, j: (0, j)),
            pl.BlockSpec((tn, d), lambda i, j: (j, 0)),
            pl.BlockSpec((tm, d), lambda i, j: (i, 0)),
        ],
        out_specs=pl.BlockSpec((tm, d), lambda i, j: (i, 0)),
        compiler_params=_cparams(("parallel", "arbitrary")),
        name="merge",
    )(o_r, o_a, o_s, p_g, p_g, p_g, wr, wa, ws, wo, x2)


def _relayout_w_in(w, d_model):
    sizes = (RET_QK_W, RET_QK_W, RET_V_W, RET_V_W, ATT_W, ATT_W, ATT_W, IDX_Q_W, IDX_DIM, IDX_HEADS,
             SG_W, SG_W, d_model, d_model, d_model)
    offs = np.concatenate([[0], np.cumsum(sizes)])
    (q_r, k_r, v_r, g_r, q_a, k_a, v_a, q_i, k_i, w_i, u_s, v_s, gt_r, gt_a, gt_s) = [
        w[:, offs[n]:offs[n + 1]] for n in range(len(sizes))]
    rows = w.shape[0]

    def zeros(n):
        return jnp.zeros((rows, n), w.dtype)

    loc = [q_r, k_r, v_r, g_r, u_s, v_s]
    loc.append(zeros(LOC_W - sum(a.shape[1] for a in loc)))
    pq = [q_a, q_i, k_i, w_i]
    pq.append(zeros(PQ_W - sum(a.shape[1] for a in pq)))
    out = jnp.concatenate(loc + pq + [k_a, v_a, gt_r, gt_a, gt_s], axis=1).astype(BF16)
    assert out.shape[1] == LOC_W + PQ_W + KV_W + 3 * d_model
    return out


def _pick(n, prefs):
    for p in prefs:
        if n % p == 0:
            return p
    return n


def kernel(x, positions, rel_table, ffn1_norm, ffn1_w_gate, ffn1_w_up, ffn1_w_down, mix_norm, w_in, ret_norm, sg_norm, sg_w, sg_b, w_br_ret, w_br_att, w_br_sg, w_out, ffn2_norm, ffn2_w_gate, ffn2_w_up, ffn2_w_down, final_norm):
    b, s, d = x.shape
    t = b * s
    depth = w_in.shape[0]
    d_ff = ffn1_w_gate.shape[2]
    assert s % RET_CHUNK == 0 and d % PROJ_TILE == 0

    tm = _pick(t, (512, 256, 128))
    tm_proj = _pick(t, (1024, 512, 256, 128))
    tf = _pick(d_ff, (512, 256, 128))
    bq = _pick(s, (256, 128))
    bk = _pick(s, (512, 256, 128))

    cos_t, sin_t = _rope_tables(positions, tm=_pick(s, (2048, 1024, 512, 256, 128)))
    consts = _retention_constants()
    x2 = x.reshape(t, d)
    for l in range(depth):
        x2 = _ffn(x2, ffn1_norm[l], ffn1_w_gate[l].astype(BF16), ffn1_w_up[l].astype(BF16),
                  ffn1_w_down[l].astype(BF16), tm=tm, tf=tf)
        p_loc, p_q, p_kv, p_g = _proj(x2, mix_norm[l], _relayout_w_in(w_in[l], d), tm=tm_proj)
        o_r, o_s = _local(p_loc.reshape(b, s, -1), cos_t, sin_t, ret_norm[l], sg_norm[l], sg_w[l], sg_b[l], consts)
        o_a = _attn(p_q.reshape(b, s, -1), p_kv.reshape(b, s, -1), positions, rel_table, bq=bq, bk=bk)
        x2 = _merge(o_r.reshape(t, -1), o_a.reshape(t, -1), o_s.reshape(t, -1), p_g,
                    w_br_ret[l].astype(BF16), w_br_att[l].astype(BF16), w_br_sg[l].astype(BF16),
                    w_out[l].astype(BF16), x2, tm=tm)
        x2 = _ffn(x2, ffn2_norm[l], ffn2_w_gate[l].astype(BF16), ffn2_w_up[l].astype(BF16),
                  ffn2_w_down[l].astype(BF16), final_norm if l == depth - 1 else None, tm=tm, tf=tf)
    return x2.reshape(b, s, d)
```

```python
import functools
import math

import numpy as np
import jax
import jax.numpy as jnp
from jax import lax
from jax.experimental import pallas as pl
from jax.experimental.pallas import tpu as pltpu

F32 = jnp.float32
BF16 = jnp.bfloat16
I32 = jnp.int32

RET_HEADS, RET_QK_DIM, RET_V_DIM, RET_CHUNK = 6, 64, 128, 128
ATT_HEADS, ATT_HEAD_DIM = 6, 128
IDX_HEADS, IDX_DIM = 4, 64
TOPK_MAX = 256
SG_GROUPS, SG_GROUP_DIM, SG_CHUNK = 4, 128, 128
REL_BUCKETS, REL_MAX_DIST = 32, 128
ROPE_BASE = 10000.0
EPS = 1e-6

RET_QK_W = RET_HEADS * RET_QK_DIM
RET_V_W = RET_HEADS * RET_V_DIM
ATT_W = ATT_HEADS * ATT_HEAD_DIM
IDX_Q_W = IDX_HEADS * IDX_DIM
SG_W = SG_GROUPS * SG_GROUP_DIM

LANES = 128
COUNT_ROWS = 64
KV_BUFFERS = 3
VMEM_LIMIT_BYTES = 56 * 1024 * 1024

PROJ_TILE = 512
LOC_QR, LOC_KR, LOC_VR, LOC_GR, LOC_US, LOC_VS = 0, 384, 768, 1536, 2304, 2816
LOC_W = 3584
PQ_QA, PQ_QI, PQ_KIW = 0, 768, 1024
PQ_W = 1536
KV_K, KV_V = 0, 768
KV_W = 1536

INT_MIN = np.int32(-2 ** 31)
INT_MAX = np.int32(2 ** 31 - 1)
NEG_INF = float("-inf")
LOG2E = math.log2(math.e)


def _cparams(sem):
    return pltpu.CompilerParams(dimension_semantics=sem, vmem_limit_bytes=VMEM_LIMIT_BYTES)


def _rms(x, g):
    return x * lax.rsqrt(jnp.mean(x * x, axis=-1, keepdims=True) + EPS) * g


def _ffn_kernel(x_ref, g_ref, wg_ref, wu_ref, wd_ref, *rest, final_norm):
    if final_norm:
        fg_ref, o_ref, xn_ref = rest
    else:
        o_ref, xn_ref = rest
    j = pl.program_id(1)

    @pl.when(j == 0)
    def _():
        xn_ref[...] = _rms(x_ref[...], g_ref[...]).astype(BF16)
        o_ref[...] = jnp.zeros_like(o_ref)

    xn = xn_ref[...]
    a = jnp.dot(xn, wg_ref[...], preferred_element_type=F32)
    b = jnp.dot(xn, wu_ref[...], preferred_element_type=F32)
    h = (a * jax.nn.sigmoid(a) * b).astype(BF16)
    o_ref[...] += jnp.dot(h, wd_ref[...], preferred_element_type=F32)

    @pl.when(j == pl.num_programs(1) - 1)
    def _():
        y = x_ref[...] + 0.5 * o_ref[...]
        if final_norm:
            y = _rms(y, fg_ref[...])
        o_ref[...] = y


def _ffn(x2, gain, wg, wu, wd, final_gain=None, *, tm, tf):
    t, d = x2.shape
    f = wg.shape[1]
    final_norm = final_gain is not None
    in_specs = [
        pl.BlockSpec((tm, d), lambda i, j: (i, 0)),
        pl.BlockSpec((1, d), lambda i, j: (0, 0)),
        pl.BlockSpec((d, tf), lambda i, j: (0, j)),
        pl.BlockSpec((d, tf), lambda i, j: (0, j)),
        pl.BlockSpec((tf, d), lambda i, j: (j, 0)),
    ]
    args = [x2, gain.reshape(1, d), wg, wu, wd]
    if final_norm:
        in_specs.append(pl.BlockSpec((1, d), lambda i, j: (0, 0)))
        args.append(final_gain.reshape(1, d))
    return pl.pallas_call(
        functools.partial(_ffn_kernel, final_norm=final_norm),
        out_shape=jax.ShapeDtypeStruct((t, d), F32),
        grid=(t // tm, f // tf),
        in_specs=in_specs,
        out_specs=pl.BlockSpec((tm, d), lambda i, j: (i, 0)),
        scratch_shapes=[pltpu.VMEM((tm, d), BF16)],
        compiler_params=_cparams(("parallel", "arbitrary")),
        name="ffn",
    )(*args)


def _proj_kernel(x_ref, g_ref, w_ref, ol_ref, oq_ref, okv_ref, og_ref, xn_ref, *, t_loc, t_q, t_kv):
    j = pl.program_id(1)

    @pl.when(j == 0)
    def _():
        xn_ref[...] = _rms(x_ref[...], g_ref[...]).astype(BF16)

    y = jnp.dot(xn_ref[...], w_ref[...], preferred_element_type=F32).astype(BF16)

    @pl.when(j < t_loc)
    def _():
        ol_ref[...] = y

    @pl.when(jnp.logical_and(j >= t_loc, j < t_loc + t_q))
    def _():
        oq_ref[...] = y

    @pl.when(jnp.logical_and(j >= t_loc + t_q, j < t_loc + t_q + t_kv))
    def _():
        okv_ref[...] = y

    @pl.when(j >= t_loc + t_q + t_kv)
    def _():
        og_ref[0] = y


def _proj(x2, gain, w, *, tm):
    t, d = x2.shape
    tn = PROJ_TILE
    n_tiles = w.shape[1] // tn
    t_loc, t_q, t_kv = LOC_W // tn, PQ_W // tn, KV_W // tn
    t_g = n_tiles - t_loc - t_q - t_kv

    def clampspec(first, count):
        return pl.BlockSpec((tm, tn), lambda i, j: (i, jnp.clip(j - first, 0, count - 1)))

    return pl.pallas_call(
        functools.partial(_proj_kernel, t_loc=t_loc, t_q=t_q, t_kv=t_kv),
        out_shape=(jax.ShapeDtypeStruct((t, LOC_W), BF16),
                   jax.ShapeDtypeStruct((t, PQ_W), BF16),
                   jax.ShapeDtypeStruct((t, KV_W), BF16),
                   jax.ShapeDtypeStruct((t_g, t, tn), BF16)),
        grid=(t // tm, n_tiles),
        in_specs=[
            pl.BlockSpec((tm, d), lambda i, j: (i, 0)),
            pl.BlockSpec((1, d), lambda i, j: (0, 0)),
            pl.BlockSpec((d, tn), lambda i, j: (0, j)),
        ],
        out_specs=(clampspec(0, t_loc), clampspec(t_loc, t_q), clampspec(t_loc + t_q, t_kv),
                   pl.BlockSpec((1, tm, tn),
                                lambda i, j: (jnp.clip(j - (t_loc + t_q + t_kv), 0, t_g - 1), i, 0))),
        scratch_shapes=[pltpu.VMEM((tm, d), BF16)],
        compiler_params=_cparams(("arbitrary", "arbitrary")),
        name="proj",
    )(x2, gain.reshape(1, d), w)


def _rope_table_kernel(pos_ref, invf_ref, sign_ref, c_ref, s_ref):
    ang = pos_ref[0].astype(F32) * invf_ref[...]
    c_ref[0] = jnp.cos(ang)
    s_ref[0] = jnp.sin(ang) * sign_ref[...]


def _rope_tables(positions, *, tm):
    b, s = positions.shape
    half = RET_QK_DIM // 2
    inv_freq = ROPE_BASE ** (-jnp.arange(0, RET_QK_DIM, 2, dtype=F32) / RET_QK_DIM)
    lane = np.arange(LANES)
    invf = inv_freq[lane % half].reshape(1, LANES)
    sign = jnp.asarray(np.where(lane % RET_QK_DIM < half, -1.0, 1.0), F32).reshape(1, LANES)
    out = jax.ShapeDtypeStruct((b, s, LANES), F32)
    return pl.pallas_call(
        _rope_table_kernel,
        out_shape=(out, out),
        grid=(b, s // tm),
        in_specs=[
            pl.BlockSpec((1, tm, 1), lambda i, j: (i, j, 0)),
            pl.BlockSpec((1, LANES), lambda i, j: (0, 0)),
            pl.BlockSpec((1, LANES), lambda i, j: (0, 0)),
        ],
        out_specs=(pl.BlockSpec((1, tm, LANES), lambda i, j: (i, j, 0)),
                   pl.BlockSpec((1, tm, LANES), lambda i, j: (i, j, 0))),
        compiler_params=_cparams(("parallel", "parallel")),
        name="rope_tables",
    )(positions.reshape(b, s, 1), invf, sign)


def _retention_constants():
    c = RET_CHUNK
    h = np.arange(RET_HEADS, dtype=np.float64)
    log_g = np.log(1.0 - 2.0 ** (-5.0 - h))
    idx = np.arange(c, dtype=np.float64)
    rel = idx[:, None] - idx[None, :]
    dmat = np.where(rel >= 0, np.exp(np.maximum(rel, 0.0) * log_g[:, None, None]), 0.0)
    k_decay = np.exp((c - 1 - idx)[None, :] * log_g[:, None])
    q_decay = np.exp((idx + 1)[None, :] * log_g[:, None])
    chunk_decay = np.exp(c * log_g)
    kd = np.repeat(k_decay.T, RET_QK_DIM, axis=1)
    qd = np.repeat(q_decay.T, RET_QK_DIM, axis=1) * RET_QK_DIM ** -0.5
    lane = np.arange(RET_QK_W)
    src = (lane // RET_QK_DIM) * RET_QK_DIM + (lane % RET_QK_DIM + RET_QK_DIM // 2) % RET_QK_DIM
    perm = np.zeros((RET_QK_W, RET_QK_W), np.float32)
    perm[src, lane] = 1.0
    return (jnp.asarray(dmat, F32), jnp.asarray(qd, F32), jnp.asarray(kd, F32),
            [float(np.float32(v)) for v in chunk_decay], jnp.asarray(perm, BF16))


def _gelu_tanh(x):
    return 0.5 * x * (1.0 + jnp.tanh(math.sqrt(2.0 / math.pi) * (x + 0.044715 * (x * x * x))))


def _local_kernel(p_ref, cos_ref, sin_ref,
                  perm_ref, dmat_ref, qd_ref, kd_ref, rn_ref, sn_ref, sgw_ref, sgb_ref,
                  or_ref, os_ref, state_ref, *, chunk_decay):
    @pl.when(pl.program_id(1) == 0)
    def _():
        state_ref[...] = jnp.zeros_like(state_ref)

    n_rep = RET_QK_W // LANES
    cos3 = jnp.concatenate([cos_ref[0]] * n_rep, axis=1)
    sin3 = jnp.concatenate([sin_ref[0]] * n_rep, axis=1)
    q = p_ref[0, :, LOC_QR:LOC_QR + RET_QK_W]
    k = p_ref[0, :, LOC_KR:LOC_KR + RET_QK_W]
    perm = perm_ref[...]
    qr = q.astype(F32) * cos3 + jnp.dot(q, perm, preferred_element_type=F32) * sin3
    kr = k.astype(F32) * cos3 + jnp.dot(k, perm, preferred_element_type=F32) * sin3
    q_in = (qr * (RET_QK_DIM ** -0.5)).astype(BF16)
    q_cr = (qr * qd_ref[...]).astype(BF16)
    k_b = kr.astype(BF16)
    k_dec_t = jnp.transpose(kr * kd_ref[...]).astype(BF16)
    v = p_ref[0, :, LOC_VR:LOC_VR + RET_V_W]
    outs = []
    for h in range(RET_HEADS):
        qs = slice(h * RET_QK_DIM, (h + 1) * RET_QK_DIM)
        vh = v[:, h * RET_V_DIM:(h + 1) * RET_V_DIM]
        a = lax.dot_general(q_in[:, qs], k_b[:, qs], (((1,), (1,)), ((), ())),
                            preferred_element_type=F32) * dmat_ref[h]
        prev = state_ref[h]
        o = (jnp.dot(a.astype(BF16), vh, preferred_element_type=F32)
             + jnp.dot(q_cr[:, qs], prev.astype(BF16), preferred_element_type=F32))
        state_ref[h] = prev * chunk_decay[h] + jnp.dot(k_dec_t[qs, :], vh, preferred_element_type=F32)
        mu = jnp.mean(o, axis=-1, keepdims=True)
        d = o - mu
        var = jnp.mean(d * d, axis=-1, keepdims=True)
        outs.append(d * lax.rsqrt(var + EPS))
    g = p_ref[0, :, LOC_GR:LOC_GR + RET_V_W].astype(F32)
    y = jnp.concatenate(outs, axis=1) * rn_ref[...]
    or_ref[0] = (y * (g * jax.nn.sigmoid(g))).astype(BF16)

    u = _gelu_tanh(p_ref[0, :, LOC_US:LOC_US + SG_W].astype(F32))
    vn = _rms(_gelu_tanh(p_ref[0, :, LOC_VS:LOC_VS + SG_W].astype(F32)), sn_ref[...])
    vn_b = vn.astype(BF16)
    c = SG_CHUNK
    tril = lax.broadcasted_iota(I32, (c, c), 0) >= lax.broadcasted_iota(I32, (c, c), 1)
    bias = sgb_ref[...]
    mixed = []
    for gi in range(SG_GROUPS):
        w = jnp.where(tril, sgw_ref[gi], jnp.zeros((), BF16))
        m = jnp.dot(w, vn_b[:, gi * SG_GROUP_DIM:(gi + 1) * SG_GROUP_DIM], preferred_element_type=F32)
        mixed.append(m + bias[:, gi:gi + 1])
    os_ref[0] = (u * jnp.concatenate(mixed, axis=1)).astype(BF16)


def _local(p_loc3, cos_t, sin_t, ret_norm, sg_norm, sg_w, sg_b, consts):
    b, s, _ = p_loc3.shape
    c = RET_CHUNK
    dmat, qd, kd, chunk_decay, perm = consts

    def whole(arr):
        nd = arr.ndim
        return pl.BlockSpec(arr.shape, lambda i, j: (0,) * nd)

    rn = ret_norm.reshape(1, RET_V_W)
    sn = sg_norm.reshape(1, SG_W)
    sgw = sg_w.astype(BF16)
    sgb_t = jnp.transpose(sg_b)
    tab = pl.BlockSpec((1, c, LANES), lambda i, j: (i, j, 0))
    return pl.pallas_call(
        functools.partial(_local_kernel, chunk_decay=chunk_decay),
        out_shape=(jax.ShapeDtypeStruct((b, s, RET_V_W), BF16),
                   jax.ShapeDtypeStruct((b, s, SG_W), BF16)),
        grid=(b, s // c),
        in_specs=[
            pl.BlockSpec((1, c, LOC_W), lambda i, j: (i, j, 0)),
            tab, tab,
            whole(perm), whole(dmat), whole(qd), whole(kd), whole(rn), whole(sn), whole(sgw), whole(sgb_t),
        ],
        out_specs=(pl.BlockSpec((1, c, RET_V_W), lambda i, j: (i, j, 0)),
                   pl.BlockSpec((1, c, SG_W), lambda i, j: (i, j, 0))),
        scratch_shapes=[pltpu.VMEM((RET_HEADS, RET_QK_DIM, RET_V_DIM), F32)],
        compiler_params=_cparams(("parallel", "arbitrary")),
        name="local_mixers",
    )(p_loc3, cos_t, sin_t, perm, dmat, qd, kd, rn, sn, sgw, sgb_t)


def _t5_bucket_table():
    max_exact = REL_BUCKETS // 2
    d = np.arange(REL_MAX_DIST)
    df = np.maximum(d, 1).astype(np.float32)
    large = max_exact + (np.log(df / max_exact) / np.float32(math.log(REL_MAX_DIST / max_exact))
                         * (REL_BUCKETS - max_exact)).astype(np.int32)
    large = np.minimum(large, REL_BUCKETS - 1)
    return np.where(d < max_exact, d, large)


def _bucket_starts():
    bk = _t5_bucket_table()
    assert np.all(np.diff(bk) >= 0) and bk[-1] == REL_BUCKETS - 1
    return [(int(np.argmax(bk == v)), int(v)) for v in sorted(set(bk.tolist())) if v > 0]


def _attn_kernel(qmin_ref, kmax_ref, qmin_sub_ref, kmax_sub_ref, relb_ref,
                 pq_ref, kiw_ref, kv_hbm, posq_ref, posk_ref,
                 o_ref, kv_buf, kv_sem, keys_ref, jst_ref, thr_ref, cnt_ref, m_ref, acc_ref, bias_ref,
                 *, bq, bk, n_keep, seq, far_dist, bucket_starts):
    b = pl.program_id(0)
    qi = pl.program_id(1)
    n_chunks = (qi * bq + bq - 1) // bk + 1
    nlc = bk // LANES

    def kv_copy(cc, slot):
        return pltpu.make_async_copy(kv_hbm.at[b, pl.ds(cc * bk, bk), :], kv_buf.at[slot], kv_sem.at[slot])

    for i in range(KV_BUFFERS - 1):
        @pl.when(i < n_chunks)
        def _():
            kv_copy(i, i).start()
    nsub_q, nsub_k = bq // LANES, bk // LANES
    k_f = float(n_keep)

    def tile_lanes(x, n=None):
        return jnp.concatenate([x] * (nlc if n is None else n), axis=1)

    def lane_chunk(x, j):
        return x[:, j * LANES:(j + 1) * LANES]

    def _select():
        qiv = pq_ref[0, :, PQ_QI:PQ_QI + IDX_Q_W]
        zpad = jnp.zeros((bq, LANES - IDX_DIM), BF16)
        q_heads = [jnp.concatenate([qiv[:, h * IDX_DIM:(h + 1) * IDX_DIM], zpad], axis=1)
                   for h in range(IDX_HEADS)]
        kiw_q = pq_ref[0, :, PQ_KIW:PQ_KIW + LANES]
        w = kiw_q[:, IDX_DIM:IDX_DIM + IDX_HEADS].astype(F32)
        w_bc = [jnp.broadcast_to(w[:, h:h + 1], (bq, bk)) for h in range(IDX_HEADS)]
        row_t = qi * bq + lax.broadcasted_iota(I32, (bq, bk), 0)
        lane_s = lax.broadcasted_iota(I32, (bq, bk), 1)

        def score_body(cc, carry):
            off = pl.multiple_of(cc * bk, bk)
            kc = kiw_ref[0, pl.ds(off, bk), :]
            sc = jnp.zeros((bq, bk), F32)
            for h in range(IDX_HEADS):
                sh = lax.dot_general(q_heads[h], kc, (((1,), (1,)), ((), ())), preferred_element_type=F32)
                sc = sc + w_bc[h] * jnp.maximum(sh, 0.0)
            sc = jnp.where(sc == 0.0, 0.0, sc)
            bits = lax.bitcast_convert_type(sc, I32)
            key = jnp.where(bits < 0, bits ^ INT_MAX, bits)
            keys_ref[cc] = jnp.where(cc * bk + lane_s <= row_t, key, INT_MIN)
            return carry

        lax.fori_loop(0, n_chunks, score_body, 0)

        def count_ge(thr):
            thr_ref[...] = thr
            cnt_ref[...] = jnp.zeros((bq, LANES), F32)

            def body(cc, carry):
                for r in range(bq // COUNT_ROWS):
                    rows = pl.ds(r * COUNT_ROWS, COUNT_ROWS)
                    t = keys_ref[cc, rows, :]
                    th = thr_ref[rows, :]
                    acc = cnt_ref[rows, :]
                    for j in range(nlc):
                        acc = acc + jnp.where(lane_chunk(t, j) >= th, 1.0, 0.0)
                    cnt_ref[rows, :] = acc
                return carry

            lax.fori_loop(0, n_chunks, body, 0)
            return jnp.broadcast_to(jnp.sum(cnt_ref[...], axis=1, keepdims=True), (bq, LANES))

        def bis(_, lohi):
            lo, hi = lohi
            mid = (lo & hi) + ((lo ^ hi) >> 1)
            ok = count_ge(mid) >= k_f
            return jnp.where(ok, mid, lo), jnp.where(ok, hi, mid)

        tau, _ = lax.fori_loop(0, 32, bis, (jnp.full((bq, LANES), INT_MIN, I32),
                                            jnp.full((bq, LANES), INT_MAX, I32)))
        c_ge = count_ge(tau)
        c_gt = count_ge(tau + 1)
        need = k_f - c_gt
        excess = jnp.logical_and(c_ge > k_f, tau != INT_MIN)
        jst_ref[...] = jnp.full((bq, LANES), seq, I32)
        lane = lax.broadcasted_iota(I32, (bq, LANES), 1)

        @pl.when(jnp.max(jnp.where(excess, 1.0, 0.0)) > 0.0)
        def _ties():
            def count_tie_le(jmax):
                def body(cc, acc):
                    t = keys_ref[cc]
                    for j in range(nlc):
                        hit = jnp.logical_and(lane_chunk(t, j) == tau, lane <= jmax - (cc * bk + j * LANES))
                        acc = acc + jnp.where(hit, 1.0, 0.0)
                    return acc
                acc = lax.fori_loop(0, n_chunks, body, jnp.zeros((bq, LANES), F32))
                return jnp.broadcast_to(jnp.sum(acc, axis=1, keepdims=True), (bq, LANES))

            def bis_j(_, lohi):
                jl, jh = lohi
                mid = (jl + jh) >> 1
                ok = count_tie_le(mid) >= need
                return jnp.where(ok, jl, mid), jnp.where(ok, mid, jh)

            n_it = max(1, int(math.ceil(math.log2(seq + 1))))
            _, jh = lax.fori_loop(0, n_it, bis_j, (jnp.full((bq, LANES), -1, I32),
                                                   jnp.full((bq, LANES), seq - 1, I32)))
            jst_ref[...] = jnp.where(excess, jh, seq)

        jst = jst_ref[...]

        def build_mask(cc, carry):
            t = keys_ref[cc]
            cols = []
            for j in range(nlc):
                tj = lane_chunk(t, j)
                tie_ok = lane <= jst - (cc * bk + j * LANES)
                sel = jnp.logical_and(jnp.logical_or(tj > tau, jnp.logical_and(tj == tau, tie_ok)), tj != INT_MIN)
                cols.append(jnp.where(sel, 0.0, NEG_INF))
            keys_ref[cc] = lax.bitcast_convert_type(jnp.concatenate(cols, axis=1), I32)
            return carry

        lax.fori_loop(0, n_chunks, build_mask, 0)

        m_ref[...] = jnp.full(m_ref.shape, NEG_INF, F32)
        acc_ref[...] = jnp.zeros_like(acc_ref)

    _select()

    c1 = (ATT_HEAD_DIM ** -0.5) * LOG2E
    q = pq_ref[0, :, PQ_QA:PQ_QA + ATT_W]
    ones_v = jnp.ones((bk, ATT_HEAD_DIM), BF16)

    def heads(near, maskf, kk, vv):
        for h in range(ATT_HEADS):
            hs = slice(h * ATT_HEAD_DIM, (h + 1) * ATT_HEAD_DIM)
            s = lax.dot_general(q[:, hs], kk[:, hs], (((1,), (1,)), ((), ())), preferred_element_type=F32)
            t = s * c1 + maskf
            if near:
                t = t + bias_ref[h]
            m_old = m_ref[h]
            m_new = jnp.maximum(m_old, jnp.max(t, axis=1, keepdims=True))
            m_safe = jnp.where(m_new == NEG_INF, 0.0, m_new)
            alpha = jnp.exp2(m_old - m_safe)
            p = jnp.exp2(t - tile_lanes(m_safe)).astype(BF16)
            v_aug = jnp.concatenate([vv[:, hs], ones_v], axis=1)
            acc_ref[h] = tile_lanes(alpha, 2) * acc_ref[h] + jnp.dot(p, v_aug, preferred_element_type=F32)
            m_ref[h] = m_new

    def attend(c, carry):
        slot = c % KV_BUFFERS
        nxt = c + (KV_BUFFERS - 1)

        @pl.when(nxt < n_chunks)
        def _():
            kv_copy(nxt, nxt % KV_BUFFERS).start()

        kv_copy(c, slot).wait()
        maskf = lax.bitcast_convert_type(keys_ref[c], F32)
        kk = kv_buf[slot, :, KV_K:KV_K + ATT_W]
        vv = kv_buf[slot, :, KV_V:KV_V + ATT_W]
        far = qmin_ref[b * (seq // bq) + qi] - kmax_ref[b * (seq // bk) + c] >= far_dist

        @pl.when(far)
        def _():
            heads(False, maskf, kk, vv)

        @pl.when(jnp.logical_not(far))
        def _():
            nsub = seq // LANES
            for r in range(nsub_q):
                for j in range(nsub_k):
                    rows = slice(r * LANES, (r + 1) * LANES)
                    cols = slice(j * LANES, (j + 1) * LANES)
                    sub_far = (qmin_sub_ref[b * nsub + qi * nsub_q + r]
                               - kmax_sub_ref[b * nsub + c * nsub_k + j]) >= far_dist

                    @pl.when(sub_far)
                    def _():
                        for h in range(ATT_HEADS):
                            bias_ref[h, rows, cols] = jnp.zeros((LANES, LANES), F32)

                    @pl.when(jnp.logical_not(sub_far))
                    def _():
                        half = LANES // 2
                        for hr in range(2):
                            rr = slice(r * LANES + hr * half, r * LANES + (hr + 1) * half)
                            dist = posq_ref[0, rr, :] - posk_ref[0, c, :, cols]
                            bias = [jnp.full((half, LANES), relb_ref[h], F32) for h in range(ATT_HEADS)]
                            for start, bucket in bucket_starts:
                                ge = dist >= start
                                for h in range(ATT_HEADS):
                                    bias[h] = jnp.where(ge, relb_ref[bucket * ATT_HEADS + h], bias[h])
                            for h in range(ATT_HEADS):
                                bias_ref[h, rr, cols] = bias[h]

            heads(True, maskf, kk, vv)

        return carry

    lax.fori_loop(0, n_chunks, attend, 0)

    outs = []
    for h in range(ATT_HEADS):
        a = acc_ref[h]
        outs.append(a[:, :ATT_HEAD_DIM] / a[:, ATT_HEAD_DIM:])
    o_ref[0] = jnp.concatenate(outs, axis=1).astype(BF16)


def _attn(p_q3, p_kv3, positions, rel_table, *, bq, bk):
    b, s, _ = p_q3.shape
    assert bq % LANES == 0 and bk % LANES == 0
    n_keep = min(TOPK_MAX, s // 4)
    nq, nkc, nsub = s // bq, s // bk, s // LANES
    qmin =jnp.min(positions.reshape(b, nq, bq), axis=-1).reshape(-1)
    kmax = jnp.max(positions.reshape(b, nkc, bk), axis=-1).reshape(-1)
    qmin_sub = jnp.min(positions.reshape(b, nsub, LANES), axis=-1).reshape(-1)
    kmax_sub = jnp.max(positions.reshape(b, nsub, LANES), axis=-1).reshape(-1)
    starts = _bucket_starts()
    far_dist = REL_MAX_DIST - 1
    assert starts[-1][0] <= far_dist and starts[-1][1] == REL_BUCKETS - 1
    relb = ((rel_table - rel_table[REL_BUCKETS - 1:REL_BUCKETS, :]) * LOG2E).astype(F32).reshape(-1)

    grid_spec = pltpu.PrefetchScalarGridSpec(
        num_scalar_prefetch=5,
        grid=(b, nq),
        in_specs=[
            pl.BlockSpec((1, bq, PQ_W), lambda i, t, *_: (i, t, 0)),
            pl.BlockSpec((1, s, LANES), lambda i, t, *_: (i, 0, PQ_KIW // LANES)),
            pl.BlockSpec(memory_space=pl.ANY),
            pl.BlockSpec((1, bq, 1), lambda i, t, *_: (i, t, 0)),
            pl.BlockSpec((1, nkc, 1, bk), lambda i, t, *_: (i, 0, 0, 0)),
        ],
        out_specs=pl.BlockSpec((1, bq, ATT_W), lambda i, t, *_: (i, t, 0)),
        scratch_shapes=[
            pltpu.VMEM((KV_BUFFERS, bk, KV_W), BF16),
            pltpu.SemaphoreType.DMA((KV_BUFFERS,)),
            pltpu.VMEM((nkc, bq, bk), I32),
            pltpu.VMEM((bq, LANES), I32),
            pltpu.VMEM((bq, LANES), I32),
            pltpu.VMEM((bq, LANES), F32),
            pltpu.VMEM((ATT_HEADS, bq, LANES), F32),
            pltpu.VMEM((ATT_HEADS, bq, 2 * ATT_HEAD_DIM), F32),
            pltpu.VMEM((ATT_HEADS, bq, bk), F32),
        ],
    )
    kern = functools.partial(_attn_kernel, bq=bq, bk=bk, n_keep=n_keep, seq=s,
                             far_dist=far_dist, bucket_starts=starts)
    return pl.pallas_call(
        kern,
        out_shape=jax.ShapeDtypeStruct((b, s, ATT_W), BF16),
        grid_spec=grid_spec,
        compiler_params=_cparams(("parallel", "arbitrary")),
        name="sparse_attn",
    )(qmin, kmax, qmin_sub, kmax_sub, relb,
      p_q3, p_q3, p_kv3, positions.reshape(b, s, 1), positions.reshape(b, nkc, 1, bk))


def _merge_kernel(or_ref, oa_ref, os_ref, gr_ref, ga_ref, gs_ref, wr_ref, wa_ref, ws_ref, wo_ref, x_ref, o_ref):
    j = pl.program_id(1)

    @pl.when(j == 0)
    def _():
        o_ref[...] = jnp.zeros_like(o_ref)

    def branch(o, w, g):
        return jax.nn.sigmoid(g[0].astype(F32)) * jnp.dot(o[...], w[...], preferred_element_type=F32)

    merged = branch(or_ref, wr_ref, gr_ref) + branch(oa_ref, wa_ref, ga_ref) + branch(os_ref, ws_ref, gs_ref)
    o_ref[...] += jnp.dot(merged.astype(BF16), wo_ref[...], preferred_element_type=F32)

    @pl.when(j == pl.num_programs(1) - 1)
    def _():
        o_ref[...] = x_ref[...] + o_ref[...]


def _merge(o_r, o_a, o_s, p_g, wr, wa, ws, wo, x2, *, tm):
    t, d = x2.shape
    tn = PROJ_TILE
    per_gate = d // tn

    def gspec(g):
        return pl.BlockSpec((1, tm, tn), lambda i, j: (g * per_gate + j, i, 0))

    return pl.pallas_call(
        _merge_kernel,
        out_shape=jax.ShapeDtypeStruct((t, d), F32),
        grid=(t // tm, per_gate),
        in_specs=[
            pl.BlockSpec((tm, RET_V_W), lambda i, j: (i, 0)),
            pl.BlockSpec((tm, ATT_W), lambda i, j: (i, 0)),
            pl.BlockSpec((tm, SG_W), lambda i, j: (i, 0)),
            gspec(0), gspec(1), gspec(2),
            pl.BlockSpec((RET_V_W, tn), lambda i, j: (0, j)),
            pl.BlockSpec((ATT_W, tn), lambda i, j: (0, j)),
            pl.BlockSpec((SG_W, tn), lambda i, j: (0, j)),
            pl.BlockSpec((tn, d), lambda i, j: (j, 0)),
            pl.BlockSpec((tm, d), lambda i, j: (i, 0)),
        ],
        out_specs=pl.BlockSpec((tm, d), lambda i, j: (i, 0)),
        compiler_params=_cparams(("parallel", "arbitrary")),
        name="merge",
    )(o_r, o_a, o_s, p_g, p_g, p_g, wr, wa, ws, wo, x2)


def _relayout_w_in(w, d_model):
    sizes = (RET_QK_W, RET_QK_W, RET_V_W, RET_V_W, ATT_W, ATT_W, ATT_W, IDX_Q_W, IDX_DIM, IDX_HEADS,
             SG_W, SG_W, d_model, d_model, d_model)
    offs = np.concatenate([[0], np.cumsum(sizes)])
    (q_r, k_r, v_r, g_r, q_a, k_a, v_a, q_i, k_i, w_i, u_s, v_s, gt_r, gt_a, gt_s) = [
        w[:, offs[n]:offs[n + 1]] for n in range(len(sizes))]
    rows = w.shape[0]

    def zeros(n):
        return jnp.zeros((rows, n), w.dtype)

    loc = [q_r, k_r, v_r, g_r, u_s, v_s]
    loc.append(zeros(LOC_W - sum(a.shape[1] for a in loc)))
    pq = [q_a, q_i, k_i, w_i]
    pq.append(zeros(PQ_W - sum(a.shape[1] for a in pq)))
    out = jnp.concatenate(loc + pq + [k_a, v_a, gt_r, gt_a, gt_s], axis=1).astype(BF16)
    assert out.shape[1] == LOC_W + PQ_W + KV_W + 3 * d_model
    return out


def _pick(n, prefs):
    for p in prefs:
        if n % p == 0:
            return p
    return n


def kernel(x, positions, rel_table, ffn1_norm, ffn1_w_gate, ffn1_w_up, ffn1_w_down, mix_norm, w_in, ret_norm, sg_norm, sg_w, sg_b, w_br_ret, w_br_att, w_br_sg, w_out, ffn2_norm, ffn2_w_gate, ffn2_w_up, ffn2_w_down, final_norm):
    b, s, d = x.shape
    t = b * s
    depth = w_in.shape[0]
    d_ff = ffn1_w_gate.shape[2]
    assert s % RET_CHUNK == 0 and d % PROJ_TILE == 0

    tm = _pick(t, (512, 256, 128))
    tm_proj = _pick(t, (1024, 512, 256, 128))
    tf = _pick(d_ff, (512, 256, 128))
    bq = _pick(s, (256, 128))
    bk = _pick(s, (512, 256, 128))

    cos_t, sin_t = _rope_tables(positions, tm=_pick(s, (2048, 1024, 512, 256, 128)))
    consts = _retention_constants()
    x2 = x.reshape(t, d)
    for l in range(depth):
        x2 = _ffn(x2, ffn1_norm[l], ffn1_w_gate[l].astype(BF16), ffn1_w_up[l].astype(BF16),
                  ffn1_w_down[l].astype(BF16), tm=tm, tf=tf)
        p_loc, p_q, p_kv, p_g = _proj(x2, mix_norm[l], _relayout_w_in(w_in[l], d), tm=tm_proj)
        o_r, o_s = _local(p_loc.reshape(b, s, -1), cos_t, sin_t, ret_norm[l], sg_norm[l], sg_w[l], sg_b[l], consts)
        o_a = _attn(p_q.reshape(b, s, -1), p_kv.reshape(b, s, -1), positions, rel_table, bq=bq, bk=bk)
        x2 = _merge(o_r.reshape(t, -1), o_a.reshape(t, -1), o_s.reshape(t, -1), p_g,
                    w_br_ret[l].astype(BF16), w_br_att[l].astype(BF16), w_br_sg[l].astype(BF16),
                    w_out[l].astype(BF16), x2, tm=tm)
        x2 = _ffn(x2, ffn2_norm[l], ffn2_w_gate[l].astype(BF16), ffn2_w_up[l].astype(BF16),
                  ffn2_w_down[l].astype(BF16), final_norm if l == depth - 1 else None, tm=tm, tf=tf)
    return x2.reshape(b, s, d)
```

```python
import functools
import math

import numpy as np
import jax
import jax.numpy as jnp
from jax import lax
from jax.experimental import pallas as pl
from jax.experimental.pallas import tpu as pltpu

F32 = jnp.float32
BF16 = jnp.bfloat16
I32 = jnp.int32

RET_HEADS, RET_QK_DIM, RET_V_DIM, RET_CHUNK = 6, 64, 128, 128
ATT_HEADS, ATT_HEAD_DIM = 6, 128
IDX_HEADS, IDX_DIM = 4, 64
TOPK_MAX = 256
SG_GROUPS, SG_GROUP_DIM, SG_CHUNK = 4, 128, 128
REL_BUCKETS, REL_MAX_DIST = 32, 128
ROPE_BASE = 10000.0
EPS = 1e-6

RET_QK_W = RET_HEADS * RET_QK_DIM
RET_V_W = RET_HEADS * RET_V_DIM
ATT_W = ATT_HEADS * ATT_HEAD_DIM
IDX_Q_W = IDX_HEADS * IDX_DIM
SG_W = SG_GROUPS * SG_GROUP_DIM

LANES = 128
COUNT_ROWS = 64
KV_BUFFERS = 3
VMEM_LIMIT_BYTES = 56 * 1024 * 1024

PROJ_TILE = 512
LOC_QR, LOC_KR, LOC_VR, LOC_GR, LOC_US, LOC_VS = 0, 384, 768, 1536, 2304, 2816
LOC_W = 3584
PQ_QA, PQ_QI, PQ_KIW = 0, 768, 1024
PQ_W = 1536
KV_K, KV_V = 0, 768
KV_W = 1536

INT_MIN = np.int32(-2 ** 31)
INT_MAX = np.int32(2 ** 31 - 1)
NEG_INF = float("-inf")
LOG2E = math.log2(math.e)


def _cparams(sem):
    return pltpu.CompilerParams(dimension_semantics=sem, vmem_limit_bytes=VMEM_LIMIT_BYTES)


def _rms(x, g):
    return x * lax.rsqrt(jnp.mean(x * x, axis=-1, keepdims=True) + EPS) * g


def _ffn_kernel(x_ref, g_ref, wg_ref, wu_ref, wd_ref, *rest, final_norm):
    if final_norm:
        fg_ref, o_ref, xn_ref = rest
    else:
        o_ref, xn_ref = rest
    j = pl.program_id(1)

    @pl.when(j == 0)
    def _():
        xn_ref[...] = _rms(x_ref[...], g_ref[...]).astype(BF16)
        o_ref[...] = jnp.zeros_like(o_ref)

    xn = xn_ref[...]
    a = jnp.dot(xn, wg_ref[...], preferred_element_type=F32)
    b = jnp.dot(xn, wu_ref[...], preferred_element_type=F32)
    h = (a * jax.nn.sigmoid(a) * b).astype(BF16)
    o_ref[...] += jnp.dot(h, wd_ref[...], preferred_element_type=F32)

    @pl.when(j == pl.num_programs(1) - 1)
    def _():
        y = x_ref[...] + 0.5 * o_ref[...]
        if final_norm:
            y = _rms(y, fg_ref[...])
        o_ref[...] = y


def _ffn(x2, gain, wg, wu, wd, final_gain=None, *, tm, tf):
    t, d = x2.shape
    f = wg.shape[1]
    final_norm = final_gain is not None
    in_specs = [
        pl.BlockSpec((tm, d), lambda i, j: (i, 0)),
        pl.BlockSpec((1, d), lambda i, j: (0, 0)),
        pl.BlockSpec((d, tf), lambda i, j: (0, j)),
        pl.BlockSpec((d, tf), lambda i, j: (0, j)),
        pl.BlockSpec((tf, d), lambda i, j: (j, 0)),
    ]
    args = [x2, gain.reshape(1, d), wg, wu, wd]
    if final_norm:
        in_specs.append(pl.BlockSpec((1, d), lambda i, j: (0, 0)))
        args.append(final_gain.reshape(1, d))
    return pl.pallas_call(
        functools.partial(_ffn_kernel, final_norm=final_norm),
        out_shape=jax.ShapeDtypeStruct((t, d), F32),
        grid=(t // tm, f // tf),
        in_specs=in_specs,
        out_specs=pl.BlockSpec((tm, d), lambda i, j: (i, 0)),
        scratch_shapes=[pltpu.VMEM((tm, d), BF16)],
        compiler_params=_cparams(("parallel", "arbitrary")),
        name="ffn",
    )(*args)


def _proj_kernel(x_ref, g_ref, w_ref, ol_ref, oq_ref, okv_ref, og_ref, xn_ref, *, t_loc, t_q, t_kv):
    j = pl.program_id(1)

    @pl.when(j == 0)
    def _():
        xn_ref[...] = _rms(x_ref[...], g_ref[...]).astype(BF16)

    y = jnp.dot(xn_ref[...], w_ref[...], preferred_element_type=F32).astype(BF16)

    @pl.when(j < t_loc)
    def _():
        ol_ref[...] = y

    @pl.when(jnp.logical_and(j >= t_loc, j < t_loc + t_q))
    def _():
        oq_ref[...] = y

    @pl.when(jnp.logical_and(j >= t_loc + t_q, j < t_loc + t_q + t_kv))
    def _():
        okv_ref[...] = y

    @pl.when(j >= t_loc + t_q + t_kv)
    def _():
        og_ref[0] = y


def _proj(x2, gain, w, *, tm):
    t, d = x2.shape
    tn = PROJ_TILE
    n_tiles = w.shape[1] // tn
    t_loc, t_q, t_kv = LOC_W // tn, PQ_W // tn, KV_W // tn
    t_g = n_tiles - t_loc - t_q - t_kv

    def clampspec(first, count):
        return pl.BlockSpec((tm, tn), lambda i, j: (i, jnp.clip(j - first, 0, count - 1)))

    return pl.pallas_call(
        functools.partial(_proj_kernel, t_loc=t_loc, t_q=t_q, t_kv=t_kv),
        out_shape=(jax.ShapeDtypeStruct((t, LOC_W), BF16),
                   jax.ShapeDtypeStruct((t, PQ_W), BF16),
                   jax.ShapeDtypeStruct((t, KV_W), BF16),
                   jax.ShapeDtypeStruct((t_g, t, tn), BF16)),
        grid=(t // tm, n_tiles),
        in_specs=[
            pl.BlockSpec((tm, d), lambda i, j: (i, 0)),
            pl.BlockSpec((1, d), lambda i, j: (0, 0)),
            pl.BlockSpec((d, tn), lambda i, j: (0, j)),
        ],
        out_specs=(clampspec(0, t_loc), clampspec(t_loc, t_q), clampspec(t_loc + t_q, t_kv),
                   pl.BlockSpec((1, tm, tn),
                                lambda i, j: (jnp.clip(j - (t_loc + t_q + t_kv), 0, t_g - 1), i, 0))),
        scratch_shapes=[pltpu.VMEM((tm, d), BF16)],
        compiler_params=_cparams(("arbitrary", "arbitrary")),
        name="proj",
    )(x2, gain.reshape(1, d), w)


def _rope_table_kernel(pos_ref, invf_ref, sign_ref, c_ref, s_ref):
    ang = pos_ref[0].astype(F32) * invf_ref[...]
    c_ref[0] = jnp.cos(ang)
    s_ref[0] = jnp.sin(ang) * sign_ref[...]


def _rope_tables(positions, *, tm):
    b, s = positions.shape
    half = RET_QK_DIM // 2
    inv_freq = ROPE_BASE ** (-jnp.arange(0, RET_QK_DIM, 2, dtype=F32) / RET_QK_DIM)
    lane = np.arange(LANES)
    invf = inv_freq[lane % half].reshape(1, LANES)
    sign = jnp.asarray(np.where(lane % RET_QK_DIM < half, -1.0, 1.0), F32).reshape(1, LANES)
    out = jax.ShapeDtypeStruct((b, s, LANES), F32)
    return pl.pallas_call(
        _rope_table_kernel,
        out_shape=(out, out),
        grid=(b, s // tm),
        in_specs=[
            pl.BlockSpec((1, tm, 1), lambda i, j: (i, j, 0)),
            pl.BlockSpec((1, LANES), lambda i, j: (0, 0)),
            pl.BlockSpec((1, LANES), lambda i, j: (0, 0)),
        ],
        out_specs=(pl.BlockSpec((1, tm, LANES), lambda i, j: (i, j, 0)),
                   pl.BlockSpec((1, tm, LANES), lambda i, j: (i, j, 0))),
        compiler_params=_cparams(("parallel", "parallel")),
        name="rope_tables",
    )(positions.reshape(b, s, 1), invf, sign)


def _retention_constants():
    c = RET_CHUNK
    h = np.arange(RET_HEADS, dtype=np.float64)
    log_g = np.log(1.0 - 2.0 ** (-5.0 - h))
    idx = np.arange(c, dtype=np.float64)
    rel = idx[:, None] - idx[None, :]
    dmat = np.where(rel >= 0, np.exp(np.maximum(rel, 0.0) * log_g[:, None, None]), 0.0)
    k_decay = np.exp((c - 1 - idx)[None, :] * log_g[:, None])
    q_decay = np.exp((idx + 1)[None, :] * log_g[:, None])
    chunk_decay = np.exp(c * log_g)
    kd = np.repeat(k_decay.T, RET_QK_DIM, axis=1)
    qd = np.repeat(q_decay.T, RET_QK_DIM, axis=1) * RET_QK_DIM ** -0.5
    lane = np.arange(RET_QK_W)
    src = (lane // RET_QK_DIM) * RET_QK_DIM + (lane % RET_QK_DIM + RET_QK_DIM // 2) % RET_QK_DIM
    perm = np.zeros((RET_QK_W, RET_QK_W), np.float32)
    perm[src, lane] = 1.0
    return (jnp.asarray(dmat, F32), jnp.asarray(qd, F32), jnp.asarray(kd, F32),
            [float(np.float32(v)) for v in chunk_decay], jnp.asarray(perm, BF16))


def _gelu_tanh(x):
    return 0.5 * x * (1.0 + jnp.tanh(math.sqrt(2.0 / math.pi) * (x + 0.044715 * (x * x * x))))


def _local_kernel(p_ref, cos_ref, sin_ref,
                  perm_ref, dmat_ref, qd_ref, kd_ref, rn_ref, sn_ref, sgw_ref, sgb_ref,
                  or_ref, os_ref, state_ref, *, chunk_decay):
    @pl.when(pl.program_id(1) == 0)
    def _():
        state_ref[...] = jnp.zeros_like(state_ref)

    n_rep = RET_QK_W // LANES
    cos3 = jnp.concatenate([cos_ref[0]] * n_rep, axis=1)
    sin3 = jnp.concatenate([sin_ref[0]] * n_rep, axis=1)
    q = p_ref[0, :, LOC_QR:LOC_QR + RET_QK_W]
    k = p_ref[0, :, LOC_KR:LOC_KR + RET_QK_W]
    perm = perm_ref[...]
    qr = q.astype(F32) * cos3 + jnp.dot(q, perm, preferred_element_type=F32) * sin3
    kr = k.astype(F32) * cos3 + jnp.dot(k, perm, preferred_element_type=F32) * sin3
    q_in = (qr * (RET_QK_DIM ** -0.5)).astype(BF16)
    q_cr = (qr * qd_ref[...]).astype(BF16)
    k_b = kr.astype(BF16)
    k_dec_t = jnp.transpose(kr * kd_ref[...]).astype(BF16)
    v = p_ref[0, :, LOC_VR:LOC_VR + RET_V_W]
    outs = []
    for h in range(RET_HEADS):
        qs = slice(h * RET_QK_DIM, (h + 1) * RET_QK_DIM)
        vh = v[:, h * RET_V_DIM:(h + 1) * RET_V_DIM]
        a = lax.dot_general(q_in[:, qs], k_b[:, qs], (((1,), (1,)), ((), ())),
                            preferred_element_type=F32) * dmat_ref[h]
        prev = state_ref[h]
        o = (jnp.dot(a.astype(BF16), vh, preferred_element_type=F32)
             + jnp.dot(q_cr[:, qs], prev.astype(BF16), preferred_element_type=F32))
        state_ref[h] = prev * chunk_decay[h] + jnp.dot(k_dec_t[qs, :], vh, preferred_element_type=F32)
        mu = jnp.mean(o, axis=-1, keepdims=True)
        d = o - mu
        var = jnp.mean(d * d, axis=-1, keepdims=True)
        outs.append(d * lax.rsqrt(var + EPS))
    g = p_ref[0, :, LOC_GR:LOC_GR + RET_V_W].astype(F32)
    y = jnp.concatenate(outs, axis=1) * rn_ref[...]
    or_ref[0] = (y * (g * jax.nn.sigmoid(g))).astype(BF16)

    u = _gelu_tanh(p_ref[0, :, LOC_US:LOC_US + SG_W].astype(F32))
    vn = _rms(_gelu_tanh(p_ref[0, :, LOC_VS:LOC_VS + SG_W].astype(F32)), sn_ref[...])
    vn_b = vn.astype(BF16)
    c = SG_CHUNK
    tril = lax.broadcasted_iota(I32, (c, c), 0) >= lax.broadcasted_iota(I32, (c, c), 1)
    bias = sgb_ref[...]
    mixed = []
    for gi in range(SG_GROUPS):
        w = jnp.where(tril, sgw_ref[gi], jnp.zeros((), BF16))
        m = jnp.dot(w, vn_b[:, gi * SG_GROUP_DIM:(gi + 1) * SG_GROUP_DIM], preferred_element_type=F32)
        mixed.append(m + bias[:, gi:gi + 1])
    os_ref[0] = (u * jnp.concatenate(mixed, axis=1)).astype(BF16)


def _local(p_loc3, cos_t, sin_t, ret_norm, sg_norm, sg_w, sg_b, consts):
    b, s, _ = p_loc3.shape
    c = RET_CHUNK
    dmat, qd, kd, chunk_decay, perm = consts

    def whole(arr):
        nd = arr.ndim
        return pl.BlockSpec(arr.shape, lambda i, j: (0,) * nd)

    rn = ret_norm.reshape(1, RET_V_W)
    sn = sg_norm.reshape(1, SG_W)
    sgw = sg_w.astype(BF16)
    sgb_t = jnp.transpose(sg_b)
    tab = pl.BlockSpec((1, c, LANES), lambda i, j: (i, j, 0))
    return pl.pallas_call(
        functools.partial(_local_kernel, chunk_decay=chunk_decay),
        out_shape=(jax.ShapeDtypeStruct((b, s, RET_V_W), BF16),
                   jax.ShapeDtypeStruct((b, s, SG_W), BF16)),
        grid=(b, s // c),
        in_specs=[
            pl.BlockSpec((1, c, LOC_W), lambda i, j: (i, j, 0)),
            tab, tab,
            whole(perm), whole(dmat), whole(qd), whole(kd), whole(rn), whole(sn), whole(sgw), whole(sgb_t),
        ],
        out_specs=(pl.BlockSpec((1, c, RET_V_W), lambda i, j: (i, j, 0)),
                   pl.BlockSpec((1, c, SG_W), lambda i, j: (i, j, 0))),
        scratch_shapes=[pltpu.VMEM((RET_HEADS, RET_QK_DIM, RET_V_DIM), F32)],
        compiler_params=_cparams(("parallel", "arbitrary")),
        name="local_mixers",
    )(p_loc3, cos_t, sin_t, perm, dmat, qd, kd, rn, sn, sgw, sgb_t)


def _t5_bucket_table():
    max_exact = REL_BUCKETS // 2
    d = np.arange(REL_MAX_DIST)
    df = np.maximum(d, 1).astype(np.float32)
    large = max_exact + (np.log(df / max_exact) / np.float32(math.log(REL_MAX_DIST / max_exact))
                         * (REL_BUCKETS - max_exact)).astype(np.int32)
    large = np.minimum(large, REL_BUCKETS - 1)
    return np.where(d < max_exact, d, large)


def _bucket_starts():
    bk = _t5_bucket_table()
    assert np.all(np.diff(bk) >= 0) and bk[-1] == REL_BUCKETS - 1
    return [(int(np.argmax(bk == v)), int(v)) for v in sorted(set(bk.tolist())) if v > 0]


def _attn_kernel(qmin_ref, kmax_ref, qmin_sub_ref, kmax_sub_ref, relb_ref,
                 pq_ref, kiw_ref, kv_hbm, posq_ref, posk_ref,
                 o_ref, kv_buf, kv_sem, keys_ref, thr_ref, cnt_ref, m_ref, acc_ref, bias_ref,
                 *, bq, bk, n_keep, seq, far_dist, bucket_starts):
    b = pl.program_id(0)
    qi = pl.program_id(1)
    n_chunks = (qi * bq + bq - 1) // bk + 1
    nlc = bk // LANES

    def kv_copy(cc, slot):
        return pltpu.make_async_copy(kv_hbm.at[b, pl.ds(cc * bk, bk), :], kv_buf.at[slot], kv_sem.at[slot])

    for i in range(KV_BUFFERS - 1):
        @pl.when(i < n_chunks)
        def _():
            kv_copy(i, i).start()
    nsub_q, nsub_k = bq // LANES, bk // LANES
    k_f = float(n_keep)

    def tile_lanes(x, n=None):
        return jnp.concatenate([x] * (nlc if n is None else n), axis=1)

    def lane_chunk(x, j):
        return x[:, j * LANES:(j + 1) * LANES]

    def _select():
        qiv = pq_ref[0, :, PQ_QI:PQ_QI + IDX_Q_W]
        zpad = jnp.zeros((bq, LANES - IDX_DIM), BF16)
        q_heads = [jnp.concatenate([qiv[:, h * IDX_DIM:(h + 1) * IDX_DIM], zpad], axis=1)
                   for h in range(IDX_HEADS)]
        kiw_q = pq_ref[0, :, PQ_KIW:PQ_KIW + LANES]
        w = kiw_q[:, IDX_DIM:IDX_DIM + IDX_HEADS].astype(F32)
        w_bc = [jnp.broadcast_to(w[:, h:h + 1], (bq, bk)) for h in range(IDX_HEADS)]
        row_t = qi * bq + lax.broadcasted_iota(I32, (bq, bk), 0)
        lane_s = lax.broadcasted_iota(I32, (bq, bk), 1)

        def score_body(cc, carry):
            off = pl.multiple_of(cc * bk, bk)
            kc = kiw_ref[0, pl.ds(off, bk), :]
            sc = jnp.zeros((bq, bk), F32)
            for h in range(IDX_HEADS):
                sh = lax.dot_general(q_heads[h], kc, (((1,), (1,)), ((), ())), preferred_element_type=F32)
                sc = sc + w_bc[h] * jnp.maximum(sh, 0.0)
            sc = jnp.where(sc == 0.0, 0.0, sc)
            bits = lax.bitcast_convert_type(sc, I32)
            key = jnp.where(bits < 0, bits ^ INT_MAX, bits)
            keys_ref[cc] = jnp.where(cc * bk + lane_s <= row_t, key, INT_MIN)
            return carry

        lax.fori_loop(0, n_chunks, score_body, 0)

        def count_ge(thr):
            thr_ref[...] = thr
            cnt_ref[...] = jnp.zeros((bq, LANES), F32)

            def body(cc, carry):
                for r in range(bq // COUNT_ROWS):
                    rows = pl.ds(r * COUNT_ROWS, COUNT_ROWS)
                    t = keys_ref[cc, rows, :]
                    th = thr_ref[rows, :]
                    acc = cnt_ref[rows, :]
                    for j in range(nlc):
                        acc = acc + jnp.where(lane_chunk(t, j) >= th, 1.0, 0.0)
                    cnt_ref[rows, :] = acc
                return carry

            lax.fori_loop(0, n_chunks, body, 0)
            return jnp.broadcast_to(jnp.sum(cnt_ref[...], axis=1, keepdims=True), (bq, LANES))

        def bis(_, lohi):
            lo, hi = lohi
            mid = (lo & hi) + ((lo ^ hi) >> 1)
            ok = count_ge(mid) >= k_f
            return jnp.where(ok, mid, lo), jnp.where(ok, hi, mid)

        tau, _ = lax.fori_loop(0, 32, bis, (jnp.full((bq, LANES), INT_MIN, I32),
                                            jnp.full((bq, LANES), INT_MAX, I32)))
        c_gt = count_ge(tau + 1)
        need = jnp.where(tau == INT_MIN, 0.0, k_f - c_gt)
        upper = lax.broadcasted_iota(I32, (LANES, LANES), 0) <= lax.broadcasted_iota(I32, (LANES, LANES), 1)
        scan = jnp.concatenate([jnp.where(upper, 1.0, 0.0), jnp.ones((LANES, LANES), F32)], axis=1).astype(BF16)
        cnt_ref[...] = jnp.zeros((bq, LANES), F32)

        def build_mask(cc, carry):
            t = keys_ref[cc]
            cols = []
            seen = cnt_ref[...]
            for j in range(nlc):
                tj = lane_chunk(t, j)
                tie = tj == tau
                pt = jnp.dot(jnp.where(tie, 1.0, 0.0).astype(BF16), scan, preferred_element_type=F32)
                keep_tie = jnp.logical_and(tie, seen + pt[:, :LANES] <= need)
                cols.append(jnp.where(jnp.logical_or(tj > tau, keep_tie), 0.0, NEG_INF))
                seen = seen + pt[:, LANES:]
            cnt_ref[...] = seen
            keys_ref[cc] = lax.bitcast_convert_type(jnp.concatenate(cols, axis=1), I32)
            return carry

        lax.fori_loop(0, n_chunks, build_mask, 0)

        m_ref[...] = jnp.full(m_ref.shape, NEG_INF, F32)
        acc_ref[...] = jnp.zeros_like(acc_ref)

    _select()

    c1 = (ATT_HEAD_DIM ** -0.5) * LOG2E
    q = pq_ref[0, :, PQ_QA:PQ_QA + ATT_W]
    ones_v = jnp.ones((bk, ATT_HEAD_DIM), BF16)

    def heads(near, maskf, kk, vv):
        for h in range(ATT_HEADS):
            hs = slice(h * ATT_HEAD_DIM, (h + 1) * ATT_HEAD_DIM)
            s = lax.dot_general(q[:, hs], kk[:, hs], (((1,), (1,)), ((), ())), preferred_element_type=F32)
            t = s * c1 + maskf
            if near:
                t = t + bias_ref[h]
            m_old = m_ref[h]
            m_new = jnp.maximum(m_old, jnp.max(t, axis=1, keepdims=True))
            m_safe = jnp.where(m_new == NEG_INF, 0.0, m_new)
            alpha = jnp.exp2(m_old - m_safe)
            p = jnp.exp2(t - tile_lanes(m_safe)).astype(BF16)
            v_aug = jnp.concatenate([vv[:, hs], ones_v], axis=1)
            acc_ref[h] = tile_lanes(alpha, 2) * acc_ref[h] + jnp.dot(p, v_aug, preferred_element_type=F32)
            m_ref[h] = m_new

    def attend(c, carry):
        slot = c % KV_BUFFERS
        nxt = c + (KV_BUFFERS - 1)

        @pl.when(nxt < n_chunks)
        def _():
            kv_copy(nxt, nxt % KV_BUFFERS).start()

        kv_copy(c, slot).wait()
        maskf = lax.bitcast_convert_type(keys_ref[c], F32)
        kk = kv_buf[slot, :, KV_K:KV_K + ATT_W]
        vv = kv_buf[slot, :, KV_V:KV_V + ATT_W]
        far = qmin_ref[b * (seq // bq) + qi] - kmax_ref[b * (seq // bk) + c] >= far_dist

        @pl.when(far)
        def _():
            heads(False, maskf, kk, vv)

        @pl.when(jnp.logical_not(far))
        def _():
            nsub = seq // LANES
            for r in range(nsub_q):
                for j in range(nsub_k):
                    rows = slice(r * LANES, (r + 1) * LANES)
                    cols = slice(j * LANES, (j + 1) * LANES)
                    sub_far = (qmin_sub_ref[b * nsub + qi * nsub_q + r]
                               - kmax_sub_ref[b * nsub + c * nsub_k + j]) >= far_dist

                    @pl.when(sub_far)
                    def _():
                        for h in range(ATT_HEADS):
                            bias_ref[h, rows, cols] = jnp.zeros((LANES, LANES), F32)

                    @pl.when(jnp.logical_not(sub_far))
                    def _():
                        half = LANES // 2
                        for hr in range(2):
                            rr = slice(r * LANES + hr * half, r * LANES + (hr + 1) * half)
                            dist = posq_ref[0, rr, :] - posk_ref[0, c, :, cols]
                            bias = [jnp.full((half, LANES), relb_ref[h], F32) for h in range(ATT_HEADS)]
                            for start, bucket in bucket_starts:
                                ge = dist >= start
                                for h in range(ATT_HEADS):
                                    bias[h] = jnp.where(ge, relb_ref[bucket * ATT_HEADS + h], bias[h])
                            for h in range(ATT_HEADS):
                                bias_ref[h, rr, cols] = bias[h]

            heads(True, maskf, kk, vv)

        return carry

    lax.fori_loop(0, n_chunks, attend, 0)

    outs = []
    for h in range(ATT_HEADS):
        a = acc_ref[h]
        outs.append(a[:, :ATT_HEAD_DIM] / a[:, ATT_HEAD_DIM:])
    o_ref[0] = jnp.concatenate(outs, axis=1).astype(BF16)


def _attn(p_q3, p_kv3, positions, rel_table, *, bq, bk):
    b, s, _ = p_q3.shape
    assert bq % LANES == 0 and bk % LANES == 0
    n_keep = min(TOPK_MAX, s // 4)
    nq, nkc, nsub = s // bq, s // bk, s // LANES
    qmin =jnp.min(positions.reshape(b, nq, bq), axis=-1).reshape(-1)
    kmax = jnp.max(positions.reshape(b, nkc, bk), axis=-1).reshape(-1)
    qmin_sub = jnp.min(positions.reshape(b, nsub, LANES), axis=-1).reshape(-1)
    kmax_sub = jnp.max(positions.reshape(b, nsub, LANES), axis=-1).reshape(-1)
    starts = _bucket_starts()
    far_dist = REL_MAX_DIST - 1
    assert starts[-1][0] <= far_dist and starts[-1][1] == REL_BUCKETS - 1
    relb = ((rel_table - rel_table[REL_BUCKETS - 1:REL_BUCKETS, :]) * LOG2E).astype(F32).reshape(-1)

    grid_spec = pltpu.PrefetchScalarGridSpec(
        num_scalar_prefetch=5,
        grid=(b, nq),
        in_specs=[
            pl.BlockSpec((1, bq, PQ_W), lambda i, t, *_: (i, t, 0)),
            pl.BlockSpec((1, s, LANES), lambda i, t, *_: (i, 0, PQ_KIW // LANES)),
            pl.BlockSpec(memory_space=pl.ANY),
            pl.BlockSpec((1, bq, 1), lambda i, t, *_: (i, t, 0)),
            pl.BlockSpec((1, nkc, 1, bk), lambda i, t, *_: (i, 0, 0, 0)),
        ],
        out_specs=pl.BlockSpec((1, bq, ATT_W), lambda i, t, *_: (i, t, 0)),
        scratch_shapes=[
            pltpu.VMEM((KV_BUFFERS, bk, KV_W), BF16),
            pltpu.SemaphoreType.DMA((KV_BUFFERS,)),
            pltpu.VMEM((nkc, bq, bk), I32),
            pltpu.VMEM((bq, LANES), I32),
            pltpu.VMEM((bq, LANES), F32),
            pltpu.VMEM((ATT_HEADS, bq, LANES), F32),
            pltpu.VMEM((ATT_HEADS, bq, 2 * ATT_HEAD_DIM), F32),
            pltpu.VMEM((ATT_HEADS, bq, bk), F32),
        ],
    )
    kern = functools.partial(_attn_kernel, bq=bq, bk=bk, n_keep=n_keep, seq=s,
                             far_dist=far_dist, bucket_starts=starts)
    return pl.pallas_call(
        kern,
        out_shape=jax.ShapeDtypeStruct((b, s, ATT_W), BF16),
        grid_spec=grid_spec,
        compiler_params=_cparams(("parallel", "arbitrary")),
        name="sparse_attn",
    )(qmin, kmax, qmin_sub, kmax_sub, relb,
      p_q3, p_q3, p_kv3, positions.reshape(b, s, 1), positions.reshape(b, nkc, 1, bk))


def _merge_kernel(or_ref, oa_ref, os_ref, gr_ref, ga_ref, gs_ref, wr_ref, wa_ref, ws_ref, wo_ref, x_ref, o_ref):
    j = pl.program_id(1)

    @pl.when(j == 0)
    def _():
        o_ref[...] = jnp.zeros_like(o_ref)

    def branch(o, w, g):
        return jax.nn.sigmoid(g[0].astype(F32)) * jnp.dot(o[...], w[...], preferred_element_type=F32)

    merged = branch(or_ref, wr_ref, gr_ref) + branch(oa_ref, wa_ref, ga_ref) + branch(os_ref, ws_ref, gs_ref)
    o_ref[...] += jnp.dot(merged.astype(BF16), wo_ref[...], preferred_element_type=F32)

    @pl.when(j == pl.num_programs(1) - 1)
    def _():
        o_ref[...] = x_ref[...] + o_ref[...]


def _merge(o_r, o_a, o_s, p_g, wr, wa, ws, wo, x2, *, tm):
    t, d = x2.shape
    tn = PROJ_TILE
    per_gate = d // tn

    def gspec(g):
        return pl.BlockSpec((1, tm, tn), lambda i, j: (g * per_gate + j, i, 0))

    return pl.pallas_call(
        _merge_kernel,
        out_shape=jax.ShapeDtypeStruct((t, d), F32),
        grid=(t // tm, per_gate),
        in_specs=[
            pl.BlockSpec((tm, RET_V_W), lambda i, j: (i, 0)),
            pl.BlockSpec((tm, ATT_W), lambda i, j: (i, 0)),
            pl.BlockSpec((tm, SG_W), lambda i, j: (i, 0)),
            gspec(0), gspec(1), gspec(2),
            pl.BlockSpec((RET_V_W, tn), lambda i, j: (0, j)),
            pl.BlockSpec((ATT_W, tn), lambda i, j: (0, j)),
            pl.BlockSpec((SG_W, tn), lambda i, j: (0, j)),
            pl.BlockSpec((tn, d), lambda i, j: (j, 0)),
            pl.BlockSpec((tm, d), lambda i, j: (i, 0)),
        ],
        out_specs=pl.BlockSpec((tm, d), lambda i, j: (i, 0)),
        compiler_params=_cparams(("parallel", "arbitrary")),
        name="merge",
    )(o_r, o_a, o_s, p_g, p_g, p_g, wr, wa, ws, wo, x2)


def _relayout_w_in(w, d_model):
    sizes = (RET_QK_W, RET_QK_W, RET_V_W, RET_V_W, ATT_W, ATT_W, ATT_W, IDX_Q_W, IDX_DIM, IDX_HEADS,
             SG_W, SG_W, d_model, d_model, d_model)
    offs = np.concatenate([[0], np.cumsum(sizes)])
    (q_r, k_r, v_r, g_r, q_a, k_a, v_a, q_i, k_i, w_i, u_s, v_s, gt_r, gt_a, gt_s) = [
        w[:, offs[n]:offs[n + 1]] for n in range(len(sizes))]
    rows = w.shape[0]

    def zeros(n):
        return jnp.zeros((rows, n), w.dtype)

    loc = [q_r, k_r, v_r, g_r, u_s, v_s]
    loc.append(zeros(LOC_W - sum(a.shape[1] for a in loc)))
    pq = [q_a, q_i, k_i, w_i]
    pq.append(zeros(PQ_W - sum(a.shape[1] for a in pq)))
    out = jnp.concatenate(loc + pq + [k_a, v_a, gt_r, gt_a, gt_s], axis=1).astype(BF16)
    assert out.shape[1] == LOC_W + PQ_W + KV_W + 3 * d_model
    return out


def _pick(n, prefs):
    for p in prefs:
        if n % p == 0:
            return p
    return n


def kernel(x, positions, rel_table, ffn1_norm, ffn1_w_gate, ffn1_w_up, ffn1_w_down, mix_norm, w_in, ret_norm, sg_norm, sg_w, sg_b, w_br_ret, w_br_att, w_br_sg, w_out, ffn2_norm, ffn2_w_gate, ffn2_w_up, ffn2_w_down, final_norm):
    b, s, d = x.shape
    t = b * s
    depth = w_in.shape[0]
    d_ff = ffn1_w_gate.shape[2]
    assert s % RET_CHUNK == 0 and d % PROJ_TILE == 0

    tm = _pick(t, (512, 256, 128))
    tm_proj = _pick(t, (1024, 512, 256, 128))
    tf = _pick(d_ff, (512, 256, 128))
    bq = _pick(s, (256, 128))
    bk = _pick(s, (512, 256, 128))

    cos_t, sin_t = _rope_tables(positions, tm=_pick(s, (2048, 1024, 512, 256, 128)))
    consts = _retention_constants()
    x2 = x.reshape(t, d)
    for l in range(depth):
        x2 = _ffn(x2, ffn1_norm[l], ffn1_w_gate[l].astype(BF16), ffn1_w_up[l].astype(BF16),
                  ffn1_w_down[l].astype(BF16), tm=tm, tf=tf)
        p_loc, p_q, p_kv, p_g = _proj(x2, mix_norm[l], _relayout_w_in(w_in[l], d), tm=tm_proj)
        o_r, o_s = _local(p_loc.reshape(b, s, -1), cos_t, sin_t, ret_norm[l], sg_norm[l], sg_w[l], sg_b[l], consts)
        o_a = _attn(p_q.reshape(b, s, -1), p_kv.reshape(b, s, -1), positions, rel_table, bq=bq, bk=bk)
        x2 = _merge(o_r.reshape(t, -1), o_a.reshape(t, -1), o_s.reshape(t, -1), p_g,
                    w_br_ret[l].astype(BF16), w_br_att[l].astype(BF16), w_br_sg[l].astype(BF16),
                    w_out[l].astype(BF16), x2, tm=tm)
        x2 = _ffn(x2, ffn2_norm[l], ffn2_w_gate[l].astype(BF16), ffn2_w_up[l].astype(BF16),
                  ffn2_w_down[l].astype(BF16), final_norm if l == depth - 1 else None, tm=tm, tf=tf)
    return x2.reshape(b, s, d)
```

```python
import functools
import math

import numpy as np
import jax
import jax.numpy as jnp
from jax import lax
from jax.experimental import pallas as pl
from jax.experimental.pallas import tpu as pltpu

F32 = jnp.float32
BF16 = jnp.bfloat16
I32 = jnp.int32

RET_HEADS, RET_QK_DIM, RET_V_DIM, RET_CHUNK = 6, 64, 128, 128
ATT_HEADS, ATT_HEAD_DIM = 6, 128
IDX_HEADS, IDX_DIM = 4, 64
TOPK_MAX = 256
SG_GROUPS, SG_GROUP_DIM, SG_CHUNK = 4, 128, 128
REL_BUCKETS, REL_MAX_DIST = 32, 128
ROPE_BASE = 10000.0
EPS = 1e-6

RET_QK_W = RET_HEADS * RET_QK_DIM
RET_V_W = RET_HEADS * RET_V_DIM
ATT_W = ATT_HEADS * ATT_HEAD_DIM
IDX_Q_W = IDX_HEADS * IDX_DIM
SG_W = SG_GROUPS * SG_GROUP_DIM

LANES = 128
COUNT_ROWS = 64
KV_BUFFERS = 3
VMEM_LIMIT_BYTES = 56 * 1024 * 1024

PROJ_TILE = 512
W_A_COLS = 2 * RET_QK_W + 2 * RET_V_W + 3 * ATT_W + IDX_Q_W + IDX_DIM + IDX_HEADS
W_A_PAD = 5120
A1_QR, A1_KR, A1_VR, A1_GR, A1_QA = 0, 384, 768, 1536, 2304
A1_W = 3072
KV_K, KV_V = 0, 768
KV_W = 1536
QI_Q, QI_KIW = 0, 256
QI_W = 512
SG_U, SG_V = 0, 512
SG_PW = 1024

INT_MIN = np.int32(-2 ** 31)
INT_MAX = np.int32(2 ** 31 - 1)
NEG_INF = float("-inf")
LOG2E = math.log2(math.e)


def _cparams(sem):
    return pltpu.CompilerParams(dimension_semantics=sem, vmem_limit_bytes=VMEM_LIMIT_BYTES)


def _rms(x, g):
    return x * lax.rsqrt(jnp.mean(x * x, axis=-1, keepdims=True) + EPS) * g


def _ffn_kernel(x_ref, g_ref, wg_ref, wu_ref, wd_ref, *rest, final_norm):
    if final_norm:
        fg_ref, o_ref, xn_ref = rest
    else:
        o_ref, xn_ref = rest
    j = pl.program_id(1)

    @pl.when(j == 0)
    def _():
        xn_ref[...] = _rms(x_ref[...], g_ref[...]).astype(BF16)
        o_ref[...] = jnp.zeros_like(o_ref)

    xn = xn_ref[...]
    a = jnp.dot(xn, wg_ref[...], preferred_element_type=F32)
    b = jnp.dot(xn, wu_ref[...], preferred_element_type=F32)
    h = (a * jax.nn.sigmoid(a) * b).astype(BF16)
    o_ref[...] += jnp.dot(h, wd_ref[...], preferred_element_type=F32)

    @pl.when(j == pl.num_programs(1) - 1)
    def _():
        y = x_ref[...] + 0.5 * o_ref[...]
        if final_norm:
            y = _rms(y, fg_ref[...])
        o_ref[...] = y


def _ffn(x2, gain, wg, wu, wd, final_gain=None, *, tm, tf):
    t, d = x2.shape
    f = wg.shape[1]
    final_norm = final_gain is not None
    in_specs = [
        pl.BlockSpec((tm, d), lambda i, j: (i, 0)),
        pl.BlockSpec((1, d), lambda i, j: (0, 0)),
        pl.BlockSpec((d, tf), lambda i, j: (0, j)),
        pl.BlockSpec((d, tf), lambda i, j: (0, j)),
        pl.BlockSpec((tf, d), lambda i, j: (j, 0)),
    ]
    args = [x2, gain.reshape(1, d), wg, wu, wd]
    if final_norm:
        in_specs.append(pl.BlockSpec((1, d), lambda i, j: (0, 0)))
        args.append(final_gain.reshape(1, d))
    return pl.pallas_call(
        functools.partial(_ffn_kernel, final_norm=final_norm),
        out_shape=jax.ShapeDtypeStruct((t, d), F32),
        grid=(t // tm, f // tf),
        in_specs=in_specs,
        out_specs=pl.BlockSpec((tm, d), lambda i, j: (i, 0)),
        scratch_shapes=[pltpu.VMEM((tm, d), BF16)],
        compiler_params=_cparams(("parallel", "arbitrary")),
        name="ffn",
    )(*args)


def _proj_kernel(x_ref, g_ref, wa_ref, wb_ref, oa1_ref, okv_ref, oqi_ref, osg_ref, og_ref, xn_ref, *, ends):
    j = pl.program_id(1)
    e_a1, e_kv, e_qi, e_sg = ends

    @pl.when(j == 0)
    def _():
        xn_ref[...] = _rms(x_ref[...], g_ref[...]).astype(BF16)

    def tile(w_ref):
        return jnp.dot(xn_ref[...], w_ref[...], preferred_element_type=F32).astype(BF16)

    @pl.when(j < e_a1)
    def _():
        oa1_ref[...] = tile(wa_ref)

    @pl.when(jnp.logical_and(j >= e_a1, j < e_kv))
    def _():
        okv_ref[...] = tile(wa_ref)

    @pl.when(jnp.logical_and(j >= e_kv, j < e_qi))
    def _():
        oqi_ref[...] = tile(wa_ref)

    @pl.when(jnp.logical_and(j >= e_qi, j < e_sg))
    def _():
        osg_ref[...] = tile(wb_ref)

    @pl.when(j >= e_sg)
    def _():
        og_ref[0] = tile(wb_ref)


def _proj(x2, gain, w_a, w_b, *, tm):
    t, d = x2.shape
    tn = PROJ_TILE
    n_a, n_b = w_a.shape[1] // tn, w_b.shape[1] // tn
    e_a1 = A1_W // tn
    e_kv = e_a1 + KV_W // tn
    e_qi = e_kv + QI_W // tn
    e_sg = e_qi + SG_PW // tn
    assert e_qi == n_a
    t_g = n_a + n_b - e_sg

    def clampspec(first, count):
        return pl.BlockSpec((tm, tn), lambda i, j: (i, jnp.clip(j - first, 0, count - 1)))

    return pl.pallas_call(
        functools.partial(_proj_kernel, ends=(e_a1, e_kv, e_qi, e_sg)),
        out_shape=(jax.ShapeDtypeStruct((t, A1_W), BF16),
                   jax.ShapeDtypeStruct((t, KV_W), BF16),
                   jax.ShapeDtypeStruct((t, QI_W), BF16),
                   jax.ShapeDtypeStruct((t, SG_PW), BF16),
                   jax.ShapeDtypeStruct((t_g, t, tn), BF16)),
        grid=(t // tm, n_a + n_b),
        in_specs=[
            pl.BlockSpec((tm, d), lambda i, j: (i, 0)),
            pl.BlockSpec((1, d), lambda i, j: (0, 0)),
            pl.BlockSpec((d, tn), lambda i, j: (0, jnp.clip(j, 0, n_a - 1))),
            pl.BlockSpec((d, tn), lambda i, j: (0, jnp.clip(j - n_a, 0, n_b - 1))),
        ],
        out_specs=(clampspec(0, e_a1), clampspec(e_a1, e_kv - e_a1), clampspec(e_kv, e_qi - e_kv),
                   clampspec(e_qi, e_sg - e_qi),
                   pl.BlockSpec((1, tm, tn), lambda i, j: (jnp.clip(j - e_sg, 0, t_g - 1), i, 0))),
        scratch_shapes=[pltpu.VMEM((tm, d), BF16)],
        compiler_params=_cparams(("arbitrary", "arbitrary")),
        name="proj",
    )(x2, gain.reshape(1, d), w_a, w_b)


def _rope_table_kernel(pos_ref, invf_ref, sign_ref, c_ref, s_ref):
    ang = pos_ref[0].astype(F32) * invf_ref[...]
    c_ref[0] = jnp.cos(ang)
    s_ref[0] = jnp.sin(ang) * sign_ref[...]


def _rope_tables(positions, *, tm):
    b, s = positions.shape
    half = RET_QK_DIM // 2
    inv_freq = ROPE_BASE ** (-jnp.arange(0, RET_QK_DIM, 2, dtype=F32) / RET_QK_DIM)
    lane = np.arange(LANES)
    invf = inv_freq[lane % half].reshape(1, LANES)
    sign = jnp.asarray(np.where(lane % RET_QK_DIM < half, -1.0, 1.0), F32).reshape(1, LANES)
    out = jax.ShapeDtypeStruct((b, s, LANES), F32)
    return pl.pallas_call(
        _rope_table_kernel,
        out_shape=(out, out),
        grid=(b, s // tm),
        in_specs=[
            pl.BlockSpec((1, tm, 1), lambda i, j: (i, j, 0)),
            pl.BlockSpec((1, LANES), lambda i, j: (0, 0)),
            pl.BlockSpec((1, LANES), lambda i, j: (0, 0)),
        ],
        out_specs=(pl.BlockSpec((1, tm, LANES), lambda i, j: (i, j, 0)),
                   pl.BlockSpec((1, tm, LANES), lambda i, j: (i, j, 0))),
        compiler_params=_cparams(("parallel", "parallel")),
        name="rope_tables",
    )(positions.reshape(b, s, 1), invf, sign)


def _retention_constants():
    c = RET_CHUNK
    h = np.arange(RET_HEADS, dtype=np.float64)
    log_g = np.log(1.0 - 2.0 ** (-5.0 - h))
    idx = np.arange(c, dtype=np.float64)
    rel = idx[:, None] - idx[None, :]
    dmat = np.where(rel >= 0, np.exp(np.maximum(rel, 0.0) * log_g[:, None, None]), 0.0)
    k_decay = np.exp((c - 1 - idx)[None, :] * log_g[:, None])
    q_decay = np.exp((idx + 1)[None, :] * log_g[:, None])
    chunk_decay = np.exp(c * log_g)
    kd = np.repeat(k_decay.T, RET_QK_DIM, axis=1)
    qd = np.repeat(q_decay.T, RET_QK_DIM, axis=1) * RET_QK_DIM ** -0.5
    lane = np.arange(RET_QK_W)
    src = (lane // RET_QK_DIM) * RET_QK_DIM + (lane % RET_QK_DIM + RET_QK_DIM // 2) % RET_QK_DIM
    perm = np.zeros((RET_QK_W, RET_QK_W), np.float32)
    perm[src, lane] = 1.0
    return (jnp.asarray(dmat, F32), jnp.asarray(qd, F32), jnp.asarray(kd, F32),
            [float(np.float32(v)) for v in chunk_decay], jnp.asarray(perm, BF16))


def _gelu_tanh(x):
    return 0.5 * x * (1.0 + jnp.tanh(math.sqrt(2.0 / math.pi) * (x + 0.044715 * (x * x * x))))


def _local_kernel(p_ref, psg_ref, cos_ref, sin_ref,
                  perm_ref, dmat_ref, qd_ref, kd_ref, rn_ref, sn_ref, sgw_ref, sgb_ref,
                  or_ref, os_ref, state_ref, *, chunk_decay):
    @pl.when(pl.program_id(1) == 0)
    def _():
        state_ref[...] = jnp.zeros_like(state_ref)

    n_rep = RET_QK_W // LANES
    cos3 = jnp.concatenate([cos_ref[0]] * n_rep, axis=1)
    sin3 = jnp.concatenate([sin_ref[0]] * n_rep, axis=1)
    q = p_ref[0, :, A1_QR:A1_QR + RET_QK_W]
    k = p_ref[0, :, A1_KR:A1_KR + RET_QK_W]
    perm = perm_ref[...]
    qr = q.astype(F32) * cos3 + jnp.dot(q, perm, preferred_element_type=F32) * sin3
    kr = k.astype(F32) * cos3 + jnp.dot(k, perm, preferred_element_type=F32) * sin3
    q_in = (qr * (RET_QK_DIM ** -0.5)).astype(BF16)
    q_cr = (qr * qd_ref[...]).astype(BF16)
    k_b = kr.astype(BF16)
    k_dec_t = jnp.transpose(kr * kd_ref[...]).astype(BF16)
    v = p_ref[0, :, A1_VR:A1_VR + RET_V_W]
    outs = []
    for h in range(RET_HEADS):
        qs = slice(h * RET_QK_DIM, (h + 1) * RET_QK_DIM)
        vh = v[:, h * RET_V_DIM:(h + 1) * RET_V_DIM]
        a = lax.dot_general(q_in[:, qs], k_b[:, qs], (((1,), (1,)), ((), ())),
                            preferred_element_type=F32) * dmat_ref[h]
        prev = state_ref[h]
        o = (jnp.dot(a.astype(BF16), vh, preferred_element_type=F32)
             + jnp.dot(q_cr[:, qs], prev.astype(BF16), preferred_element_type=F32))
        state_ref[h] = prev * chunk_decay[h] + jnp.dot(k_dec_t[qs, :], vh, preferred_element_type=F32)
        mu = jnp.mean(o, axis=-1, keepdims=True)
        d = o - mu
        var = jnp.mean(d * d, axis=-1, keepdims=True)
        outs.append(d * lax.rsqrt(var + EPS))
    g = p_ref[0, :, A1_GR:A1_GR + RET_V_W].astype(F32)
    y = jnp.concatenate(outs, axis=1) * rn_ref[...]
    or_ref[0] = (y * (g * jax.nn.sigmoid(g))).astype(BF16)

    u = _gelu_tanh(psg_ref[0, :, SG_U:SG_U + SG_W].astype(F32))
    vn = _rms(_gelu_tanh(psg_ref[0, :, SG_V:SG_V + SG_W].astype(F32)), sn_ref[...])
    vn_b = vn.astype(BF16)
    c = SG_CHUNK
    tril = lax.broadcasted_iota(I32, (c, c), 0) >= lax.broadcasted_iota(I32, (c, c), 1)
    bias = sgb_ref[...]
    mixed = []
    for gi in range(SG_GROUPS):
        w = jnp.where(tril, sgw_ref[gi], jnp.zeros((), BF16))
        m = jnp.dot(w, vn_b[:, gi * SG_GROUP_DIM:(gi + 1) * SG_GROUP_DIM], preferred_element_type=F32)
        mixed.append(m + bias[:, gi:gi + 1])
    os_ref[0] = (u * jnp.concatenate(mixed, axis=1)).astype(BF16)


def _local(p_a13, p_sg3, cos_t, sin_t, ret_norm, sg_norm, sg_w, sg_b, consts):
    b, s, _ = p_a13.shape
    c = RET_CHUNK
    dmat, qd, kd, chunk_decay, perm = consts

    def whole(arr):
        nd = arr.ndim
        return pl.BlockSpec(arr.shape, lambda i, j: (0,) * nd)

    rn = ret_norm.reshape(1, RET_V_W)
    sn = sg_norm.reshape(1, SG_W)
    sgw = sg_w.astype(BF16)
    sgb_t = jnp.transpose(sg_b)
    tab = pl.BlockSpec((1, c, LANES), lambda i, j: (i, j, 0))
    return pl.pallas_call(
        functools.partial(_local_kernel, chunk_decay=chunk_decay),
        out_shape=(jax.ShapeDtypeStruct((b, s, RET_V_W), BF16),
                   jax.ShapeDtypeStruct((b, s, SG_W), BF16)),
        grid=(b, s // c),
        in_specs=[
            pl.BlockSpec((1, c, A1_W), lambda i, j: (i, j, 0)),
            pl.BlockSpec((1, c, SG_PW), lambda i, j: (i, j, 0)),
            tab, tab,
            whole(perm), whole(dmat), whole(qd), whole(kd), whole(rn), whole(sn), whole(sgw), whole(sgb_t),
        ],
        out_specs=(pl.BlockSpec((1, c, RET_V_W), lambda i, j: (i, j, 0)),
                   pl.BlockSpec((1, c, SG_W), lambda i, j: (i, j, 0))),
        scratch_shapes=[pltpu.VMEM((RET_HEADS, RET_QK_DIM, RET_V_DIM), F32)],
        compiler_params=_cparams(("parallel", "arbitrary")),
        name="local_mixers",
    )(p_a13, p_sg3, cos_t, sin_t, perm, dmat, qd, kd, rn, sn, sgw, sgb_t)


def _t5_bucket_table():
    max_exact = REL_BUCKETS // 2
    d = np.arange(REL_MAX_DIST)
    df = np.maximum(d, 1).astype(np.float32)
    large = max_exact + (np.log(df / max_exact) / np.float32(math.log(REL_MAX_DIST / max_exact))
                         * (REL_BUCKETS - max_exact)).astype(np.int32)
    large = np.minimum(large, REL_BUCKETS - 1)
    return np.where(d < max_exact, d, large)


def _bucket_starts():
    bk = _t5_bucket_table()
    assert np.all(np.diff(bk) >= 0) and bk[-1] == REL_BUCKETS - 1
    return [(int(np.argmax(bk == v)), int(v)) for v in sorted(set(bk.tolist())) if v > 0]


def _attn_kernel(qmin_ref, kmax_ref, qmin_sub_ref, kmax_sub_ref, relb_ref,
                 qa_ref, pqi_ref, kiw_ref, kv_hbm, posq_ref, posk_ref,
                 o_ref, kv_buf, kv_sem, keys_ref, thr_ref, cnt_ref, m_ref, acc_ref, bias_ref,
                 *, bq, bk, n_keep, seq, far_dist, bucket_starts):
    b = pl.program_id(0)
    qi = pl.program_id(1)
    n_chunks = (qi * bq + bq - 1) // bk + 1
    nlc = bk // LANES

    def kv_copy(cc, slot):
        return pltpu.make_async_copy(kv_hbm.at[b, pl.ds(cc * bk, bk), :], kv_buf.at[slot], kv_sem.at[slot])

    for i in range(KV_BUFFERS - 1):
        @pl.when(i < n_chunks)
        def _():
            kv_copy(i, i).start()
    nsub_q, nsub_k = bq // LANES, bk // LANES
    k_f = float(n_keep)

    def tile_lanes(x, n=None):
        return jnp.concatenate([x] * (nlc if n is None else n), axis=1)

    def lane_chunk(x, j):
        return x[:, j * LANES:(j + 1) * LANES]

    def _select():
        qiv = pqi_ref[0, :, QI_Q:QI_Q + IDX_Q_W]
        zpad = jnp.zeros((bq, LANES - IDX_DIM), BF16)
        q_heads = [jnp.concatenate([qiv[:, h * IDX_DIM:(h + 1) * IDX_DIM], zpad], axis=1)
                   for h in range(IDX_HEADS)]
        kiw_q = pqi_ref[0, :, QI_KIW:QI_KIW + LANES]
        w = kiw_q[:, IDX_DIM:IDX_DIM + IDX_HEADS].astype(F32)
        w_bc = [jnp.broadcast_to(w[:, h:h + 1], (bq, bk)) for h in range(IDX_HEADS)]
        row_t = qi * bq + lax.broadcasted_iota(I32, (bq, bk), 0)
        lane_s = lax.broadcasted_iota(I32, (bq, bk), 1)

        def score_body(cc, carry):
            off = pl.multiple_of(cc * bk, bk)
            kc = kiw_ref[0, pl.ds(off, bk), :]
            sc = jnp.zeros((bq, bk), F32)
            for h in range(IDX_HEADS):
                sh = lax.dot_general(q_heads[h], kc, (((1,), (1,)), ((), ())), preferred_element_type=F32)
                sc = sc + w_bc[h] * jnp.maximum(sh, 0.0)
            sc = jnp.where(sc == 0.0, 0.0, sc)
            bits = lax.bitcast_convert_type(sc, I32)
            key = jnp.where(bits < 0, bits ^ INT_MAX, bits)
            keys_ref[cc] = jnp.where(cc * bk + lane_s <= row_t, key, INT_MIN)
            return carry

        lax.fori_loop(0, n_chunks, score_body, 0)

        def count_ge(thr):
            thr_ref[...] = thr
            cnt_ref[...] = jnp.zeros((bq, LANES), F32)

            def body(cc, carry):
                for r in range(bq // COUNT_ROWS):
                    rows = pl.ds(r * COUNT_ROWS, COUNT_ROWS)
                    t = keys_ref[cc, rows, :]
                    th = thr_ref[rows, :]
                    acc = cnt_ref[rows, :]
                    for j in range(nlc):
                        acc = acc + jnp.where(lane_chunk(t, j) >= th, 1.0, 0.0)
                    cnt_ref[rows, :] = acc
                return carry

            lax.fori_loop(0, n_chunks, body, 0)
            return jnp.broadcast_to(jnp.sum(cnt_ref[...], axis=1, keepdims=True), (bq, LANES))

        def bis(_, lohi):
            lo, hi = lohi
            mid = (lo & hi) + ((lo ^ hi) >> 1)
            ok = count_ge(mid) >= k_f
            return jnp.where(ok, mid, lo), jnp.where(ok, hi, mid)

        tau, _ = lax.fori_loop(0, 32, bis, (jnp.full((bq, LANES), INT_MIN, I32),
                                            jnp.full((bq, LANES), INT_MAX, I32)))
        c_gt = count_ge(tau + 1)
        need = jnp.where(tau == INT_MIN, 0.0, k_f - c_gt)
        upper = lax.broadcasted_iota(I32, (LANES, LANES), 0) <= lax.broadcasted_iota(I32, (LANES, LANES), 1)
        scan = jnp.concatenate([jnp.where(upper, 1.0, 0.0), jnp.ones((LANES, LANES), F32)], axis=1).astype(BF16)
        cnt_ref[...] = jnp.zeros((bq, LANES), F32)

        def build_mask(cc, carry):
            t = keys_ref[cc]
            cols = []
            seen = cnt_ref[...]
            for j in range(nlc):
                tj = lane_chunk(t, j)
                tie = tj == tau
                pt = jnp.dot(jnp.where(tie, 1.0, 0.0).astype(BF16), scan, preferred_element_type=F32)
                keep_tie = jnp.logical_and(tie, seen + pt[:, :LANES] <= need)
                cols.append(jnp.where(jnp.logical_or(tj > tau, keep_tie), 0.0, NEG_INF))
                seen = seen + pt[:, LANES:]
            cnt_ref[...] = seen
            keys_ref[cc] = lax.bitcast_convert_type(jnp.concatenate(cols, axis=1), I32)
            return carry

        lax.fori_loop(0, n_chunks, build_mask, 0)

        m_ref[...] = jnp.full(m_ref.shape, NEG_INF, F32)
        acc_ref[...] = jnp.zeros_like(acc_ref)

    _select()

    c1 = (ATT_HEAD_DIM ** -0.5) * LOG2E
    q = qa_ref[0]
    ones_v = jnp.ones((bk, ATT_HEAD_DIM), BF16)

    def heads(near, maskf, kk, vv):
        for h in range(ATT_HEADS):
            hs = slice(h * ATT_HEAD_DIM, (h + 1) * ATT_HEAD_DIM)
            s = lax.dot_general(q[:, hs], kk[:, hs], (((1,), (1,)), ((), ())), preferred_element_type=F32)
            t = s * c1 + maskf
            if near:
                t = t + bias_ref[h]
            m_old = m_ref[h]
            m_new = jnp.maximum(m_old, jnp.max(t, axis=1, keepdims=True))
            m_safe = jnp.where(m_new == NEG_INF, 0.0, m_new)
            alpha = jnp.exp2(m_old - m_safe)
            p = jnp.exp2(t - tile_lanes(m_safe)).astype(BF16)
            v_aug = jnp.concatenate([vv[:, hs], ones_v], axis=1)
            acc_ref[h] = tile_lanes(alpha, 2) * acc_ref[h] + jnp.dot(p, v_aug, preferred_element_type=F32)
            m_ref[h] = m_new

    def attend(c, carry):
        slot = c % KV_BUFFERS
        nxt = c + (KV_BUFFERS - 1)

        @pl.when(nxt < n_chunks)
        def _():
            kv_copy(nxt, nxt % KV_BUFFERS).start()

        kv_copy(c, slot).wait()
        maskf = lax.bitcast_convert_type(keys_ref[c], F32)
        kk = kv_buf[slot, :, KV_K:KV_K + ATT_W]
        vv = kv_buf[slot, :, KV_V:KV_V + ATT_W]
        far = qmin_ref[b * (seq // bq) + qi] - kmax_ref[b * (seq // bk) + c] >= far_dist

        @pl.when(far)
        def _():
            heads(False, maskf, kk, vv)

        @pl.when(jnp.logical_not(far))
        def _():
            nsub = seq // LANES
            for r in range(nsub_q):
                for j in range(nsub_k):
                    rows = slice(r * LANES, (r + 1) * LANES)
                    cols = slice(j * LANES, (j + 1) * LANES)
                    sub_far = (qmin_sub_ref[b * nsub + qi * nsub_q + r]
                               - kmax_sub_ref[b * nsub + c * nsub_k + j]) >= far_dist

                    @pl.when(sub_far)
                    def _():
                        for h in range(ATT_HEADS):
                            bias_ref[h, rows, cols] = jnp.zeros((LANES, LANES), F32)

                    @pl.when(jnp.logical_not(sub_far))
                    def _():
                        half = LANES // 2
                        for hr in range(2):
                            rr = slice(r * LANES + hr * half, r * LANES + (hr + 1) * half)
                            dist = posq_ref[0, rr, :] - posk_ref[0, c, :, cols]
                            bias = [jnp.full((half, LANES), relb_ref[h], F32) for h in range(ATT_HEADS)]
                            for start, bucket in bucket_starts:
                                ge = dist >= start
                                for h in range(ATT_HEADS):
                                    bias[h] = jnp.where(ge, relb_ref[bucket * ATT_HEADS + h], bias[h])
                            for h in range(ATT_HEADS):
                                bias_ref[h, rr, cols] = bias[h]

            heads(True, maskf, kk, vv)

        return carry

    lax.fori_loop(0, n_chunks, attend, 0)

    outs = []
    for h in range(ATT_HEADS):
        a = acc_ref[h]
        outs.append(a[:, :ATT_HEAD_DIM] / a[:, ATT_HEAD_DIM:])
    o_ref[0] = jnp.concatenate(outs, axis=1).astype(BF16)


def _attn(p_a13, p_qi3, p_kv3, positions, rel_table, *, bq, bk):
    b, s, _ = p_qi3.shape
    assert bq % LANES == 0 and bk % LANES == 0
    n_keep = min(TOPK_MAX, s // 4)
    nq, nkc, nsub = s // bq, s // bk, s // LANES
    qmin =jnp.min(positions.reshape(b, nq, bq), axis=-1).reshape(-1)
    kmax = jnp.max(positions.reshape(b, nkc, bk), axis=-1).reshape(-1)
    qmin_sub = jnp.min(positions.reshape(b, nsub, LANES), axis=-1).reshape(-1)
    kmax_sub = jnp.max(positions.reshape(b, nsub, LANES), axis=-1).reshape(-1)
    starts = _bucket_starts()
    far_dist = REL_MAX_DIST - 1
    assert starts[-1][0] <= far_dist and starts[-1][1] == REL_BUCKETS - 1
    relb = ((rel_table - rel_table[REL_BUCKETS - 1:REL_BUCKETS, :]) * LOG2E).astype(F32).reshape(-1)

    grid_spec = pltpu.PrefetchScalarGridSpec(
        num_scalar_prefetch=5,
        grid=(b, nq),
        in_specs=[
            pl.BlockSpec((1, bq, ATT_W), lambda i, t, *_: (i, t, A1_QA // ATT_W)),
            pl.BlockSpec((1, bq, QI_W), lambda i, t, *_: (i, t, 0)),
            pl.BlockSpec((1, s, LANES), lambda i, t, *_: (i, 0, QI_KIW // LANES)),
            pl.BlockSpec(memory_space=pl.ANY),
            pl.BlockSpec((1, bq, 1), lambda i, t, *_: (i, t, 0)),
            pl.BlockSpec((1, nkc, 1, bk), lambda i, t, *_: (i, 0, 0, 0)),
        ],
        out_specs=pl.BlockSpec((1, bq, ATT_W), lambda i, t, *_: (i, t, 0)),
        scratch_shapes=[
            pltpu.VMEM((KV_BUFFERS, bk, KV_W), BF16),
            pltpu.SemaphoreType.DMA((KV_BUFFERS,)),
            pltpu.VMEM((nkc, bq, bk), I32),
            pltpu.VMEM((bq, LANES), I32),
            pltpu.VMEM((bq, LANES), F32),
            pltpu.VMEM((ATT_HEADS, bq, LANES), F32),
            pltpu.VMEM((ATT_HEADS, bq, 2 * ATT_HEAD_DIM), F32),
            pltpu.VMEM((ATT_HEADS, bq, bk), F32),
        ],
    )
    kern = functools.partial(_attn_kernel, bq=bq, bk=bk, n_keep=n_keep, seq=s,
                             far_dist=far_dist, bucket_starts=starts)
    return pl.pallas_call(
        kern,
        out_shape=jax.ShapeDtypeStruct((b, s, ATT_W), BF16),
        grid_spec=grid_spec,
        compiler_params=_cparams(("parallel", "arbitrary")),
        name="sparse_attn",
    )(qmin, kmax, qmin_sub, kmax_sub, relb,
      p_a13, p_qi3, p_qi3, p_kv3, positions.reshape(b, s, 1), positions.reshape(b, nkc, 1, bk))


def _merge_kernel(or_ref, oa_ref, os_ref, gr_ref, ga_ref, gs_ref, wr_ref, wa_ref, ws_ref, wo_ref, x_ref, o_ref):
    j = pl.program_id(1)

    @pl.when(j == 0)
    def _():
        o_ref[...] = jnp.zeros_like(o_ref)

    def branch(o, w, g):
        return jax.nn.sigmoid(g[0].astype(F32)) * jnp.dot(o[...], w[...], preferred_element_type=F32)

    merged = branch(or_ref, wr_ref, gr_ref) + branch(oa_ref, wa_ref, ga_ref) + branch(os_ref, ws_ref, gs_ref)
    o_ref[...] += jnp.dot(merged.astype(BF16), wo_ref[...], preferred_element_type=F32)

    @pl.when(j == pl.num_programs(1) - 1)
    def _():
        o_ref[...] = x_ref[...] + o_ref[...]


def _merge(o_r, o_a, o_s, p_g, wr, wa, ws, wo, x2, *, tm):
    t, d = x2.shape
    tn = PROJ_TILE
    per_gate = d // tn

    def gspec(g):
        return pl.BlockSpec((1, tm, tn), lambda i, j: (g * per_gate + j, i, 0))

    return pl.pallas_call(
        _merge_kernel,
        out_shape=jax.ShapeDtypeStruct((t, d), F32),
        grid=(t // tm, per_gate),
        in_specs=[
            pl.BlockSpec((tm, RET_V_W), lambda i, j: (i, 0)),
            pl.BlockSpec((tm, ATT_W), lambda i, j: (i, 0)),
            pl.BlockSpec((tm, SG_W), lambda i, j: (i, 0)),
            gspec(0), gspec(1), gspec(2),
            pl.BlockSpec((RET_V_W, tn), lambda i, j: (0, j)),
            pl.BlockSpec((ATT_W, tn), lambda i, j: (0, j)),
            pl.BlockSpec((SG_W, tn), lambda i, j: (0, j)),
            pl.BlockSpec((tn, d), lambda i, j: (j, 0)),
            pl.BlockSpec((tm, d), lambda i, j: (i, 0)),
        ],
        out_specs=pl.BlockSpec((tm, d), lambda i, j: (i, 0)),
        compiler_params=_cparams(("parallel", "arbitrary")),
        name="merge",
    )(o_r, o_a, o_s, p_g, p_g, p_g, wr, wa, ws, wo, x2)


def _split_w_in_kernel(w_ref, wa_ref, wb_ref):
    w = w_ref[0]
    wa_ref[:, :W_A_COLS] = w[:, :W_A_COLS].astype(BF16)
    wa_ref[:, W_A_COLS:] = jnp.zeros((w.shape[0], W_A_PAD - W_A_COLS), BF16)
    wb_ref[...] = w[:, W_A_COLS:].astype(BF16)


def _split_w_in(w_in, layer, d_model, *, tr=64):
    _, rows, cols = w_in.shape
    assert cols == W_A_COLS + 2 * SG_W + 3 * d_model and A1_W + KV_W + QI_W == W_A_PAD
    w_b_cols = cols - W_A_COLS
    assert w_b_cols % PROJ_TILE == 0 and rows % tr == 0
    return pl.pallas_call(
        _split_w_in_kernel,
        out_shape=(jax.ShapeDtypeStruct((rows, W_A_PAD), BF16), jax.ShapeDtypeStruct((rows, w_b_cols), BF16)),
        grid=(rows // tr,),
        in_specs=[pl.BlockSpec((1, tr, cols), lambda i: (layer, i, 0))],
        out_specs=(pl.BlockSpec((tr, W_A_PAD), lambda i: (i, 0)), pl.BlockSpec((tr, w_b_cols), lambda i: (i, 0))),
        compiler_params=_cparams(("parallel",)),
        name="split_w_in",
    )(w_in)


def _pick(n, prefs):
    for p in prefs:
        if n % p == 0:
            return p
    return n


def kernel(x, positions, rel_table, ffn1_norm, ffn1_w_gate, ffn1_w_up, ffn1_w_down, mix_norm, w_in, ret_norm, sg_norm, sg_w, sg_b, w_br_ret, w_br_att, w_br_sg, w_out, ffn2_norm, ffn2_w_gate, ffn2_w_up, ffn2_w_down, final_norm):
    b, s, d = x.shape
    t = b * s
    depth = w_in.shape[0]
    d_ff = ffn1_w_gate.shape[2]
    assert s % RET_CHUNK == 0 and d % PROJ_TILE == 0

    tm = _pick(t, (512, 256, 128))
    tm_proj = _pick(t, (1024, 512, 256, 128))
    tf = _pick(d_ff, (512, 256, 128))
    bq = _pick(s, (256, 128))
    bk = _pick(s, (512, 256, 128))

    cos_t, sin_t = _rope_tables(positions, tm=_pick(s, (2048, 1024, 512, 256, 128)))
    consts = _retention_constants()
    x2 = x.reshape(t, d)
    for l in range(depth):
        x2 = _ffn(x2, ffn1_norm[l], ffn1_w_gate[l].astype(BF16), ffn1_w_up[l].astype(BF16),
                  ffn1_w_down[l].astype(BF16), tm=tm, tf=tf)
        p_a1, p_kv, p_qi, p_sg, p_g = _proj(x2, mix_norm[l], *_split_w_in(w_in, l, d), tm=tm_proj)
        p_a13 = p_a1.reshape(b, s, -1)
        o_r, o_s = _local(p_a13, p_sg.reshape(b, s, -1), cos_t, sin_t, ret_norm[l], sg_norm[l], sg_w[l], sg_b[l],
                          consts)
        o_a = _attn(p_a13, p_qi.reshape(b, s, -1), p_kv.reshape(b, s, -1), positions, rel_table, bq=bq, bk=bk)
        x2 = _merge(o_r.reshape(t, -1), o_a.reshape(t, -1), o_s.reshape(t, -1), p_g,
                    w_br_ret[l].astype(BF16), w_br_att[l].astype(BF16), w_br_sg[l].astype(BF16),
                    w_out[l].astype(BF16), x2, tm=tm)
        x2 = _ffn(x2, ffn2_norm[l], ffn2_w_gate[l].astype(BF16), ffn2_w_up[l].astype(BF16),
                  ffn2_w_down[l].astype(BF16), final_norm if l == depth - 1 else None, tm=tm, tf=tf)
    return x2.reshape(b, s, d)
```

```python
import functools
import math

import numpy as np
import jax
import jax.numpy as jnp
from jax import lax
from jax.experimental import pallas as pl
from jax.experimental.pallas import tpu as pltpu

F32 = jnp.float32
BF16 = jnp.bfloat16
I32 = jnp.int32

RET_HEADS, RET_QK_DIM, RET_V_DIM, RET_CHUNK = 6, 64, 128, 128
ATT_HEADS, ATT_HEAD_DIM = 6, 128
IDX_HEADS, IDX_DIM = 4, 64
TOPK_MAX = 256
SG_GROUPS, SG_GROUP_DIM, SG_CHUNK = 4, 128, 128
REL_BUCKETS, REL_MAX_DIST = 32, 128
ROPE_BASE = 10000.0
EPS = 1e-6

RET_QK_W = RET_HEADS * RET_QK_DIM
RET_V_W = RET_HEADS * RET_V_DIM
ATT_W = ATT_HEADS * ATT_HEAD_DIM
IDX_Q_W = IDX_HEADS * IDX_DIM
SG_W = SG_GROUPS * SG_GROUP_DIM

LANES = 128
COUNT_ROWS = 64
KV_BUFFERS = 3
SEARCH_PERIOD = 6
VMEM_LIMIT_BYTES = 56 * 1024 * 1024

PROJ_TILE = 512
W_A_COLS = 2 * RET_QK_W + 2 * RET_V_W + 3 * ATT_W + IDX_Q_W + IDX_DIM + IDX_HEADS
W_A_PAD = 5120
A1_QR, A1_KR, A1_VR, A1_GR, A1_QA = 0, 384, 768, 1536, 2304
A1_W = 3072
KV_K, KV_V = 0, 768
KV_W = 1536
QI_Q, QI_KIW = 0, 256
QI_W = 512
SG_U, SG_V = 0, 512
SG_PW = 1024

INT_MIN = np.int32(-2 ** 31)
INT_MAX = np.int32(2 ** 31 - 1)
NEG_INF = float("-inf")
LOG2E = math.log2(math.e)


def _cparams(sem):
    return pltpu.CompilerParams(dimension_semantics=sem, vmem_limit_bytes=VMEM_LIMIT_BYTES)


def _rms(x, g):
    return x * lax.rsqrt(jnp.mean(x * x, axis=-1, keepdims=True) + EPS) * g


def _ffn_kernel(x_ref, g_ref, wg_ref, wu_ref, wd_ref, *rest, final_norm):
    if final_norm:
        fg_ref, o_ref, xn_ref = rest
    else:
        o_ref, xn_ref = rest
    j = pl.program_id(1)

    @pl.when(j == 0)
    def _():
        xn_ref[...] = _rms(x_ref[...], g_ref[...]).astype(BF16)
        o_ref[...] = jnp.zeros_like(o_ref)

    xn = xn_ref[...]
    a = jnp.dot(xn, wg_ref[...], preferred_element_type=F32)
    b = jnp.dot(xn, wu_ref[...], preferred_element_type=F32)
    h = (a * jax.nn.sigmoid(a) * b).astype(BF16)
    o_ref[...] += jnp.dot(h, wd_ref[...], preferred_element_type=F32)

    @pl.when(j == pl.num_programs(1) - 1)
    def _():
        y = x_ref[...] + 0.5 * o_ref[...]
        if final_norm:
            y = _rms(y, fg_ref[...])
        o_ref[...] = y


def _ffn(x2, gain, wg, wu, wd, final_gain=None, *, tm, tf):
    t, d = x2.shape
    f = wg.shape[1]
    final_norm = final_gain is not None
    in_specs = [
        pl.BlockSpec((tm, d), lambda i, j: (i, 0)),
        pl.BlockSpec((1, d), lambda i, j: (0, 0)),
        pl.BlockSpec((d, tf), lambda i, j: (0, j)),
        pl.BlockSpec((d, tf), lambda i, j: (0, j)),
        pl.BlockSpec((tf, d), lambda i, j: (j, 0)),
    ]
    args = [x2, gain.reshape(1, d), wg, wu, wd]
    if final_norm:
        in_specs.append(pl.BlockSpec((1, d), lambda i, j: (0, 0)))
        args.append(final_gain.reshape(1, d))
    return pl.pallas_call(
        functools.partial(_ffn_kernel, final_norm=final_norm),
        out_shape=jax.ShapeDtypeStruct((t, d), F32),
        grid=(t // tm, f // tf),
        in_specs=in_specs,
        out_specs=pl.BlockSpec((tm, d), lambda i, j: (i, 0)),
        scratch_shapes=[pltpu.VMEM((tm, d), BF16)],
        compiler_params=_cparams(("parallel", "arbitrary")),
        name="ffn",
    )(*args)


def _proj_kernel(x_ref, g_ref, wa_ref, wb_ref, oa1_ref, okv_ref, oqi_ref, osg_ref, og_ref, xn_ref, *, ends):
    j = pl.program_id(1)
    e_a1, e_kv, e_qi, e_sg = ends

    @pl.when(j == 0)
    def _():
        xn_ref[...] = _rms(x_ref[...], g_ref[...]).astype(BF16)

    def tile(w_ref):
        return jnp.dot(xn_ref[...], w_ref[...], preferred_element_type=F32).astype(BF16)

    @pl.when(j < e_a1)
    def _():
        oa1_ref[...] = tile(wa_ref)

    @pl.when(jnp.logical_and(j >= e_a1, j < e_kv))
    def _():
        okv_ref[...] = tile(wa_ref)

    @pl.when(jnp.logical_and(j >= e_kv, j < e_qi))
    def _():
        oqi_ref[...] = tile(wa_ref)

    @pl.when(jnp.logical_and(j >= e_qi, j < e_sg))
    def _():
        osg_ref[...] = tile(wb_ref)

    @pl.when(j >= e_sg)
    def _():
        og_ref[0] = tile(wb_ref)


def _proj(x2, gain, w_a, w_b, *, tm):
    t, d = x2.shape
    tn = PROJ_TILE
    n_a, n_b = w_a.shape[1] // tn, w_b.shape[1] // tn
    e_a1 = A1_W // tn
    e_kv = e_a1 + KV_W // tn
    e_qi = e_kv + QI_W // tn
    e_sg = e_qi + SG_PW // tn
    assert e_qi == n_a
    t_g = n_a + n_b - e_sg

    def clampspec(first, count):
        return pl.BlockSpec((tm, tn), lambda i, j: (i, jnp.clip(j - first, 0, count - 1)))

    return pl.pallas_call(
        functools.partial(_proj_kernel, ends=(e_a1, e_kv, e_qi, e_sg)),
        out_shape=(jax.ShapeDtypeStruct((t, A1_W), BF16),
                   jax.ShapeDtypeStruct((t, KV_W), BF16),
                   jax.ShapeDtypeStruct((t, QI_W), BF16),
                   jax.ShapeDtypeStruct((t, SG_PW), BF16),
                   jax.ShapeDtypeStruct((t_g, t, tn), BF16)),
        grid=(t // tm, n_a + n_b),
        in_specs=[
            pl.BlockSpec((tm, d), lambda i, j: (i, 0)),
            pl.BlockSpec((1, d), lambda i, j: (0, 0)),
            pl.BlockSpec((d, tn), lambda i, j: (0, jnp.clip(j, 0, n_a - 1))),
            pl.BlockSpec((d, tn), lambda i, j: (0, jnp.clip(j - n_a, 0, n_b - 1))),
        ],
        out_specs=(clampspec(0, e_a1), clampspec(e_a1, e_kv - e_a1), clampspec(e_kv, e_qi - e_kv),
                   clampspec(e_qi, e_sg - e_qi),
                   pl.BlockSpec((1, tm, tn), lambda i, j: (jnp.clip(j - e_sg, 0, t_g - 1), i, 0))),
        scratch_shapes=[pltpu.VMEM((tm, d), BF16)],
        compiler_params=_cparams(("arbitrary", "arbitrary")),
        name="proj",
    )(x2, gain.reshape(1, d), w_a, w_b)


def _rope_table_kernel(pos_ref, invf_ref, sign_ref, c_ref, s_ref):
    ang = pos_ref[0].astype(F32) * invf_ref[...]
    c_ref[0] = jnp.cos(ang)
    s_ref[0] = jnp.sin(ang) * sign_ref[...]


def _rope_tables(positions, *, tm):
    b, s = positions.shape
    half = RET_QK_DIM // 2
    inv_freq = ROPE_BASE ** (-jnp.arange(0, RET_QK_DIM, 2, dtype=F32) / RET_QK_DIM)
    lane = np.arange(LANES)
    invf = inv_freq[lane % half].reshape(1, LANES)
    sign = jnp.asarray(np.where(lane % RET_QK_DIM < half, -1.0, 1.0), F32).reshape(1, LANES)
    out = jax.ShapeDtypeStruct((b, s, LANES), F32)
    return pl.pallas_call(
        _rope_table_kernel,
        out_shape=(out, out),
        grid=(b, s // tm),
        in_specs=[
            pl.BlockSpec((1, tm, 1), lambda i, j: (i, j, 0)),
            pl.BlockSpec((1, LANES), lambda i, j: (0, 0)),
            pl.BlockSpec((1, LANES), lambda i, j: (0, 0)),
        ],
        out_specs=(pl.BlockSpec((1, tm, LANES), lambda i, j: (i, j, 0)),
                   pl.BlockSpec((1, tm, LANES), lambda i, j: (i, j, 0))),
        compiler_params=_cparams(("parallel", "parallel")),
        name="rope_tables",
    )(positions.reshape(b, s, 1), invf, sign)


def _retention_constants():
    c = RET_CHUNK
    h = np.arange(RET_HEADS, dtype=np.float64)
    log_g = np.log(1.0 - 2.0 ** (-5.0 - h))
    idx = np.arange(c, dtype=np.float64)
    rel = idx[:, None] - idx[None, :]
    dmat = np.where(rel >= 0, np.exp(np.maximum(rel, 0.0) * log_g[:, None, None]), 0.0)
    k_decay = np.exp((c - 1 - idx)[None, :] * log_g[:, None])
    q_decay = np.exp((idx + 1)[None, :] * log_g[:, None])
    chunk_decay = np.exp(c * log_g)
    kd = np.repeat(k_decay.T, RET_QK_DIM, axis=1)
    qd = np.repeat(q_decay.T, RET_QK_DIM, axis=1) * RET_QK_DIM ** -0.5
    lane = np.arange(RET_QK_W)
    src = (lane // RET_QK_DIM) * RET_QK_DIM + (lane % RET_QK_DIM + RET_QK_DIM // 2) % RET_QK_DIM
    perm = np.zeros((RET_QK_W, RET_QK_W), np.float32)
    perm[src, lane] = 1.0
    return (jnp.asarray(dmat, F32), jnp.asarray(qd, F32), jnp.asarray(kd, F32),
            [float(np.float32(v)) for v in chunk_decay], jnp.asarray(perm, BF16))


def _gelu_tanh(x):
    return 0.5 * x * (1.0 + jnp.tanh(math.sqrt(2.0 / math.pi) * (x + 0.044715 * (x * x * x))))


def _local_kernel(p_ref, psg_ref, cos_ref, sin_ref,
                  perm_ref, dmat_ref, qd_ref, kd_ref, rn_ref, sn_ref, sgw_ref, sgb_ref,
                  or_ref, os_ref, state_ref, *, chunk_decay):
    @pl.when(pl.program_id(1) == 0)
    def _():
        state_ref[...] = jnp.zeros_like(state_ref)

    n_rep = RET_QK_W // LANES
    cos3 = jnp.concatenate([cos_ref[0]] * n_rep, axis=1)
    sin3 = jnp.concatenate([sin_ref[0]] * n_rep, axis=1)
    q = p_ref[0, :, A1_QR:A1_QR + RET_QK_W]
    k = p_ref[0, :, A1_KR:A1_KR + RET_QK_W]
    perm = perm_ref[...]
    qr = q.astype(F32) * cos3 + jnp.dot(q, perm, preferred_element_type=F32) * sin3
    kr = k.astype(F32) * cos3 + jnp.dot(k, perm, preferred_element_type=F32) * sin3
    q_in = (qr * (RET_QK_DIM ** -0.5)).astype(BF16)
    q_cr = (qr * qd_ref[...]).astype(BF16)
    k_b = kr.astype(BF16)
    k_dec_t = jnp.transpose(kr * kd_ref[...]).astype(BF16)
    v = p_ref[0, :, A1_VR:A1_VR + RET_V_W]
    outs = []
    for h in range(RET_HEADS):
        qs = slice(h * RET_QK_DIM, (h + 1) * RET_QK_DIM)
        vh = v[:, h * RET_V_DIM:(h + 1) * RET_V_DIM]
        a = lax.dot_general(q_in[:, qs], k_b[:, qs], (((1,), (1,)), ((), ())),
                            preferred_element_type=F32) * dmat_ref[h]
        prev = state_ref[h]
        o = (jnp.dot(a.astype(BF16), vh, preferred_element_type=F32)
             + jnp.dot(q_cr[:, qs], prev.astype(BF16), preferred_element_type=F32))
        state_ref[h] = prev * chunk_decay[h] + jnp.dot(k_dec_t[qs, :], vh, preferred_element_type=F32)
        mu = jnp.mean(o, axis=-1, keepdims=True)
        d = o - mu
        var = jnp.mean(d * d, axis=-1, keepdims=True)
        outs.append(d * lax.rsqrt(var + EPS))
    g = p_ref[0, :, A1_GR:A1_GR + RET_V_W].astype(F32)
    y = jnp.concatenate(outs, axis=1) * rn_ref[...]
    or_ref[0] = (y * (g * jax.nn.sigmoid(g))).astype(BF16)

    u = _gelu_tanh(psg_ref[0, :, SG_U:SG_U + SG_W].astype(F32))
    vn = _rms(_gelu_tanh(psg_ref[0, :, SG_V:SG_V + SG_W].astype(F32)), sn_ref[...])
    vn_b = vn.astype(BF16)
    c = SG_CHUNK
    tril = lax.broadcasted_iota(I32, (c, c), 0) >= lax.broadcasted_iota(I32, (c, c), 1)
    bias = sgb_ref[...]
    mixed = []
    for gi in range(SG_GROUPS):
        w = jnp.where(tril, sgw_ref[gi], jnp.zeros((), BF16))
        m = jnp.dot(w, vn_b[:, gi * SG_GROUP_DIM:(gi + 1) * SG_GROUP_DIM], preferred_element_type=F32)
        mixed.append(m + bias[:, gi:gi + 1])
    os_ref[0] = (u * jnp.concatenate(mixed, axis=1)).astype(BF16)


def _local(p_a13, p_sg3, cos_t, sin_t, ret_norm, sg_norm, sg_w, sg_b, consts):
    b, s, _ = p_a13.shape
    c = RET_CHUNK
    dmat, qd, kd, chunk_decay, perm = consts

    def whole(arr):
        nd = arr.ndim
        return pl.BlockSpec(arr.shape, lambda i, j: (0,) * nd)

    rn = ret_norm.reshape(1, RET_V_W)
    sn = sg_norm.reshape(1, SG_W)
    sgw = sg_w.astype(BF16)
    sgb_t = jnp.transpose(sg_b)
    tab = pl.BlockSpec((1, c, LANES), lambda i, j: (i, j, 0))
    return pl.pallas_call(
        functools.partial(_local_kernel, chunk_decay=chunk_decay),
        out_shape=(jax.ShapeDtypeStruct((b, s, RET_V_W), BF16),
                   jax.ShapeDtypeStruct((b, s, SG_W), BF16)),
        grid=(b, s // c),
        in_specs=[
            pl.BlockSpec((1, c, A1_W), lambda i, j: (i, j, 0)),
            pl.BlockSpec((1, c, SG_PW), lambda i, j: (i, j, 0)),
            tab, tab,
            whole(perm), whole(dmat), whole(qd), whole(kd), whole(rn), whole(sn), whole(sgw), whole(sgb_t),
        ],
        out_specs=(pl.BlockSpec((1, c, RET_V_W), lambda i, j: (i, j, 0)),
                   pl.BlockSpec((1, c, SG_W), lambda i, j: (i, j, 0))),
        scratch_shapes=[pltpu.VMEM((RET_HEADS, RET_QK_DIM, RET_V_DIM), F32)],
        compiler_params=_cparams(("parallel", "arbitrary")),
        name="local_mixers",
    )(p_a13, p_sg3, cos_t, sin_t, perm, dmat, qd, kd, rn, sn, sgw, sgb_t)


def _t5_bucket_table():
    max_exact = REL_BUCKETS // 2
    d = np.arange(REL_MAX_DIST)
    df = np.maximum(d, 1).astype(np.float32)
    large = max_exact + (np.log(df / max_exact) / np.float32(math.log(REL_MAX_DIST / max_exact))
                         * (REL_BUCKETS - max_exact)).astype(np.int32)
    large = np.minimum(large, REL_BUCKETS - 1)
    return np.where(d < max_exact, d, large)


def _bucket_starts():
    bk = _t5_bucket_table()
    assert np.all(np.diff(bk) >= 0) and bk[-1] == REL_BUCKETS - 1
    return [(int(np.argmax(bk == v)), int(v)) for v in sorted(set(bk.tolist())) if v > 0]


def _attn_kernel(qmin_ref, kmax_ref, qmin_sub_ref, kmax_sub_ref, relb_ref,
                 qa_ref, pqi_ref, kiw_ref, kv_hbm, posq_ref, posk_ref,
                 o_ref, kv_buf, kv_sem, keys_ref, thr_ref, aux_ref, cnt_ref, m_ref, acc_ref, bias_ref,
                 *, bq, bk, n_keep, seq, far_dist, bucket_starts):
    b = pl.program_id(0)
    qi = pl.program_id(1)
    n_chunks = (qi * bq + bq - 1) // bk + 1
    nlc = bk // LANES

    def kv_copy(cc, slot):
        return pltpu.make_async_copy(kv_hbm.at[b, pl.ds(cc * bk, bk), :], kv_buf.at[slot], kv_sem.at[slot])

    for i in range(KV_BUFFERS - 1):
        @pl.when(i < n_chunks)
        def _():
            kv_copy(i, i).start()
    nsub_q, nsub_k = bq // LANES, bk // LANES
    k_f = float(n_keep)

    def tile_lanes(x, n=None):
        return jnp.concatenate([x] * (nlc if n is None else n), axis=1)

    def lane_chunk(x, j):
        return x[:, j * LANES:(j + 1) * LANES]

    def _select():
        qiv = pqi_ref[0, :, QI_Q:QI_Q + IDX_Q_W]
        zpad = jnp.zeros((bq, LANES - IDX_DIM), BF16)
        q_heads = [jnp.concatenate([qiv[:, h * IDX_DIM:(h + 1) * IDX_DIM], zpad], axis=1)
                   for h in range(IDX_HEADS)]
        kiw_q = pqi_ref[0, :, QI_KIW:QI_KIW + LANES]
        w = kiw_q[:, IDX_DIM:IDX_DIM + IDX_HEADS].astype(F32)
        w_bc = [jnp.broadcast_to(w[:, h:h + 1], (bq, bk)) for h in range(IDX_HEADS)]
        row_t = qi * bq + lax.broadcasted_iota(I32, (bq, bk), 0)
        lane_s = lax.broadcasted_iota(I32, (bq, bk), 1)

        def score_body(cc, carry):
            off = pl.multiple_of(cc * bk, bk)
            kc = kiw_ref[0, pl.ds(off, bk), :]
            sc = jnp.zeros((bq, bk), F32)
            for h in range(IDX_HEADS):
                sh = lax.dot_general(q_heads[h], kc, (((1,), (1,)), ((), ())), preferred_element_type=F32)
                sc = sc + w_bc[h] * jnp.maximum(sh, 0.0)
            sc = jnp.where(sc == 0.0, 0.0, sc)
            bits = lax.bitcast_convert_type(sc, I32)
            key = jnp.where(bits < 0, bits ^ INT_MAX, bits)
            keys_ref[cc] = jnp.where(cc * bk + lane_s <= row_t, key, INT_MIN)
            return carry

        lax.fori_loop(0, n_chunks, score_body, 0)

        def count_ge(thr):
            thr_ref[...] = thr
            cnt_ref[...] = jnp.zeros((bq, LANES), F32)

            def body(cc, carry):
                for r in range(bq // COUNT_ROWS):
                    rows = pl.ds(r * COUNT_ROWS, COUNT_ROWS)
                    t = keys_ref[cc, rows, :]
                    th = thr_ref[rows, :]
                    acc = cnt_ref[rows, :]
                    for j in range(nlc):
                        acc = acc + jnp.where(lane_chunk(t, j) >= th, 1.0, 0.0)
                    cnt_ref[rows, :] = acc
                return carry

            lax.fori_loop(0, n_chunks, body, 0)
            return jnp.broadcast_to(jnp.sum(cnt_ref[...], axis=1, keepdims=True), (bq, LANES))

        thr_ref[...] = jnp.full((bq, LANES), INT_MIN, I32)
        aux_ref[...] = jnp.full((bq, LANES), INT_MIN, I32)

        def top2(cc, carry):
            for r in range(bq // COUNT_ROWS):
                rows = pl.ds(r * COUNT_ROWS, COUNT_ROWS)
                t = keys_ref[cc, rows, :]
                m1 = thr_ref[rows, :]
                m2 = aux_ref[rows, :]
                for j in range(nlc):
                    x = lane_chunk(t, j)
                    m2 = jnp.maximum(m2, jnp.minimum(m1, x))
                    m1 = jnp.maximum(m1, x)
                thr_ref[rows, :] = m1
                aux_ref[rows, :] = m2
            return carry

        lax.fori_loop(0, n_chunks, top2, 0)

        def lane_all(x, op):
            shift = LANES // 2
            while shift >= 1:
                x = op(x, pltpu.roll(x, shift, 1))
                shift //= 2
            return x

        def key_value(k):
            return lax.bitcast_convert_type(jnp.where(k < 0, k ^ INT_MAX, k), F32)

        def value_key(v):
            bits = lax.bitcast_convert_type(v, I32)
            return jnp.where(bits < 0, bits ^ INT_MAX, bits)

        lo0 = lane_all(aux_ref[...], jnp.minimum)
        top = lane_all(thr_ref[...], jnp.maximum)
        hi0 = jnp.where(top == INT_MAX, INT_MAX, top + 1)
        c_lo0 = count_ge(lo0)
        log2_k = math.log2(k_f)

        def excess(cnt):
            return jnp.log2(jnp.maximum(cnt, 0.5)) - log2_k

        def resolved(lo, hi, c_lo):
            return jnp.logical_or(hi - 1 <= lo, c_lo == k_f)

        def probe(mid, lo, hi, c_lo, f_lo, f_hi, last):
            cnt = count_ge(mid)
            ok = cnt >= k_f
            f_new = excess(cnt)
            f_hi = jnp.where(jnp.logical_and(ok, last > 0.0), 0.5 * f_hi, f_hi)
            f_lo = jnp.where(jnp.logical_and(jnp.logical_not(ok), last < 0.0), 0.5 * f_lo, f_lo)
            return (jnp.where(ok, mid, lo), jnp.where(ok, hi, mid), jnp.where(ok, cnt, c_lo),
                    jnp.where(ok, f_new, f_lo), jnp.where(ok, f_hi, f_new), jnp.where(ok, 1.0, -1.0))

        def search_step(state):
            it, lo, hi, c_lo, f_lo, f_hi, last, _ = state
            done = resolved(lo, hi, c_lo)
            mid_b = (lo & hi) + ((lo ^ hi) >> 1)
            v_lo = key_value(lo)
            mid_i = jnp.clip(value_key(v_lo + f_lo / (f_lo - f_hi) * (key_value(hi) - v_lo)), lo + 1, hi - 1)
            mid = jnp.where(it % SEARCH_PERIOD == SEARCH_PERIOD - 1, mid_b, mid_i)
            mid = jnp.where(done, lo, mid)
            new = probe(mid, lo, hi, c_lo, f_lo, f_hi, last)
            new = tuple(jnp.where(done, old, upd) for old, upd in zip((lo, hi, c_lo, f_lo, f_hi, last), new))
            pending = jnp.max(jnp.where(resolved(new[0], new[1], new[2]), 0.0, 1.0))
            return (it + 1,) + new + (pending,)

        state = (lo0, hi0, c_lo0, excess(c_lo0), jnp.full((bq, LANES), excess(jnp.float32(0.0)), F32),
                 jnp.zeros((bq, LANES), F32))
        for z in (0, 1):
            lo, hi = state[0], state[1]
            zk = jnp.full((bq, LANES), z, I32)
            inside = jnp.logical_and(jnp.logical_and(zk > lo, zk < hi), jnp.logical_not(resolved(lo, hi, state[2])))
            new = probe(jnp.where(inside, zk, lo), *state)
            state = tuple(jnp.where(inside, upd, old) for old, upd in zip(state, new))
        pending0 = jnp.max(jnp.where(resolved(state[0], state[1], state[2]), 0.0, 1.0))
        state = lax.while_loop(lambda st: st[7] > 0.0, search_step, (jnp.int32(0),) + state + (pending0,))
        tau = state[1]
        c_gt = count_ge(tau + 1)
        need = jnp.where(tau == INT_MIN, 0.0, k_f - c_gt)
        upper = lax.broadcasted_iota(I32, (LANES, LANES), 0) <= lax.broadcasted_iota(I32, (LANES, LANES), 1)
        scan = jnp.concatenate([jnp.where(upper, 1.0, 0.0), jnp.ones((LANES, LANES), F32)], axis=1).astype(BF16)
        cnt_ref[...] = jnp.zeros((bq, LANES), F32)

        def build_mask(cc, carry):
            t = keys_ref[cc]
            cols = []
            seen = cnt_ref[...]
            for j in range(nlc):
                tj = lane_chunk(t, j)
                tie = tj == tau
                pt = jnp.dot(jnp.where(tie, 1.0, 0.0).astype(BF16), scan, preferred_element_type=F32)
                keep_tie = jnp.logical_and(tie, seen + pt[:, :LANES] <= need)
                cols.append(jnp.where(jnp.logical_or(tj > tau, keep_tie), 0.0, NEG_INF))
                seen = seen + pt[:, LANES:]
            cnt_ref[...] = seen
            keys_ref[cc] = lax.bitcast_convert_type(jnp.concatenate(cols, axis=1), I32)
            return carry

        lax.fori_loop(0, n_chunks, build_mask, 0)

        m_ref[...] = jnp.full(m_ref.shape, NEG_INF, F32)
        acc_ref[...] = jnp.zeros_like(acc_ref)

    _select()

    c1 = (ATT_HEAD_DIM ** -0.5) * LOG2E
    q = qa_ref[0]
    ones_v = jnp.ones((bk, ATT_HEAD_DIM), BF16)

    def heads(near, maskf, kk, vv):
        for h in range(ATT_HEADS):
            hs = slice(h * ATT_HEAD_DIM, (h + 1) * ATT_HEAD_DIM)
            s = lax.dot_general(q[:, hs], kk[:, hs], (((1,), (1,)), ((), ())), preferred_element_type=F32)
            t = s * c1 + maskf
            if near:
                t = t + bias_ref[h]
            m_old = m_ref[h]
            m_new = jnp.maximum(m_old, jnp.max(t, axis=1, keepdims=True))
            m_safe = jnp.where(m_new == NEG_INF, 0.0, m_new)
            alpha = jnp.exp2(m_old - m_safe)
            p = jnp.exp2(t - tile_lanes(m_safe)).astype(BF16)
            v_aug = jnp.concatenate([vv[:, hs], ones_v], axis=1)
            acc_ref[h] = tile_lanes(alpha, 2) * acc_ref[h] + jnp.dot(p, v_aug, preferred_element_type=F32)
            m_ref[h] = m_new

    def attend(c, carry):
        slot = c % KV_BUFFERS
        nxt = c + (KV_BUFFERS - 1)

        @pl.when(nxt < n_chunks)
        def _():
            kv_copy(nxt, nxt % KV_BUFFERS).start()

        kv_copy(c, slot).wait()
        maskf = lax.bitcast_convert_type(keys_ref[c], F32)
        kk = kv_buf[slot, :, KV_K:KV_K + ATT_W]
        vv = kv_buf[slot, :, KV_V:KV_V + ATT_W]
        far = qmin_ref[b * (seq // bq) + qi] - kmax_ref[b * (seq // bk) + c] >= far_dist

        @pl.when(far)
        def _():
            heads(False, maskf, kk, vv)

        @pl.when(jnp.logical_not(far))
        def _():
            nsub = seq // LANES
            for r in range(nsub_q):
                for j in range(nsub_k):
                    rows = slice(r * LANES, (r + 1) * LANES)
                    cols = slice(j * LANES, (j + 1) * LANES)
                    sub_far = (qmin_sub_ref[b * nsub + qi * nsub_q + r]
                               - kmax_sub_ref[b * nsub + c * nsub_k + j]) >= far_dist

                    @pl.when(sub_far)
                    def _():
                        for h in range(ATT_HEADS):
                            bias_ref[h, rows, cols] = jnp.zeros((LANES, LANES), F32)

                    @pl.when(jnp.logical_not(sub_far))
                    def _():
                        half = LANES // 2
                        for hr in range(2):
                            rr = slice(r * LANES + hr * half, r * LANES + (hr + 1) * half)
                            dist = posq_ref[0, rr, :] - posk_ref[0, c, :, cols]
                            bias = [jnp.full((half, LANES), relb_ref[h], F32) for h in range(ATT_HEADS)]
                            for start, bucket in bucket_starts:
                                ge = dist >= start
                                for h in range(ATT_HEADS):
                                    bias[h] = jnp.where(ge, relb_ref[bucket * ATT_HEADS + h], bias[h])
                            for h in range(ATT_HEADS):
                                bias_ref[h, rr, cols] = bias[h]

            heads(True, maskf, kk, vv)

        return carry

    lax.fori_loop(0, n_chunks, attend, 0)

    outs = []
    for h in range(ATT_HEADS):
        a = acc_ref[h]
        outs.append(a[:, :ATT_HEAD_DIM] / a[:, ATT_HEAD_DIM:])
    o_ref[0] = jnp.concatenate(outs, axis=1).astype(BF16)


def _attn(p_a13, p_qi3, p_kv3, positions, rel_table, *, bq, bk):
    b, s, _ = p_qi3.shape
    assert bq % LANES == 0 and bk % LANES == 0
    n_keep = min(TOPK_MAX, s // 4)
    assert n_keep <= 2 * LANES
    nq, nkc, nsub = s // bq, s // bk, s // LANES
    qmin =jnp.min(positions.reshape(b, nq, bq), axis=-1).reshape(-1)
    kmax = jnp.max(positions.reshape(b, nkc, bk), axis=-1).reshape(-1)
    qmin_sub = jnp.min(positions.reshape(b, nsub, LANES), axis=-1).reshape(-1)
    kmax_sub = jnp.max(positions.reshape(b, nsub, LANES), axis=-1).reshape(-1)
    starts = _bucket_starts()
    far_dist = REL_MAX_DIST - 1
    assert starts[-1][0] <= far_dist and starts[-1][1] == REL_BUCKETS - 1
    relb = ((rel_table - rel_table[REL_BUCKETS - 1:REL_BUCKETS, :]) * LOG2E).astype(F32).reshape(-1)

    grid_spec = pltpu.PrefetchScalarGridSpec(
        num_scalar_prefetch=5,
        grid=(b, nq),
        in_specs=[
            pl.BlockSpec((1, bq, ATT_W), lambda i, t, *_: (i, t, A1_QA // ATT_W)),
            pl.BlockSpec((1, bq, QI_W), lambda i, t, *_: (i, t, 0)),
            pl.BlockSpec((1, s, LANES), lambda i, t, *_: (i, 0, QI_KIW // LANES)),
            pl.BlockSpec(memory_space=pl.ANY),
            pl.BlockSpec((1, bq, 1), lambda i, t, *_: (i, t, 0)),
            pl.BlockSpec((1, nkc, 1, bk), lambda i, t, *_: (i, 0, 0, 0)),
        ],
        out_specs=pl.BlockSpec((1, bq, ATT_W), lambda i, t, *_: (i, t, 0)),
        scratch_shapes=[
            pltpu.VMEM((KV_BUFFERS, bk, KV_W), BF16),
            pltpu.SemaphoreType.DMA((KV_BUFFERS,)),
            pltpu.VMEM((nkc, bq, bk), I32),
            pltpu.VMEM((bq, LANES), I32),
            pltpu.VMEM((bq, LANES), I32),
            pltpu.VMEM((bq, LANES), F32),
            pltpu.VMEM((ATT_HEADS, bq, LANES), F32),
            pltpu.VMEM((ATT_HEADS, bq, 2 * ATT_HEAD_DIM), F32),
            pltpu.VMEM((ATT_HEADS, bq, bk), F32),
        ],
    )
    kern = functools.partial(_attn_kernel, bq=bq, bk=bk, n_keep=n_keep, seq=s,
                             far_dist=far_dist, bucket_starts=starts)
    return pl.pallas_call(
        kern,
        out_shape=jax.ShapeDtypeStruct((b, s, ATT_W), BF16),
        grid_spec=grid_spec,
        compiler_params=_cparams(("parallel", "arbitrary")),
        name="sparse_attn",
    )(qmin, kmax, qmin_sub, kmax_sub, relb,
      p_a13, p_qi3, p_qi3, p_kv3, positions.reshape(b, s, 1), positions.reshape(b, nkc, 1, bk))


def _merge_kernel(or_ref, oa_ref, os_ref, gr_ref, ga_ref, gs_ref, wr_ref, wa_ref, ws_ref, wo_ref, x_ref, o_ref):
    j = pl.program_id(1)

    @pl.when(j == 0)
    def _():
        o_ref[...] = jnp.zeros_like(o_ref)

    def branch(o, w, g):
        return jax.nn.sigmoid(g[0].astype(F32)) * jnp.dot(o[...], w[...], preferred_element_type=F32)

    merged = branch(or_ref, wr_ref, gr_ref) + branch(oa_ref, wa_ref, ga_ref) + branch(os_ref, ws_ref, gs_ref)
    o_ref[...] += jnp.dot(merged.astype(BF16), wo_ref[...], preferred_element_type=F32)

    @pl.when(j == pl.num_programs(1) - 1)
    def _():
        o_ref[...] = x_ref[...] + o_ref[...]


def _merge(o_r, o_a, o_s, p_g, wr, wa, ws, wo, x2, *, tm):
    t, d = x2.shape
    tn = PROJ_TILE
    per_gate = d // tn

    def gspec(g):
        return pl.BlockSpec((1, tm, tn), lambda i, j: (g * per_gate + j, i, 0))

    return pl.pallas_call(
        _merge_kernel,
        out_shape=jax.ShapeDtypeStruct((t, d), F32),
        grid=(t // tm, per_gate),
        in_specs=[
            pl.BlockSpec((tm, RET_V_W), lambda i, j: (i, 0)),
            pl.BlockSpec((tm, ATT_W), lambda i, j: (i, 0)),
            pl.BlockSpec((tm, SG_W), lambda i, j: (i, 0)),
            gspec(0), gspec(1), gspec(2),
            pl.BlockSpec((RET_V_W, tn), lambda i, j: (0, j)),
            pl.BlockSpec((ATT_W, tn), lambda i, j: (0, j)),
            pl.BlockSpec((SG_W, tn), lambda i, j: (0, j)),
            pl.BlockSpec((tn, d), lambda i, j: (j, 0)),
            pl.BlockSpec((tm, d), lambda i, j: (i, 0)),
        ],
        out_specs=pl.BlockSpec((tm, d), lambda i, j: (i, 0)),
        compiler_params=_cparams(("parallel", "arbitrary")),
        name="merge",
    )(o_r, o_a, o_s, p_g, p_g, p_g, wr, wa, ws, wo, x2)


def _split_w_in_kernel(w_ref, wa_ref, wb_ref):
    w = w_ref[0]
    wa_ref[:, :W_A_COLS] = w[:, :W_A_COLS].astype(BF16)
    wa_ref[:, W_A_COLS:] = jnp.zeros((w.shape[0], W_A_PAD - W_A_COLS), BF16)
    wb_ref[...] = w[:, W_A_COLS:].astype(BF16)


def _split_w_in(w_in, layer, d_model, *, tr=64):
    _, rows, cols = w_in.shape
    assert cols == W_A_COLS + 2 * SG_W + 3 * d_model and A1_W + KV_W + QI_W == W_A_PAD
    w_b_cols = cols - W_A_COLS
    assert w_b_cols % PROJ_TILE == 0 and rows % tr == 0
    return pl.pallas_call(
        _split_w_in_kernel,
        out_shape=(jax.ShapeDtypeStruct((rows, W_A_PAD), BF16), jax.ShapeDtypeStruct((rows, w_b_cols), BF16)),
        grid=(rows // tr,),
        in_specs=[pl.BlockSpec((1, tr, cols), lambda i: (layer, i, 0))],
        out_specs=(pl.BlockSpec((tr, W_A_PAD), lambda i: (i, 0)), pl.BlockSpec((tr, w_b_cols), lambda i: (i, 0))),
        compiler_params=_cparams(("parallel",)),
        name="split_w_in",
    )(w_in)


def _pick(n, prefs):
    for p in prefs:
        if n % p == 0:
            return p
    return n


def kernel(x, positions, rel_table, ffn1_norm, ffn1_w_gate, ffn1_w_up, ffn1_w_down, mix_norm, w_in, ret_norm, sg_norm, sg_w, sg_b, w_br_ret, w_br_att, w_br_sg, w_out, ffn2_norm, ffn2_w_gate, ffn2_w_up, ffn2_w_down, final_norm):
    b, s, d = x.shape
    t = b * s
    depth = w_in.shape[0]
    d_ff = ffn1_w_gate.shape[2]
    assert s % RET_CHUNK == 0 and d % PROJ_TILE == 0

    tm = _pick(t, (512, 256, 128))
    tm_proj = _pick(t, (1024, 512, 256, 128))
    tf = _pick(d_ff, (512, 256, 128))
    bq = _pick(s, (256, 128))
    bk = _pick(s, (512, 256, 128))

    cos_t, sin_t = _rope_tables(positions, tm=_pick(s, (2048, 1024, 512, 256, 128)))
    consts = _retention_constants()
    x2 = x.reshape(t, d)
    for l in range(depth):
        x2 = _ffn(x2, ffn1_norm[l], ffn1_w_gate[l].astype(BF16), ffn1_w_up[l].astype(BF16),
                  ffn1_w_down[l].astype(BF16), tm=tm, tf=tf)
        p_a1, p_kv, p_qi, p_sg, p_g = _proj(x2, mix_norm[l], *_split_w_in(w_in, l, d), tm=tm_proj)
        p_a13 = p_a1.reshape(b, s, -1)
        o_r, o_s = _local(p_a13, p_sg.reshape(b, s, -1), cos_t, sin_t, ret_norm[l], sg_norm[l], sg_w[l], sg_b[l],
                          consts)
        o_a = _attn(p_a13, p_qi.reshape(b, s, -1), p_kv.reshape(b, s, -1), positions, rel_table, bq=bq, bk=bk)
        x2 = _merge(o_r.reshape(t, -1), o_a.reshape(t, -1), o_s.reshape(t, -1), p_g,
                    w_br_ret[l].astype(BF16), w_br_att[l].astype(BF16), w_br_sg[l].astype(BF16),
                    w_out[l].astype(BF16), x2, tm=tm)
        x2 = _ffn(x2, ffn2_norm[l], ffn2_w_gate[l].astype(BF16), ffn2_w_up[l].astype(BF16),
                  ffn2_w_down[l].astype(BF16), final_norm if l == depth - 1 else None, tm=tm, tf=tf)
    return x2.reshape(b, s, d)
```

```python
import functools
import math

import numpy as np
import jax
import jax.numpy as jnp
from jax import lax
from jax.experimental import pallas as pl
from jax.experimental.pallas import tpu as pltpu

F32 = jnp.float32
BF16 = jnp.bfloat16
I32 = jnp.int32

RET_HEADS, RET_QK_DIM, RET_V_DIM, RET_CHUNK = 6, 64, 128, 128
ATT_HEADS, ATT_HEAD_DIM = 6, 128
IDX_HEADS, IDX_DIM = 4, 64
TOPK_MAX = 256
SG_GROUPS, SG_GROUP_DIM, SG_CHUNK = 4, 128, 128
REL_BUCKETS, REL_MAX_DIST = 32, 128
ROPE_BASE = 10000.0
EPS = 1e-6

RET_QK_W = RET_HEADS * RET_QK_DIM
RET_V_W = RET_HEADS * RET_V_DIM
ATT_W = ATT_HEADS * ATT_HEAD_DIM
IDX_Q_W = IDX_HEADS * IDX_DIM
SG_W = SG_GROUPS * SG_GROUP_DIM

LANES = 128
COUNT_ROWS = 64
KV_BUFFERS = 3
SEARCH_PERIOD = 6
VMEM_LIMIT_BYTES = 56 * 1024 * 1024

PROJ_TILE = 512
W_A_COLS = 2 * RET_QK_W + 2 * RET_V_W + 3 * ATT_W + IDX_Q_W + IDX_DIM + IDX_HEADS
W_A_PAD = 5120
A1_QR, A1_KR, A1_VR, A1_GR, A1_QA = 0, 384, 768, 1536, 2304
A1_W = 3072
KV_K, KV_V = 0, 768
KV_W = 1536
QI_Q, QI_KIW = 0, 256
QI_W = 512
SG_U, SG_V = 0, 512
SG_PW = 1024

INT_MIN = np.int32(-2 ** 31)
INT_MAX = np.int32(2 ** 31 - 1)
NEG_INF = float("-inf")
LOG2E = math.log2(math.e)


def _cparams(sem):
    return pltpu.CompilerParams(dimension_semantics=sem, vmem_limit_bytes=VMEM_LIMIT_BYTES)


def _rms(x, g):
    return x * lax.rsqrt(jnp.mean(x * x, axis=-1, keepdims=True) + EPS) * g


def _ffn_kernel(x_ref, g_ref, wg_ref, wu_ref, wd_ref, *rest, final_norm):
    if final_norm:
        fg_ref, o_ref, xn_ref = rest
    else:
        o_ref, xn_ref = rest
    j = pl.program_id(1)

    @pl.when(j == 0)
    def _():
        xn_ref[...] = _rms(x_ref[...], g_ref[...]).astype(BF16)
        o_ref[...] = jnp.zeros_like(o_ref)

    xn = xn_ref[...]
    a = jnp.dot(xn, wg_ref[...], preferred_element_type=F32)
    b = jnp.dot(xn, wu_ref[...], preferred_element_type=F32)
    h = (a * jax.nn.sigmoid(a) * b).astype(BF16)
    o_ref[...] += jnp.dot(h, wd_ref[...], preferred_element_type=F32)

    @pl.when(j == pl.num_programs(1) - 1)
    def _():
        y = x_ref[...] + 0.5 * o_ref[...]
        if final_norm:
            y = _rms(y, fg_ref[...])
        o_ref[...] = y


def _ffn(x2, gain, wg, wu, wd, final_gain=None, *, tm, tf):
    t, d = x2.shape
    f = wg.shape[1]
    final_norm = final_gain is not None
    in_specs = [
        pl.BlockSpec((tm, d), lambda i, j: (i, 0), pipeline_mode=pl.Buffered(1)),
        pl.BlockSpec((1, d), lambda i, j: (0, 0)),
        pl.BlockSpec((d, tf), lambda i, j: (0, j)),
        pl.BlockSpec((d, tf), lambda i, j: (0, j)),
        pl.BlockSpec((tf, d), lambda i, j: (j, 0)),
    ]
    args = [x2, gain.reshape(1, d), wg, wu, wd]
    if final_norm:
        in_specs.append(pl.BlockSpec((1, d), lambda i, j: (0, 0)))
        args.append(final_gain.reshape(1, d))
    return pl.pallas_call(
        functools.partial(_ffn_kernel, final_norm=final_norm),
        out_shape=jax.ShapeDtypeStruct((t, d), F32),
        grid=(t // tm, f // tf),
        in_specs=in_specs,
        out_specs=pl.BlockSpec((tm, d), lambda i, j: (i, 0)),
        scratch_shapes=[pltpu.VMEM((tm, d), BF16)],
        compiler_params=_cparams(("parallel", "arbitrary")),
        name="ffn",
    )(*args)


def _proj_kernel(x_ref, g_ref, wa_ref, wb_ref, oa1_ref, okv_ref, oqi_ref, osg_ref, og_ref, xn_ref, *, ends):
    j = pl.program_id(1)
    e_a1, e_kv, e_qi, e_sg = ends

    @pl.when(j == 0)
    def _():
        xn_ref[...] = _rms(x_ref[...], g_ref[...]).astype(BF16)

    def tile(w_ref):
        return jnp.dot(xn_ref[...], w_ref[...], preferred_element_type=F32).astype(BF16)

    @pl.when(j < e_a1)
    def _():
        oa1_ref[...] = tile(wa_ref)

    @pl.when(jnp.logical_and(j >= e_a1, j < e_kv))
    def _():
        okv_ref[...] = tile(wa_ref)

    @pl.when(jnp.logical_and(j >= e_kv, j < e_qi))
    def _():
        oqi_ref[...] = tile(wa_ref)

    @pl.when(jnp.logical_and(j >= e_qi, j < e_sg))
    def _():
        osg_ref[...] = tile(wb_ref)

    @pl.when(j >= e_sg)
    def _():
        og_ref[0] = tile(wb_ref)


def _proj(x2, gain, w_a, w_b, *, tm):
    t, d = x2.shape
    tn = PROJ_TILE
    n_a, n_b = w_a.shape[1] // tn, w_b.shape[1] // tn
    e_a1 = A1_W // tn
    e_kv = e_a1 + KV_W // tn
    e_qi = e_kv + QI_W // tn
    e_sg = e_qi + SG_PW // tn
    assert e_qi == n_a
    t_g = n_a + n_b - e_sg

    def clampspec(first, count):
        return pl.BlockSpec((tm, tn), lambda i, j: (i, jnp.clip(j - first, 0, count - 1)))

    return pl.pallas_call(
        functools.partial(_proj_kernel, ends=(e_a1, e_kv, e_qi, e_sg)),
        out_shape=(jax.ShapeDtypeStruct((t, A1_W), BF16),
                   jax.ShapeDtypeStruct((t, KV_W), BF16),
                   jax.ShapeDtypeStruct((t, QI_W), BF16),
                   jax.ShapeDtypeStruct((t, SG_PW), BF16),
                   jax.ShapeDtypeStruct((t_g, t, tn), BF16)),
        grid=(t // tm, n_a + n_b),
        in_specs=[
            pl.BlockSpec((tm, d), lambda i, j: (i, 0)),
            pl.BlockSpec((1, d), lambda i, j: (0, 0)),
            pl.BlockSpec((d, tn), lambda i, j: (0, jnp.clip(j, 0, n_a - 1))),
            pl.BlockSpec((d, tn), lambda i, j: (0, jnp.clip(j - n_a, 0, n_b - 1))),
        ],
        out_specs=(clampspec(0, e_a1), clampspec(e_a1, e_kv - e_a1), clampspec(e_kv, e_qi - e_kv),
                   clampspec(e_qi, e_sg - e_qi),
                   pl.BlockSpec((1, tm, tn), lambda i, j: (jnp.clip(j - e_sg, 0, t_g - 1), i, 0))),
        scratch_shapes=[pltpu.VMEM((tm, d), BF16)],
        compiler_params=_cparams(("arbitrary", "arbitrary")),
        name="proj",
    )(x2, gain.reshape(1, d), w_a, w_b)


def _rope_table_kernel(pos_ref, invf_ref, sign_ref, c_ref, s_ref):
    ang = pos_ref[0].astype(F32) * invf_ref[...]
    c_ref[0] = jnp.cos(ang)
    s_ref[0] = jnp.sin(ang) * sign_ref[...]


def _rope_tables(positions, *, tm):
    b, s = positions.shape
    half = RET_QK_DIM // 2
    inv_freq = ROPE_BASE ** (-jnp.arange(0, RET_QK_DIM, 2, dtype=F32) / RET_QK_DIM)
    lane = np.arange(LANES)
    invf = inv_freq[lane % half].reshape(1, LANES)
    sign = jnp.asarray(np.where(lane % RET_QK_DIM < half, -1.0, 1.0), F32).reshape(1, LANES)
    out = jax.ShapeDtypeStruct((b, s, LANES), F32)
    return pl.pallas_call(
        _rope_table_kernel,
        out_shape=(out, out),
        grid=(b, s // tm),
        in_specs=[
            pl.BlockSpec((1, tm, 1), lambda i, j: (i, j, 0)),
            pl.BlockSpec((1, LANES), lambda i, j: (0, 0)),
            pl.BlockSpec((1, LANES), lambda i, j: (0, 0)),
        ],
        out_specs=(pl.BlockSpec((1, tm, LANES), lambda i, j: (i, j, 0)),
                   pl.BlockSpec((1, tm, LANES), lambda i, j: (i, j, 0))),
        compiler_params=_cparams(("parallel", "parallel")),
        name="rope_tables",
    )(positions.reshape(b, s, 1), invf, sign)


def _retention_constants():
    c = RET_CHUNK
    h = np.arange(RET_HEADS, dtype=np.float64)
    log_g = np.log(1.0 - 2.0 ** (-5.0 - h))
    idx = np.arange(c, dtype=np.float64)
    rel = idx[:, None] - idx[None, :]
    dmat = np.where(rel >= 0, np.exp(np.maximum(rel, 0.0) * log_g[:, None, None]), 0.0)
    k_decay = np.exp((c - 1 - idx)[None, :] * log_g[:, None])
    q_decay = np.exp((idx + 1)[None, :] * log_g[:, None])
    chunk_decay = np.exp(c * log_g)
    kd = np.repeat(k_decay.T, RET_QK_DIM, axis=1)
    qd = np.repeat(q_decay.T, RET_QK_DIM, axis=1) * RET_QK_DIM ** -0.5
    lane = np.arange(RET_QK_W)
    src = (lane // RET_QK_DIM) * RET_QK_DIM + (lane % RET_QK_DIM + RET_QK_DIM // 2) % RET_QK_DIM
    perm = np.zeros((RET_QK_W, RET_QK_W), np.float32)
    perm[src, lane] = 1.0
    return (jnp.asarray(dmat, F32), jnp.asarray(qd, F32), jnp.asarray(kd, F32),
            [float(np.float32(v)) for v in chunk_decay], jnp.asarray(perm, BF16))


def _gelu_tanh(x):
    return 0.5 * x * (1.0 + jnp.tanh(math.sqrt(2.0 / math.pi) * (x + 0.044715 * (x * x * x))))


def _local_kernel(p_ref, psg_ref, cos_ref, sin_ref,
                  perm_ref, dmat_ref, qd_ref, kd_ref, rn_ref, sn_ref, sgw_ref, sgb_ref,
                  or_ref, os_ref, state_ref, *, chunk_decay):
    @pl.when(pl.program_id(1) == 0)
    def _():
        state_ref[...] = jnp.zeros_like(state_ref)

    n_rep = RET_QK_W // LANES
    cos3 = jnp.concatenate([cos_ref[0]] * n_rep, axis=1)
    sin3 = jnp.concatenate([sin_ref[0]] * n_rep, axis=1)
    q = p_ref[0, :, A1_QR:A1_QR + RET_QK_W]
    k = p_ref[0, :, A1_KR:A1_KR + RET_QK_W]
    perm = perm_ref[...]
    qr = q.astype(F32) * cos3 + jnp.dot(q, perm, preferred_element_type=F32) * sin3
    kr = k.astype(F32) * cos3 + jnp.dot(k, perm, preferred_element_type=F32) * sin3
    q_in = (qr * (RET_QK_DIM ** -0.5)).astype(BF16)
    q_cr = (qr * qd_ref[...]).astype(BF16)
    k_b = kr.astype(BF16)
    k_dec_t = jnp.transpose(kr * kd_ref[...]).astype(BF16)
    v = p_ref[0, :, A1_VR:A1_VR + RET_V_W]
    outs = []
    for h in range(RET_HEADS):
        qs = slice(h * RET_QK_DIM, (h + 1) * RET_QK_DIM)
        vh = v[:, h * RET_V_DIM:(h + 1) * RET_V_DIM]
        a = lax.dot_general(q_in[:, qs], k_b[:, qs], (((1,), (1,)), ((), ())),
                            preferred_element_type=F32) * dmat_ref[h]
        prev = state_ref[h]
        o = (jnp.dot(a.astype(BF16), vh, preferred_element_type=F32)
             + jnp.dot(q_cr[:, qs], prev.astype(BF16), preferred_element_type=F32))
        state_ref[h] = prev * chunk_decay[h] + jnp.dot(k_dec_t[qs, :], vh, preferred_element_type=F32)
        mu = jnp.mean(o, axis=-1, keepdims=True)
        d = o - mu
        var = jnp.mean(d * d, axis=-1, keepdims=True)
        outs.append(d * lax.rsqrt(var + EPS))
    g = p_ref[0, :, A1_GR:A1_GR + RET_V_W].astype(F32)
    y = jnp.concatenate(outs, axis=1) * rn_ref[...]
    or_ref[0] = (y * (g * jax.nn.sigmoid(g))).astype(BF16)

    u = _gelu_tanh(psg_ref[0, :, SG_U:SG_U + SG_W].astype(F32))
    vn = _rms(_gelu_tanh(psg_ref[0, :, SG_V:SG_V + SG_W].astype(F32)), sn_ref[...])
    vn_b = vn.astype(BF16)
    c = SG_CHUNK
    tril = lax.broadcasted_iota(I32, (c, c), 0) >= lax.broadcasted_iota(I32, (c, c), 1)
    bias = sgb_ref[...]
    mixed = []
    for gi in range(SG_GROUPS):
        w = jnp.where(tril, sgw_ref[gi], jnp.zeros((), BF16))
        m = jnp.dot(w, vn_b[:, gi * SG_GROUP_DIM:(gi + 1) * SG_GROUP_DIM], preferred_element_type=F32)
        mixed.append(m + bias[:, gi:gi + 1])
    os_ref[0] = (u * jnp.concatenate(mixed, axis=1)).astype(BF16)


def _local(p_a13, p_sg3, cos_t, sin_t, ret_norm, sg_norm, sg_w, sg_b, consts):
    b, s, _ = p_a13.shape
    c = RET_CHUNK
    dmat, qd, kd, chunk_decay, perm = consts

    def whole(arr):
        nd = arr.ndim
        return pl.BlockSpec(arr.shape, lambda i, j: (0,) * nd)

    rn = ret_norm.reshape(1, RET_V_W)
    sn = sg_norm.reshape(1, SG_W)
    sgw = sg_w.astype(BF16)
    sgb_t = jnp.transpose(sg_b)
    tab = pl.BlockSpec((1, c, LANES), lambda i, j: (i, j, 0))
    return pl.pallas_call(
        functools.partial(_local_kernel, chunk_decay=chunk_decay),
        out_shape=(jax.ShapeDtypeStruct((b, s, RET_V_W), BF16),
                   jax.ShapeDtypeStruct((b, s, SG_W), BF16)),
        grid=(b, s // c),
        in_specs=[
            pl.BlockSpec((1, c, A1_W), lambda i, j: (i, j, 0)),
            pl.BlockSpec((1, c, SG_PW), lambda i, j: (i, j, 0)),
            tab, tab,
            whole(perm), whole(dmat), whole(qd), whole(kd), whole(rn), whole(sn), whole(sgw), whole(sgb_t),
        ],
        out_specs=(pl.BlockSpec((1, c, RET_V_W), lambda i, j: (i, j, 0)),
                   pl.BlockSpec((1, c, SG_W), lambda i, j: (i, j, 0))),
        scratch_shapes=[pltpu.VMEM((RET_HEADS, RET_QK_DIM, RET_V_DIM), F32)],
        compiler_params=_cparams(("parallel", "arbitrary")),
        name="local_mixers",
    )(p_a13, p_sg3, cos_t, sin_t, perm, dmat, qd, kd, rn, sn, sgw, sgb_t)


def _t5_bucket_table():
    max_exact = REL_BUCKETS // 2
    d = np.arange(REL_MAX_DIST)
    df = np.maximum(d, 1).astype(np.float32)
    large = max_exact + (np.log(df / max_exact) / np.float32(math.log(REL_MAX_DIST / max_exact))
                         * (REL_BUCKETS - max_exact)).astype(np.int32)
    large = np.minimum(large, REL_BUCKETS - 1)
    return np.where(d < max_exact, d, large)


def _bucket_starts():
    bk = _t5_bucket_table()
    assert np.all(np.diff(bk) >= 0) and bk[-1] == REL_BUCKETS - 1
    return [(int(np.argmax(bk == v)), int(v)) for v in sorted(set(bk.tolist())) if v > 0]


def _attn_kernel(qmin_ref, kmax_ref, qmin_sub_ref, kmax_sub_ref, relb_ref,
                 qa_ref, pqi_ref, kiw_ref, kv_hbm, posq_ref, posk_ref,
                 o_ref, kv_buf, kv_sem, keys_ref, thr_ref, aux_ref, cnt_ref, m_ref, acc_ref, bias_ref,
                 *, bq, bk, n_keep, seq, far_dist, bucket_starts):
    b = pl.program_id(0)
    qi = pl.program_id(1)
    n_chunks = (qi * bq + bq - 1) // bk + 1
    nlc = bk // LANES

    def kv_copy(cc, slot):
        return pltpu.make_async_copy(kv_hbm.at[b, pl.ds(cc * bk, bk), :], kv_buf.at[slot], kv_sem.at[slot])

    for i in range(KV_BUFFERS - 1):
        @pl.when(i < n_chunks)
        def _():
            kv_copy(i, i).start()
    nsub_q, nsub_k = bq // LANES, bk // LANES
    k_f = float(n_keep)

    def tile_lanes(x, n=None):
        return jnp.concatenate([x] * (nlc if n is None else n), axis=1)

    def lane_chunk(x, j):
        return x[:, j * LANES:(j + 1) * LANES]

    def _select():
        qiv = pqi_ref[0, :, QI_Q:QI_Q + IDX_Q_W]
        zpad = jnp.zeros((bq, LANES - IDX_DIM), BF16)
        q_heads = [jnp.concatenate([qiv[:, h * IDX_DIM:(h + 1) * IDX_DIM], zpad], axis=1)
                   for h in range(IDX_HEADS)]
        kiw_q = pqi_ref[0, :, QI_KIW:QI_KIW + LANES]
        w = kiw_q[:, IDX_DIM:IDX_DIM + IDX_HEADS].astype(F32)
        w_bc = [jnp.broadcast_to(w[:, h:h + 1], (bq, bk)) for h in range(IDX_HEADS)]
        row_t = qi * bq + lax.broadcasted_iota(I32, (bq, bk), 0)
        lane_s = lax.broadcasted_iota(I32, (bq, bk), 1)

        def score_body(cc, carry):
            off = pl.multiple_of(cc * bk, bk)
            kc = kiw_ref[0, pl.ds(off, bk), :]
            sc = jnp.zeros((bq, bk), F32)
            for h in range(IDX_HEADS):
                sh = lax.dot_general(q_heads[h], kc, (((1,), (1,)), ((), ())), preferred_element_type=F32)
                sc = sc + w_bc[h] * jnp.maximum(sh, 0.0)
            sc = jnp.where(sc == 0.0, 0.0, sc)
            bits = lax.bitcast_convert_type(sc, I32)
            key = jnp.where(bits < 0, bits ^ INT_MAX, bits)
            key = jnp.where(cc * bk + lane_s <= row_t, key, INT_MIN)
            keys_ref[cc] = key
            for r in range(bq // COUNT_ROWS):
                rows = slice(r * COUNT_ROWS, (r + 1) * COUNT_ROWS)
                m1 = thr_ref[rows, :]
                m2 = aux_ref[rows, :]
                for j in range(nlc):
                    x = key[rows, j * LANES:(j + 1) * LANES]
                    m2 = jnp.maximum(m2, jnp.minimum(m1, x))
                    m1 = jnp.maximum(m1, x)
                thr_ref[rows, :] = m1
                aux_ref[rows, :] = m2
            return carry

        thr_ref[...] = jnp.full((bq, LANES), INT_MIN, I32)
        aux_ref[...] = jnp.full((bq, LANES), INT_MIN, I32)
        lax.fori_loop(0, n_chunks, score_body, 0)
        top1, top2 = thr_ref[...], aux_ref[...]

        def count_ge(thr):
            thr_ref[...] = thr
            cnt_ref[...] = jnp.zeros((bq, LANES), F32)

            def body(cc, carry):
                for r in range(bq // COUNT_ROWS):
                    rows = pl.ds(r * COUNT_ROWS, COUNT_ROWS)
                    t = keys_ref[cc, rows, :]
                    th = thr_ref[rows, :]
                    acc = cnt_ref[rows, :]
                    for j in range(nlc):
                        acc = acc + jnp.where(lane_chunk(t, j) >= th, 1.0, 0.0)
                    cnt_ref[rows, :] = acc
                return carry

            lax.fori_loop(0, n_chunks, body, 0)
            return jnp.broadcast_to(jnp.sum(cnt_ref[...], axis=1, keepdims=True), (bq, LANES))


        def lane_all(x, op):
            shift = LANES // 2
            while shift >= 1:
                x = op(x, pltpu.roll(x, shift, 1))
                shift //= 2
            return x

        def key_value(k):
            return lax.bitcast_convert_type(jnp.where(k < 0, k ^ INT_MAX, k), F32)

        def value_key(v):
            bits = lax.bitcast_convert_type(v, I32)
            return jnp.where(bits < 0, bits ^ INT_MAX, bits)

        lo0 = lane_all(top2, jnp.minimum)
        top = lane_all(top1, jnp.maximum)
        hi0 = jnp.where(top == INT_MAX, INT_MAX, top + 1)
        c_lo0 = count_ge(lo0)
        log2_k = math.log2(k_f)

        def excess(cnt):
            return jnp.log2(jnp.maximum(cnt, 0.5)) - log2_k

        def resolved(lo, hi, c_lo):
            return jnp.logical_or(hi - 1 <= lo, c_lo == k_f)

        def probe(mid, lo, hi, c_lo, f_lo, f_hi, last):
            cnt = count_ge(mid)
            ok = cnt >= k_f
            f_new = excess(cnt)
            f_hi = jnp.where(jnp.logical_and(ok, last > 0.0), 0.5 * f_hi, f_hi)
            f_lo = jnp.where(jnp.logical_and(jnp.logical_not(ok), last < 0.0), 0.5 * f_lo, f_lo)
            return (jnp.where(ok, mid, lo), jnp.where(ok, hi, mid), jnp.where(ok, cnt, c_lo),
                    jnp.where(ok, f_new, f_lo), jnp.where(ok, f_hi, f_new), jnp.where(ok, 1.0, -1.0))

        def search_step(state):
            it, lo, hi, c_lo, f_lo, f_hi, last, _ = state
            done = resolved(lo, hi, c_lo)
            mid_b = (lo & hi) + ((lo ^ hi) >> 1)
            v_lo = key_value(lo)
            mid_i = jnp.clip(value_key(v_lo + f_lo / (f_lo - f_hi) * (key_value(hi) - v_lo)), lo + 1, hi - 1)
            mid = jnp.where(it % SEARCH_PERIOD == SEARCH_PERIOD - 1, mid_b, mid_i)
            mid = jnp.where(done, lo, mid)
            new = probe(mid, lo, hi, c_lo, f_lo, f_hi, last)
            new = tuple(jnp.where(done, old, upd) for old, upd in zip((lo, hi, c_lo, f_lo, f_hi, last), new))
            pending = jnp.max(jnp.where(resolved(new[0], new[1], new[2]), 0.0, 1.0))
            return (it + 1,) + new + (pending,)

        state = (lo0, hi0, c_lo0, excess(c_lo0), jnp.full((bq, LANES), excess(jnp.float32(0.0)), F32),
                 jnp.zeros((bq, LANES), F32))
        for z in (0, 1):
            lo, hi = state[0], state[1]
            zk = jnp.full((bq, LANES), z, I32)
            inside = jnp.logical_and(jnp.logical_and(zk > lo, zk < hi), jnp.logical_not(resolved(lo, hi, state[2])))
            new = probe(jnp.where(inside, zk, lo), *state)
            state = tuple(jnp.where(inside, upd, old) for old, upd in zip(state, new))
        pending0 = jnp.max(jnp.where(resolved(state[0], state[1], state[2]), 0.0, 1.0))
        state = lax.while_loop(lambda st: st[7] > 0.0, search_step, (jnp.int32(0),) + state + (pending0,))
        tau = state[1]
        c_gt = count_ge(tau + 1)
        need = jnp.where(tau == INT_MIN, 0.0, k_f - c_gt)
        upper = lax.broadcasted_iota(I32, (LANES, LANES), 0) <= lax.broadcasted_iota(I32, (LANES, LANES), 1)
        scan = jnp.concatenate([jnp.where(upper, 1.0, 0.0), jnp.ones((LANES, LANES), F32)], axis=1).astype(BF16)
        cnt_ref[...] = jnp.zeros((bq, LANES), F32)

        def build_mask(cc, carry):
            t = keys_ref[cc]
            cols = []
            seen = cnt_ref[...]
            for j in range(nlc):
                tj = lane_chunk(t, j)
                tie = tj == tau
                pt = jnp.dot(jnp.where(tie, 1.0, 0.0).astype(BF16), scan, preferred_element_type=F32)
                keep_tie = jnp.logical_and(tie, seen + pt[:, :LANES] <= need)
                cols.append(jnp.where(jnp.logical_or(tj > tau, keep_tie), 0.0, NEG_INF))
                seen = seen + pt[:, LANES:]
            cnt_ref[...] = seen
            keys_ref[cc] = lax.bitcast_convert_type(jnp.concatenate(cols, axis=1), I32)
            return carry

        lax.fori_loop(0, n_chunks, build_mask, 0)

        m_ref[...] = jnp.full(m_ref.shape, NEG_INF, F32)
        acc_ref[...] = jnp.zeros_like(acc_ref)

    _select()

    c1 = (ATT_HEAD_DIM ** -0.5) * LOG2E
    q = qa_ref[0]
    ones_v = jnp.ones((bk, ATT_HEAD_DIM), BF16)

    def heads(near, maskf, kk, vv):
        for h in range(ATT_HEADS):
            hs = slice(h * ATT_HEAD_DIM, (h + 1) * ATT_HEAD_DIM)
            s = lax.dot_general(q[:, hs], kk[:, hs], (((1,), (1,)), ((), ())), preferred_element_type=F32)
            t = s * c1 + maskf
            if near:
                t = t + bias_ref[h]
            m_old = m_ref[h]
            m_new = jnp.maximum(m_old, jnp.max(t, axis=1, keepdims=True))
            m_safe = jnp.where(m_new == NEG_INF, 0.0, m_new)
            alpha = jnp.exp2(m_old - m_safe)
            p = jnp.exp2(t - tile_lanes(m_safe)).astype(BF16)
            v_aug = jnp.concatenate([vv[:, hs], ones_v], axis=1)
            acc_ref[h] = tile_lanes(alpha, 2) * acc_ref[h] + jnp.dot(p, v_aug, preferred_element_type=F32)
            m_ref[h] = m_new

    def attend(c, carry):
        slot = c % KV_BUFFERS
        nxt = c + (KV_BUFFERS - 1)

        @pl.when(nxt < n_chunks)
        def _():
            kv_copy(nxt, nxt % KV_BUFFERS).start()

        kv_copy(c, slot).wait()
        maskf = lax.bitcast_convert_type(keys_ref[c], F32)
        kk = kv_buf[slot, :, KV_K:KV_K + ATT_W]
        vv = kv_buf[slot, :, KV_V:KV_V + ATT_W]
        far = qmin_ref[b * (seq // bq) + qi] - kmax_ref[b * (seq // bk) + c] >= far_dist

        @pl.when(far)
        def _():
            heads(False, maskf, kk, vv)

        @pl.when(jnp.logical_not(far))
        def _():
            nsub = seq // LANES
            for r in range(nsub_q):
                for j in range(nsub_k):
                    rows = slice(r * LANES, (r + 1) * LANES)
                    cols = slice(j * LANES, (j + 1) * LANES)
                    sub_far = (qmin_sub_ref[b * nsub + qi * nsub_q + r]
                               - kmax_sub_ref[b * nsub + c * nsub_k + j]) >= far_dist

                    @pl.when(sub_far)
                    def _():
                        for h in range(ATT_HEADS):
                            bias_ref[h, rows, cols] = jnp.zeros((LANES, LANES), F32)

                    @pl.when(jnp.logical_not(sub_far))
                    def _():
                        half = LANES // 2
                        for hr in range(2):
                            rr = slice(r * LANES + hr * half, r * LANES + (hr + 1) * half)
                            dist = posq_ref[0, rr, :] - posk_ref[0, c, :, cols]
                            bias = [jnp.full((half, LANES), relb_ref[h], F32) for h in range(ATT_HEADS)]
                            for start, bucket in bucket_starts:
                                ge = dist >= start
                                for h in range(ATT_HEADS):
                                    bias[h] = jnp.where(ge, relb_ref[bucket * ATT_HEADS + h], bias[h])
                            for h in range(ATT_HEADS):
                                bias_ref[h, rr, cols] = bias[h]

            heads(True, maskf, kk, vv)

        return carry

    lax.fori_loop(0, n_chunks, attend, 0)

    outs = []
    for h in range(ATT_HEADS):
        a = acc_ref[h]
        outs.append(a[:, :ATT_HEAD_DIM] / a[:, ATT_HEAD_DIM:])
    o_ref[0] = jnp.concatenate(outs, axis=1).astype(BF16)


def _attn(p_a13, p_qi3, p_kv3, positions, rel_table, *, bq, bk):
    b, s, _ = p_qi3.shape
    assert bq % LANES == 0 and bk % LANES == 0
    n_keep = min(TOPK_MAX, s // 4)
    assert n_keep <= 2 * LANES
    nq, nkc, nsub = s // bq, s // bk, s // LANES
    qmin =jnp.min(positions.reshape(b, nq, bq), axis=-1).reshape(-1)
    kmax = jnp.max(positions.reshape(b, nkc, bk), axis=-1).reshape(-1)
    qmin_sub = jnp.min(positions.reshape(b, nsub, LANES), axis=-1).reshape(-1)
    kmax_sub = jnp.max(positions.reshape(b, nsub, LANES), axis=-1).reshape(-1)
    starts = _bucket_starts()
    far_dist = REL_MAX_DIST - 1
    assert starts[-1][0] <= far_dist and starts[-1][1] == REL_BUCKETS - 1
    relb = ((rel_table - rel_table[REL_BUCKETS - 1:REL_BUCKETS, :]) * LOG2E).astype(F32).reshape(-1)

    grid_spec = pltpu.PrefetchScalarGridSpec(
        num_scalar_prefetch=5,
        grid=(b, nq),
        in_specs=[
            pl.BlockSpec((1, bq, ATT_W), lambda i, t, *_: (i, t, A1_QA // ATT_W)),
            pl.BlockSpec((1, bq, QI_W), lambda i, t, *_: (i, t, 0)),
            pl.BlockSpec((1, s, LANES), lambda i, t, *_: (i, 0, QI_KIW // LANES)),
            pl.BlockSpec(memory_space=pl.ANY),
            pl.BlockSpec((1, bq, 1), lambda i, t, *_: (i, t, 0)),
            pl.BlockSpec((1, nkc, 1, bk), lambda i, t, *_: (i, 0, 0, 0)),
        ],
        out_specs=pl.BlockSpec((1, bq, ATT_W), lambda i, t, *_: (i, t, 0)),
        scratch_shapes=[
            pltpu.VMEM((KV_BUFFERS, bk, KV_W), BF16),
            pltpu.SemaphoreType.DMA((KV_BUFFERS,)),
            pltpu.VMEM((nkc, bq, bk), I32),
            pltpu.VMEM((bq, LANES), I32),
            pltpu.VMEM((bq, LANES), I32),
            pltpu.VMEM((bq, LANES), F32),
            pltpu.VMEM((ATT_HEADS, bq, LANES), F32),
            pltpu.VMEM((ATT_HEADS, bq, 2 * ATT_HEAD_DIM), F32),
            pltpu.VMEM((ATT_HEADS, bq, bk), F32),
        ],
    )
    kern = functools.partial(_attn_kernel, bq=bq, bk=bk, n_keep=n_keep, seq=s,
                             far_dist=far_dist, bucket_starts=starts)
    return pl.pallas_call(
        kern,
        out_shape=jax.ShapeDtypeStruct((b, s, ATT_W), BF16),
        grid_spec=grid_spec,
        compiler_params=_cparams(("parallel", "arbitrary")),
        name="sparse_attn",
    )(qmin, kmax, qmin_sub, kmax_sub, relb,
      p_a13, p_qi3, p_qi3, p_kv3, positions.reshape(b, s, 1), positions.reshape(b, nkc, 1, bk))


def _merge_kernel(or_ref, oa_ref, os_ref, gr_ref, ga_ref, gs_ref, wr_ref, wa_ref, ws_ref, wo_ref, x_ref, o_ref):
    j = pl.program_id(1)

    @pl.when(j == 0)
    def _():
        o_ref[...] = jnp.zeros_like(o_ref)

    def branch(o, w, g):
        return jax.nn.sigmoid(g[0].astype(F32)) * jnp.dot(o[...], w[...], preferred_element_type=F32)

    merged = branch(or_ref, wr_ref, gr_ref) + branch(oa_ref, wa_ref, ga_ref) + branch(os_ref, ws_ref, gs_ref)
    o_ref[...] += jnp.dot(merged.astype(BF16), wo_ref[...], preferred_element_type=F32)

    @pl.when(j == pl.num_programs(1) - 1)
    def _():
        o_ref[...] = x_ref[...] + o_ref[...]


def _merge(o_r, o_a, o_s, p_g, wr, wa, ws, wo, x2, *, tm):
    t, d = x2.shape
    tn = PROJ_TILE
    per_gate = d // tn

    def gspec(g):
        return pl.BlockSpec((1, tm, tn), lambda i, j: (g * per_gate + j, i, 0))

    return pl.pallas_call(
        _merge_kernel,
        out_shape=jax.ShapeDtypeStruct((t, d), F32),
        grid=(t // tm, per_gate),
        in_specs=[
            pl.BlockSpec((tm, RET_V_W), lambda i, j: (i, 0)),
            pl.BlockSpec((tm, ATT_W), lambda i, j: (i, 0)),
            pl.BlockSpec((tm, SG_W), lambda i, j: (i, 0)),
            gspec(0), gspec(1), gspec(2),
            pl.BlockSpec((RET_V_W, tn), lambda i, j: (0, j)),
            pl.BlockSpec((ATT_W, tn), lambda i, j: (0, j)),
            pl.BlockSpec((SG_W, tn), lambda i, j: (0, j)),
            pl.BlockSpec((tn, d), lambda i, j: (j, 0)),
            pl.BlockSpec((tm, d), lambda i, j: (i, 0)),
        ],
        out_specs=pl.BlockSpec((tm, d), lambda i, j: (i, 0)),
        compiler_params=_cparams(("parallel", "arbitrary")),
        name="merge",
    )(o_r, o_a, o_s, p_g, p_g, p_g, wr, wa, ws, wo, x2)


def _split_w_in_kernel(w_ref, wa_ref, wb_ref):
    w = w_ref[0]
    wa_ref[:, :W_A_COLS] = w[:, :W_A_COLS].astype(BF16)
    wa_ref[:, W_A_COLS:] = jnp.zeros((w.shape[0], W_A_PAD - W_A_COLS), BF16)
    wb_ref[...] = w[:, W_A_COLS:].astype(BF16)


def _split_w_in(w_in, layer, d_model, *, tr=64):
    _, rows, cols = w_in.shape
    assert cols == W_A_COLS + 2 * SG_W + 3 * d_model and A1_W + KV_W + QI_W == W_A_PAD
    w_b_cols = cols - W_A_COLS
    assert w_b_cols % PROJ_TILE == 0 and rows % tr == 0
    return pl.pallas_call(
        _split_w_in_kernel,
        out_shape=(jax.ShapeDtypeStruct((rows, W_A_PAD), BF16), jax.ShapeDtypeStruct((rows, w_b_cols), BF16)),
        grid=(rows // tr,),
        in_specs=[pl.BlockSpec((1, tr, cols), lambda i: (layer, i, 0))],
        out_specs=(pl.BlockSpec((tr, W_A_PAD), lambda i: (i, 0)), pl.BlockSpec((tr, w_b_cols), lambda i: (i, 0))),
        compiler_params=_cparams(("parallel",)),
        name="split_w_in",
    )(w_in)


def _pick(n, prefs):
    for p in prefs:
        if n % p == 0:
            return p
    return n


def kernel(x, positions, rel_table, ffn1_norm, ffn1_w_gate, ffn1_w_up, ffn1_w_down, mix_norm, w_in, ret_norm, sg_norm, sg_w, sg_b, w_br_ret, w_br_att, w_br_sg, w_out, ffn2_norm, ffn2_w_gate, ffn2_w_up, ffn2_w_down, final_norm):
    b, s, d = x.shape
    t = b * s
    depth = w_in.shape[0]
    d_ff = ffn1_w_gate.shape[2]
    assert s % RET_CHUNK == 0 and d % PROJ_TILE == 0

    tm = _pick(t, (512, 256, 128))
    tm_proj = _pick(t, (1024, 512, 256, 128))
    tm_ffn = _pick(t, (1024, 512, 256, 128))
    tf = _pick(d_ff, (512, 256, 128))
    bq = _pick(s, (256, 128))
    bk = _pick(s, (512, 256, 128))

    cos_t, sin_t = _rope_tables(positions, tm=_pick(s, (2048, 1024, 512, 256, 128)))
    consts = _retention_constants()
    x2 = x.reshape(t, d)
    for l in range(depth):
        x2 = _ffn(x2, ffn1_norm[l], ffn1_w_gate[l].astype(BF16), ffn1_w_up[l].astype(BF16),
                  ffn1_w_down[l].astype(BF16), tm=tm_ffn, tf=tf)
        p_a1, p_kv, p_qi, p_sg, p_g = _proj(x2, mix_norm[l], *_split_w_in(w_in, l, d), tm=tm_proj)
        p_a13 = p_a1.reshape(b, s, -1)
        o_r, o_s = _local(p_a13, p_sg.reshape(b, s, -1), cos_t, sin_t, ret_norm[l], sg_norm[l], sg_w[l], sg_b[l],
                          consts)
        o_a = _attn(p_a13, p_qi.reshape(b, s, -1), p_kv.reshape(b, s, -1), positions, rel_table, bq=bq, bk=bk)
        x2 = _merge(o_r.reshape(t, -1), o_a.reshape(t, -1), o_s.reshape(t, -1), p_g,
                    w_br_ret[l].astype(BF16), w_br_att[l].astype(BF16), w_br_sg[l].astype(BF16),
                    w_out[l].astype(BF16), x2, tm=tm)
        x2 = _ffn(x2, ffn2_norm[l], ffn2_w_gate[l].astype(BF16), ffn2_w_up[l].astype(BF16),
                  ffn2_w_down[l].astype(BF16), final_norm if l == depth - 1 else None, tm=tm_ffn, tf=tf)
    return x2.reshape(b, s, d)
```

```python
import functools
import math

import numpy as np
import jax
import jax.numpy as jnp
from jax import lax
from jax.experimental import pallas as pl
from jax.experimental.pallas import tpu as pltpu

F32 = jnp.float32
BF16 = jnp.bfloat16
I32 = jnp.int32

RET_HEADS, RET_QK_DIM, RET_V_DIM, RET_CHUNK = 6, 64, 128, 128
ATT_HEADS, ATT_HEAD_DIM = 6, 128
IDX_HEADS, IDX_DIM = 4, 64
TOPK_MAX = 256
SG_GROUPS, SG_GROUP_DIM, SG_CHUNK = 4, 128, 128
REL_BUCKETS, REL_MAX_DIST = 32, 128
ROPE_BASE = 10000.0
EPS = 1e-6

RET_QK_W = RET_HEADS * RET_QK_DIM
RET_V_W = RET_HEADS * RET_V_DIM
ATT_W = ATT_HEADS * ATT_HEAD_DIM
IDX_Q_W = IDX_HEADS * IDX_DIM
SG_W = SG_GROUPS * SG_GROUP_DIM

LANES = 128
COUNT_ROWS = 64
KV_BUFFERS = 3
SEARCH_PERIOD = 6
SOFTMAX_ROWS = 64
VMEM_LIMIT_BYTES = 56 * 1024 * 1024

PROJ_TILE = 512
W_A_COLS = 2 * RET_QK_W + 2 * RET_V_W + 3 * ATT_W + IDX_Q_W + IDX_DIM + IDX_HEADS
W_A_PAD = 5120
A1_QR, A1_KR, A1_VR, A1_GR, A1_QA = 0, 384, 768, 1536, 2304
A1_W = 3072
KV_K, KV_V = 0, 768
KV_W = 1536
QI_Q, QI_KIW = 0, 256
QI_W = 512
SG_U, SG_V = 0, 512
SG_PW = 1024

INT_MIN = np.int32(-2 ** 31)
INT_MAX = np.int32(2 ** 31 - 1)
NEG_INF = float("-inf")
LOG2E = math.log2(math.e)


def _cparams(sem):
    return pltpu.CompilerParams(dimension_semantics=sem, vmem_limit_bytes=VMEM_LIMIT_BYTES)


def _rms(x, g):
    return x * lax.rsqrt(jnp.mean(x * x, axis=-1, keepdims=True) + EPS) * g


def _ffn_kernel(x_ref, g_ref, wg_ref, wu_ref, wd_ref, *rest, final_norm):
    if final_norm:
        fg_ref, o_ref, xn_ref = rest
    else:
        o_ref, xn_ref = rest
    j = pl.program_id(1)

    @pl.when(j == 0)
    def _():
        xn_ref[...] = _rms(x_ref[...], g_ref[...]).astype(BF16)
        o_ref[...] = jnp.zeros_like(o_ref)

    xn = xn_ref[...]
    a = jnp.dot(xn, wg_ref[...], preferred_element_type=F32)
    b = jnp.dot(xn, wu_ref[...], preferred_element_type=F32)
    h = (a * jax.nn.sigmoid(a) * b).astype(BF16)
    o_ref[...] += jnp.dot(h, wd_ref[...], preferred_element_type=F32)

    @pl.when(j == pl.num_programs(1) - 1)
    def _():
        y = x_ref[...] + 0.5 * o_ref[...]
        if final_norm:
            y = _rms(y, fg_ref[...])
        o_ref[...] = y


def _ffn(x2, gain, wg, wu, wd, final_gain=None, *, tm, tf):
    t, d = x2.shape
    f = wg.shape[1]
    final_norm = final_gain is not None
    in_specs = [
        pl.BlockSpec((tm, d), lambda i, j: (i, 0)),
        pl.BlockSpec((1, d), lambda i, j: (0, 0)),
        pl.BlockSpec((d, tf), lambda i, j: (0, j)),
        pl.BlockSpec((d, tf), lambda i, j: (0, j)),
        pl.BlockSpec((tf, d), lambda i, j: (j, 0)),
    ]
    args = [x2, gain.reshape(1, d), wg, wu, wd]
    if final_norm:
        in_specs.append(pl.BlockSpec((1, d), lambda i, j: (0, 0)))
        args.append(final_gain.reshape(1, d))
    return pl.pallas_call(
        functools.partial(_ffn_kernel, final_norm=final_norm),
        out_shape=jax.ShapeDtypeStruct((t, d), F32),
        grid=(t // tm, f // tf),
        in_specs=in_specs,
        out_specs=pl.BlockSpec((tm, d), lambda i, j: (i, 0)),
        scratch_shapes=[pltpu.VMEM((tm, d), BF16)],
        compiler_params=_cparams(("parallel", "arbitrary")),
        name="ffn",
    )(*args)


def _proj_kernel(x_ref, g_ref, wa_ref, wb_ref, oa1_ref, okv_ref, oqi_ref, osg_ref, og_ref, xn_ref, *, ends):
    j = pl.program_id(1)
    e_a1, e_kv, e_qi, e_sg = ends

    @pl.when(j == 0)
    def _():
        xn_ref[...] = _rms(x_ref[...], g_ref[...]).astype(BF16)

    def tile(w_ref):
        return jnp.dot(xn_ref[...], w_ref[...], preferred_element_type=F32).astype(BF16)

    @pl.when(j < e_a1)
    def _():
        oa1_ref[...] = tile(wa_ref)

    @pl.when(jnp.logical_and(j >= e_a1, j < e_kv))
    def _():
        okv_ref[...] = tile(wa_ref)

    @pl.when(jnp.logical_and(j >= e_kv, j < e_qi))
    def _():
        oqi_ref[...] = tile(wa_ref)

    @pl.when(jnp.logical_and(j >= e_qi, j < e_sg))
    def _():
        osg_ref[...] = tile(wb_ref)

    @pl.when(j >= e_sg)
    def _():
        og_ref[0] = tile(wb_ref)


def _proj(x2, gain, w_a, w_b, *, tm):
    t, d = x2.shape
    tn = PROJ_TILE
    n_a, n_b = w_a.shape[1] // tn, w_b.shape[1] // tn
    e_a1 = A1_W // tn
    e_kv = e_a1 + KV_W // tn
    e_qi = e_kv + QI_W // tn
    e_sg = e_qi + SG_PW // tn
    assert e_qi == n_a
    t_g = n_a + n_b - e_sg

    def clampspec(first, count):
        return pl.BlockSpec((tm, tn), lambda i, j: (i, jnp.clip(j - first, 0, count - 1)))

    return pl.pallas_call(
        functools.partial(_proj_kernel, ends=(e_a1, e_kv, e_qi, e_sg)),
        out_shape=(jax.ShapeDtypeStruct((t, A1_W), BF16),
                   jax.ShapeDtypeStruct((t, KV_W), BF16),
                   jax.ShapeDtypeStruct((t, QI_W), BF16),
                   jax.ShapeDtypeStruct((t, SG_PW), BF16),
                   jax.ShapeDtypeStruct((t_g, t, tn), BF16)),
        grid=(t // tm, n_a + n_b),
        in_specs=[
            pl.BlockSpec((tm, d), lambda i, j: (i, 0)),
            pl.BlockSpec((1, d), lambda i, j: (0, 0)),
            pl.BlockSpec((d, tn), lambda i, j: (0, jnp.clip(j, 0, n_a - 1))),
            pl.BlockSpec((d, tn), lambda i, j: (0, jnp.clip(j - n_a, 0, n_b - 1))),
        ],
        out_specs=(clampspec(0, e_a1), clampspec(e_a1, e_kv - e_a1), clampspec(e_kv, e_qi - e_kv),
                   clampspec(e_qi, e_sg - e_qi),
                   pl.BlockSpec((1, tm, tn), lambda i, j: (jnp.clip(j - e_sg, 0, t_g - 1), i, 0))),
        scratch_shapes=[pltpu.VMEM((tm, d), BF16)],
        compiler_params=_cparams(("arbitrary", "arbitrary")),
        name="proj",
    )(x2, gain.reshape(1, d), w_a, w_b)


def _rope_table_kernel(pos_ref, invf_ref, sign_ref, c_ref, s_ref):
    ang = pos_ref[0].astype(F32) * invf_ref[...]
    c_ref[0] = jnp.cos(ang)
    s_ref[0] = jnp.sin(ang) * sign_ref[...]


def _rope_tables(positions, *, tm):
    b, s = positions.shape
    half = RET_QK_DIM // 2
    inv_freq = ROPE_BASE ** (-jnp.arange(0, RET_QK_DIM, 2, dtype=F32) / RET_QK_DIM)
    lane = np.arange(LANES)
    invf = inv_freq[lane % half].reshape(1, LANES)
    sign = jnp.asarray(np.where(lane % RET_QK_DIM < half, -1.0, 1.0), F32).reshape(1, LANES)
    out = jax.ShapeDtypeStruct((b, s, LANES), F32)
    return pl.pallas_call(
        _rope_table_kernel,
        out_shape=(out, out),
        grid=(b, s // tm),
        in_specs=[
            pl.BlockSpec((1, tm, 1), lambda i, j: (i, j, 0)),
            pl.BlockSpec((1, LANES), lambda i, j: (0, 0)),
            pl.BlockSpec((1, LANES), lambda i, j: (0, 0)),
        ],
        out_specs=(pl.BlockSpec((1, tm, LANES), lambda i, j: (i, j, 0)),
                   pl.BlockSpec((1, tm, LANES), lambda i, j: (i, j, 0))),
        compiler_params=_cparams(("parallel", "parallel")),
        name="rope_tables",
    )(positions.reshape(b, s, 1), invf, sign)


def _retention_constants():
    c = RET_CHUNK
    h = np.arange(RET_HEADS, dtype=np.float64)
    log_g = np.log(1.0 - 2.0 ** (-5.0 - h))
    idx = np.arange(c, dtype=np.float64)
    rel = idx[:, None] - idx[None, :]
    dmat = np.where(rel >= 0, np.exp(np.maximum(rel, 0.0) * log_g[:, None, None]), 0.0)
    k_decay = np.exp((c - 1 - idx)[None, :] * log_g[:, None])
    q_decay = np.exp((idx + 1)[None, :] * log_g[:, None])
    chunk_decay = np.exp(c * log_g)
    kd = np.repeat(k_decay.T, RET_QK_DIM, axis=1)
    qd = np.repeat(q_decay.T, RET_QK_DIM, axis=1) * RET_QK_DIM ** -0.5
    lane = np.arange(RET_QK_W)
    src = (lane // RET_QK_DIM) * RET_QK_DIM + (lane % RET_QK_DIM + RET_QK_DIM // 2) % RET_QK_DIM
    perm = np.zeros((RET_QK_W, RET_QK_W), np.float32)
    perm[src, lane] = 1.0
    return (jnp.asarray(dmat, F32), jnp.asarray(qd, F32), jnp.asarray(kd, F32),
            [float(np.float32(v)) for v in chunk_decay], jnp.asarray(perm, BF16))


def _gelu_tanh(x):
    return 0.5 * x * (1.0 + jnp.tanh(math.sqrt(2.0 / math.pi) * (x + 0.044715 * (x * x * x))))


def _local_kernel(p_ref, psg_ref, cos_ref, sin_ref,
                  perm_ref, dmat_ref, qd_ref, kd_ref, rn_ref, sn_ref, sgw_ref, sgb_ref,
                  or_ref, os_ref, state_ref, *, chunk_decay):
    @pl.when(pl.program_id(1) == 0)
    def _():
        state_ref[...] = jnp.zeros_like(state_ref)

    n_rep = RET_QK_W // LANES
    cos3 = jnp.concatenate([cos_ref[0]] * n_rep, axis=1)
    sin3 = jnp.concatenate([sin_ref[0]] * n_rep, axis=1)
    q = p_ref[0, :, A1_QR:A1_QR + RET_QK_W]
    k = p_ref[0, :, A1_KR:A1_KR + RET_QK_W]
    perm = perm_ref[...]
    qr = q.astype(F32) * cos3 + jnp.dot(q, perm, preferred_element_type=F32) * sin3
    kr = k.astype(F32) * cos3 + jnp.dot(k, perm, preferred_element_type=F32) * sin3
    q_in = (qr * (RET_QK_DIM ** -0.5)).astype(BF16)
    q_cr = (qr * qd_ref[...]).astype(BF16)
    k_b = kr.astype(BF16)
    k_dec_t = jnp.transpose(kr * kd_ref[...]).astype(BF16)
    v = p_ref[0, :, A1_VR:A1_VR + RET_V_W]
    outs = []
    for h in range(RET_HEADS):
        qs = slice(h * RET_QK_DIM, (h + 1) * RET_QK_DIM)
        vh = v[:, h * RET_V_DIM:(h + 1) * RET_V_DIM]
        a = lax.dot_general(q_in[:, qs], k_b[:, qs], (((1,), (1,)), ((), ())),
                            preferred_element_type=F32) * dmat_ref[h]
        prev = state_ref[h]
        o = (jnp.dot(a.astype(BF16), vh, preferred_element_type=F32)
             + jnp.dot(q_cr[:, qs], prev.astype(BF16), preferred_element_type=F32))
        state_ref[h] = prev * chunk_decay[h] + jnp.dot(k_dec_t[qs, :], vh, preferred_element_type=F32)
        mu = jnp.mean(o, axis=-1, keepdims=True)
        d = o - mu
        var = jnp.mean(d * d, axis=-1, keepdims=True)
        outs.append(d * lax.rsqrt(var + EPS))
    g = p_ref[0, :, A1_GR:A1_GR + RET_V_W].astype(F32)
    y = jnp.concatenate(outs, axis=1) * rn_ref[...]
    or_ref[0] = (y * (g * jax.nn.sigmoid(g))).astype(BF16)

    u = _gelu_tanh(psg_ref[0, :, SG_U:SG_U + SG_W].astype(F32))
    vn = _rms(_gelu_tanh(psg_ref[0, :, SG_V:SG_V + SG_W].astype(F32)), sn_ref[...])
    vn_b = vn.astype(BF16)
    c = SG_CHUNK
    tril = lax.broadcasted_iota(I32, (c, c), 0) >= lax.broadcasted_iota(I32, (c, c), 1)
    bias = sgb_ref[...]
    mixed = []
    for gi in range(SG_GROUPS):
        w = jnp.where(tril, sgw_ref[gi], jnp.zeros((), BF16))
        m = jnp.dot(w, vn_b[:, gi * SG_GROUP_DIM:(gi + 1) * SG_GROUP_DIM], preferred_element_type=F32)
        mixed.append(m + bias[:, gi:gi + 1])
    os_ref[0] = (u * jnp.concatenate(mixed, axis=1)).astype(BF16)


def _local(p_a13, p_sg3, cos_t, sin_t, ret_norm, sg_norm, sg_w, sg_b, consts):
    b, s, _ = p_a13.shape
    c = RET_CHUNK
    dmat, qd, kd, chunk_decay, perm = consts

    def whole(arr):
        nd = arr.ndim
        return pl.BlockSpec(arr.shape, lambda i, j: (0,) * nd)

    rn = ret_norm.reshape(1, RET_V_W)
    sn = sg_norm.reshape(1, SG_W)
    sgw = sg_w.astype(BF16)
    sgb_t = jnp.transpose(sg_b)
    tab = pl.BlockSpec((1, c, LANES), lambda i, j: (i, j, 0))
    return pl.pallas_call(
        functools.partial(_local_kernel, chunk_decay=chunk_decay),
        out_shape=(jax.ShapeDtypeStruct((b, s, RET_V_W), BF16),
                   jax.ShapeDtypeStruct((b, s, SG_W), BF16)),
        grid=(b, s // c),
        in_specs=[
            pl.BlockSpec((1, c, A1_W), lambda i, j: (i, j, 0)),
            pl.BlockSpec((1, c, SG_PW), lambda i, j: (i, j, 0)),
            tab, tab,
            whole(perm), whole(dmat), whole(qd), whole(kd), whole(rn), whole(sn), whole(sgw), whole(sgb_t),
        ],
        out_specs=(pl.BlockSpec((1, c, RET_V_W), lambda i, j: (i, j, 0)),
                   pl.BlockSpec((1, c, SG_W), lambda i, j: (i, j, 0))),
        scratch_shapes=[pltpu.VMEM((RET_HEADS, RET_QK_DIM, RET_V_DIM), F32)],
        compiler_params=_cparams(("parallel", "arbitrary")),
        name="local_mixers",
    )(p_a13, p_sg3, cos_t, sin_t, perm, dmat, qd, kd, rn, sn, sgw, sgb_t)


def _t5_bucket_table():
    max_exact = REL_BUCKETS // 2
    d = np.arange(REL_MAX_DIST)
    df = np.maximum(d, 1).astype(np.float32)
    large = max_exact + (np.log(df / max_exact) / np.float32(math.log(REL_MAX_DIST / max_exact))
                         * (REL_BUCKETS - max_exact)).astype(np.int32)
    large = np.minimum(large, REL_BUCKETS - 1)
    return np.where(d < max_exact, d, large)


def _bucket_starts():
    bk = _t5_bucket_table()
    assert np.all(np.diff(bk) >= 0) and bk[-1] == REL_BUCKETS - 1
    return [(int(np.argmax(bk == v)), int(v)) for v in sorted(set(bk.tolist())) if v > 0]


def _attn_kernel(qmin_ref, kmax_ref, qmin_sub_ref, kmax_sub_ref, relb_ref,
                 qa_ref, pqi_ref, kiw_ref, kv_hbm, posq_ref, posk_ref,
                 o_ref, kv_buf, kv_sem, keys_ref, thr_ref, aux_ref, cnt_ref, m_ref, acc_ref, bias_ref,
                 s_ref, p_ref, alpha_ref,
                 *, bq, bk, n_keep, seq, far_dist, bucket_starts):
    b = pl.program_id(0)
    qi = pl.program_id(1)
    n_chunks = (qi * bq + bq - 1) // bk + 1
    nlc = bk // LANES

    def kv_copy(cc, slot):
        return pltpu.make_async_copy(kv_hbm.at[b, pl.ds(cc * bk, bk), :], kv_buf.at[slot], kv_sem.at[slot])

    for i in range(KV_BUFFERS - 1):
        @pl.when(i < n_chunks)
        def _():
            kv_copy(i, i).start()
    nsub_q, nsub_k = bq // LANES, bk // LANES
    k_f = float(n_keep)

    def tile_lanes(x, n=None):
        return jnp.concatenate([x] * (nlc if n is None else n), axis=1)

    def lane_chunk(x, j):
        return x[:, j * LANES:(j + 1) * LANES]

    def _select():
        qiv = pqi_ref[0, :, QI_Q:QI_Q + IDX_Q_W]
        zpad = jnp.zeros((bq, LANES - IDX_DIM), BF16)
        q_heads = [jnp.concatenate([qiv[:, h * IDX_DIM:(h + 1) * IDX_DIM], zpad], axis=1)
                   for h in range(IDX_HEADS)]
        kiw_q = pqi_ref[0, :, QI_KIW:QI_KIW + LANES]
        w = kiw_q[:, IDX_DIM:IDX_DIM + IDX_HEADS].astype(F32)
        w_bc = [jnp.broadcast_to(w[:, h:h + 1], (bq, bk)) for h in range(IDX_HEADS)]
        row_t = qi * bq + lax.broadcasted_iota(I32, (bq, bk), 0)
        lane_s = lax.broadcasted_iota(I32, (bq, bk), 1)

        def score_body(cc, carry):
            off = pl.multiple_of(cc * bk, bk)
            kc = kiw_ref[0, pl.ds(off, bk), :]
            sc = jnp.zeros((bq, bk), F32)
            for h in range(IDX_HEADS):
                sh = lax.dot_general(q_heads[h], kc, (((1,), (1,)), ((), ())), preferred_element_type=F32)
                sc = sc + w_bc[h] * jnp.maximum(sh, 0.0)
            sc = jnp.where(sc == 0.0, 0.0, sc)
            bits = lax.bitcast_convert_type(sc, I32)
            key = jnp.where(bits < 0, bits ^ INT_MAX, bits)
            key = jnp.where(cc * bk + lane_s <= row_t, key, INT_MIN)
            keys_ref[cc] = key
            for r in range(bq // COUNT_ROWS):
                rows = slice(r * COUNT_ROWS, (r + 1) * COUNT_ROWS)
                m1 = thr_ref[rows, :]
                m2 = aux_ref[rows, :]
                for j in range(nlc):
                    x = key[rows, j * LANES:(j + 1) * LANES]
                    m2 = jnp.maximum(m2, jnp.minimum(m1, x))
                    m1 = jnp.maximum(m1, x)
                thr_ref[rows, :] = m1
                aux_ref[rows, :] = m2
            return carry

        thr_ref[...] = jnp.full((bq, LANES), INT_MIN, I32)
        aux_ref[...] = jnp.full((bq, LANES), INT_MIN, I32)
        lax.fori_loop(0, n_chunks, score_body, 0)
        top1, top2 = thr_ref[...], aux_ref[...]

        nt = bq // LANES

        def to_rows(c):
            return jnp.concatenate(
                [jnp.transpose(jnp.broadcast_to(c[k:k + 1, :], (LANES, LANES))) for k in range(nt)], axis=0)

        def from_rows(x, op):
            return jnp.concatenate(
                [op(jnp.transpose(x[k * LANES:(k + 1) * LANES, :]), axis=0, keepdims=True) for k in range(nt)], axis=0)

        def count_ge(thr):
            thr_ref[...] = to_rows(thr)
            cnt_ref[...] = jnp.zeros((bq, LANES), F32)

            def body(cc, carry):
                for r in range(bq // COUNT_ROWS):
                    rows = pl.ds(r * COUNT_ROWS, COUNT_ROWS)
                    t = keys_ref[cc, rows, :]
                    th = thr_ref[rows, :]
                    acc = cnt_ref[rows, :]
                    for j in range(nlc):
                        acc = acc + jnp.where(lane_chunk(t, j) >= th, 1.0, 0.0)
                    cnt_ref[rows, :] = acc
                return carry

            lax.fori_loop(0, n_chunks, body, 0)
            return from_rows(cnt_ref[...], jnp.sum)

        def key_value(k):
            return lax.bitcast_convert_type(jnp.where(k < 0, k ^ INT_MAX, k), F32)

        def value_key(v):
            bits = lax.bitcast_convert_type(v, I32)
            return jnp.where(bits < 0, bits ^ INT_MAX, bits)

        lo0 = from_rows(top2, jnp.min)
        top = from_rows(top1, jnp.max)
        hi0 = jnp.where(top == INT_MAX, INT_MAX, top + 1)
        c_lo0 = count_ge(lo0)
        log2_k = math.log2(k_f)

        def excess(cnt):
            return jnp.log2(jnp.maximum(cnt, 0.5)) - log2_k

        def resolved(lo, hi, c_lo):
            return jnp.logical_or(hi - 1 <= lo, c_lo == k_f)

        def probe(mid, lo, hi, c_lo, f_lo, f_hi, last):
            cnt = count_ge(mid)
            ok = cnt >= k_f
            f_new = excess(cnt)
            f_hi = jnp.where(jnp.logical_and(ok, last > 0.0), 0.5 * f_hi, f_hi)
            f_lo = jnp.where(jnp.logical_and(jnp.logical_not(ok), last < 0.0), 0.5 * f_lo, f_lo)
            return (jnp.where(ok, mid, lo), jnp.where(ok, hi, mid), jnp.where(ok, cnt, c_lo),
                    jnp.where(ok, f_new, f_lo), jnp.where(ok, f_hi, f_new), jnp.where(ok, 1.0, -1.0))

        def search_step(state):
            it, lo, hi, c_lo, f_lo, f_hi, last, _ = state
            done = resolved(lo, hi, c_lo)
            mid_b = (lo & hi) + ((lo ^ hi) >> 1)
            v_lo = key_value(lo)
            mid_i = jnp.clip(value_key(v_lo + f_lo / (f_lo - f_hi) * (key_value(hi) - v_lo)), lo + 1, hi - 1)
            mid = jnp.where(it % SEARCH_PERIOD == SEARCH_PERIOD - 1, mid_b, mid_i)
            mid = jnp.where(done, lo, mid)
            new = probe(mid, lo, hi, c_lo, f_lo, f_hi, last)
            new = tuple(jnp.where(done, old, upd) for old, upd in zip((lo, hi, c_lo, f_lo, f_hi, last), new))
            pending = jnp.max(jnp.where(resolved(new[0], new[1], new[2]), 0.0, 1.0))
            return (it + 1,) + new + (pending,)

        state = (lo0, hi0, c_lo0, excess(c_lo0), jnp.full((nt, LANES), excess(jnp.float32(0.0)), F32),
                 jnp.zeros((nt, LANES), F32))
        for z in (0, 1):
            lo, hi = state[0], state[1]
            zk = jnp.full((nt, LANES), z, I32)
            inside = jnp.logical_and(jnp.logical_and(zk > lo, zk < hi), jnp.logical_not(resolved(lo, hi, state[2])))
            new = probe(jnp.where(inside, zk, lo), *state)
            state = tuple(jnp.where(inside, upd, old) for old, upd in zip(state, new))
        pending0 = jnp.max(jnp.where(resolved(state[0], state[1], state[2]), 0.0, 1.0))
        state = lax.while_loop(lambda st: st[7] > 0.0, search_step, (jnp.int32(0),) + state + (pending0,))
        tau_c = state[1]
        c_gt = count_ge(tau_c + 1)
        need = to_rows(jnp.where(tau_c == INT_MIN, 0.0, k_f - c_gt))
        tau = to_rows(tau_c)
        upper = lax.broadcasted_iota(I32, (LANES, LANES), 0) <= lax.broadcasted_iota(I32, (LANES, LANES), 1)
        scan = jnp.concatenate([jnp.where(upper, 1.0, 0.0), jnp.ones((LANES, LANES), F32)], axis=1).astype(BF16)
        cnt_ref[...] = jnp.zeros((bq, LANES), F32)

        def build_mask(cc, carry):
            t = keys_ref[cc]
            cols = []
            seen = cnt_ref[...]
            for j in range(nlc):
                tj = lane_chunk(t, j)
                tie = tj == tau
                pt = jnp.dot(jnp.where(tie, 1.0, 0.0).astype(BF16), scan, preferred_element_type=F32)
                keep_tie = jnp.logical_and(tie, seen + pt[:, :LANES] <= need)
                cols.append(jnp.where(jnp.logical_or(tj > tau, keep_tie), 0.0, NEG_INF))
                seen = seen + pt[:, LANES:]
            cnt_ref[...] = seen
            keys_ref[cc] = lax.bitcast_convert_type(jnp.concatenate(cols, axis=1), I32)
            return carry

        lax.fori_loop(0, n_chunks, build_mask, 0)

        m_ref[...] = jnp.full(m_ref.shape, NEG_INF, F32)
        acc_ref[...] = jnp.zeros_like(acc_ref)

    _select()

    c1 = (ATT_HEAD_DIM ** -0.5) * LOG2E
    q = qa_ref[0]
    ones_v = jnp.ones((bk, ATT_HEAD_DIM), BF16)

    def heads(near, c, kk, vv):
        for h in range(ATT_HEADS):
            hs = slice(h * ATT_HEAD_DIM, (h + 1) * ATT_HEAD_DIM)
            s_ref[...] = lax.dot_general(q[:, hs], kk[:, hs], (((1,), (1,)), ((), ())),
                                         preferred_element_type=F32)
            for r in range(bq // SOFTMAX_ROWS):
                rows = slice(r * SOFTMAX_ROWS, (r + 1) * SOFTMAX_ROWS)
                t = s_ref[rows, :] * c1 + lax.bitcast_convert_type(keys_ref[c, rows, :], F32)
                if near:
                    t = t + bias_ref[h, rows, :]
                m_old = m_ref[h, rows, :]
                m_new = jnp.maximum(m_old, jnp.max(t, axis=1, keepdims=True))
                m_safe = jnp.where(m_new == NEG_INF, 0.0, m_new)
                alpha_ref[rows, :] = jnp.exp2(m_old - m_safe)
                p_ref[rows, :] = jnp.exp2(t - tile_lanes(m_safe)).astype(BF16)
                m_ref[h, rows, :] = m_new
            v_aug = jnp.concatenate([vv[:, hs], ones_v], axis=1)
            acc_ref[h] = (tile_lanes(alpha_ref[...], 2) * acc_ref[h]
                          + jnp.dot(p_ref[...], v_aug, preferred_element_type=F32))

    def attend(c, carry):
        slot = c % KV_BUFFERS
        nxt = c + (KV_BUFFERS - 1)

        @pl.when(nxt < n_chunks)
        def _():
            kv_copy(nxt, nxt % KV_BUFFERS).start()

        kv_copy(c, slot).wait()
        kk = kv_buf[slot, :, KV_K:KV_K + ATT_W]
        vv = kv_buf[slot, :, KV_V:KV_V + ATT_W]
        far = qmin_ref[b * (seq // bq) + qi] - kmax_ref[b * (seq // bk) + c] >= far_dist

        @pl.when(far)
        def _():
            heads(False, c, kk, vv)

        @pl.when(jnp.logical_not(far))
        def _():
            nsub = seq // LANES
            for r in range(nsub_q):
                for j in range(nsub_k):
                    rows = slice(r * LANES, (r + 1) * LANES)
                    cols = slice(j * LANES, (j + 1) * LANES)
                    sub_far = (qmin_sub_ref[b * nsub + qi * nsub_q + r]
                               - kmax_sub_ref[b * nsub + c * nsub_k + j]) >= far_dist

                    @pl.when(sub_far)
                    def _():
                        for h in range(ATT_HEADS):
                            bias_ref[h, rows, cols] = jnp.zeros((LANES, LANES), F32)

                    @pl.when(jnp.logical_not(sub_far))
                    def _():
                        half = LANES // 2
                        for hr in range(2):
                            rr = slice(r * LANES + hr * half, r * LANES + (hr + 1) * half)
                            dist = posq_ref[0, rr, :] - posk_ref[0, c, :, cols]
                            bias = [jnp.full((half, LANES), relb_ref[h], F32) for h in range(ATT_HEADS)]
                            for start, bucket in bucket_starts:
                                ge = dist >= start
                                for h in range(ATT_HEADS):
                                    bias[h] = jnp.where(ge, relb_ref[bucket * ATT_HEADS + h], bias[h])
                            for h in range(ATT_HEADS):
                                bias_ref[h, rr, cols] = bias[h]

            heads(True, c, kk, vv)

        return carry

    lax.fori_loop(0, n_chunks, attend, 0)

    outs = []
    for h in range(ATT_HEADS):
        a = acc_ref[h]
        outs.append(a[:, :ATT_HEAD_DIM] / a[:, ATT_HEAD_DIM:])
    o_ref[0] = jnp.concatenate(outs, axis=1).astype(BF16)


def _attn(p_a13, p_qi3, p_kv3, positions, rel_table, *, bq, bk):
    b, s, _ = p_qi3.shape
    assert bq % LANES == 0 and bk % LANES == 0
    n_keep = min(TOPK_MAX, s // 4)
    assert n_keep <= 2 * LANES
    nq, nkc, nsub = s // bq, s // bk, s // LANES
    qmin =jnp.min(positions.reshape(b, nq, bq), axis=-1).reshape(-1)
    kmax = jnp.max(positions.reshape(b, nkc, bk), axis=-1).reshape(-1)
    qmin_sub = jnp.min(positions.reshape(b, nsub, LANES), axis=-1).reshape(-1)
    kmax_sub = jnp.max(positions.reshape(b, nsub, LANES), axis=-1).reshape(-1)
    starts = _bucket_starts()
    far_dist = REL_MAX_DIST - 1
    assert starts[-1][0] <= far_dist and starts[-1][1] == REL_BUCKETS - 1
    relb = ((rel_table - rel_table[REL_BUCKETS - 1:REL_BUCKETS, :]) * LOG2E).astype(F32).reshape(-1)

    grid_spec = pltpu.PrefetchScalarGridSpec(
        num_scalar_prefetch=5,
        grid=(b, nq),
        in_specs=[
            pl.BlockSpec((1, bq, ATT_W), lambda i, t, *_: (i, t, A1_QA // ATT_W)),
            pl.BlockSpec((1, bq, QI_W), lambda i, t, *_: (i, t, 0)),
            pl.BlockSpec((1, s, LANES), lambda i, t, *_: (i, 0, QI_KIW // LANES)),
            pl.BlockSpec(memory_space=pl.ANY),
            pl.BlockSpec((1, bq, 1), lambda i, t, *_: (i, t, 0)),
            pl.BlockSpec((1, nkc, 1, bk), lambda i, t, *_: (i, 0, 0, 0)),
        ],
        out_specs=pl.BlockSpec((1, bq, ATT_W), lambda i, t, *_: (i, t, 0)),
        scratch_shapes=[
            pltpu.VMEM((KV_BUFFERS, bk, KV_W), BF16),
            pltpu.SemaphoreType.DMA((KV_BUFFERS,)),
            pltpu.VMEM((nkc, bq, bk), I32),
            pltpu.VMEM((bq, LANES), I32),
            pltpu.VMEM((bq, LANES), I32),
            pltpu.VMEM((bq, LANES), F32),
            pltpu.VMEM((ATT_HEADS, bq, LANES), F32),
            pltpu.VMEM((ATT_HEADS, bq, 2 * ATT_HEAD_DIM), F32),
            pltpu.VMEM((ATT_HEADS, bq, bk), F32),
            pltpu.VMEM((bq, bk), F32),
            pltpu.VMEM((bq, bk), BF16),
            pltpu.VMEM((bq, LANES), F32),
        ],
    )
    kern = functools.partial(_attn_kernel, bq=bq, bk=bk, n_keep=n_keep, seq=s,
                             far_dist=far_dist, bucket_starts=starts)
    return pl.pallas_call(
        kern,
        out_shape=jax.ShapeDtypeStruct((b, s, ATT_W), BF16),
        grid_spec=grid_spec,
        compiler_params=_cparams(("parallel", "arbitrary")),
        name="sparse_attn",
    )(qmin, kmax, qmin_sub, kmax_sub, relb,
      p_a13, p_qi3, p_qi3, p_kv3, positions.reshape(b, s, 1), positions.reshape(b, nkc, 1, bk))


def _merge_kernel(or_ref, oa_ref, os_ref, gr_ref, ga_ref, gs_ref, wr_ref, wa_ref, ws_ref, wo_ref, x_ref, o_ref):
    j = pl.program_id(1)

    @pl.when(j == 0)
    def _():
        o_ref[...] = jnp.zeros_like(o_ref)

    def branch(o, w, g):
        return jax.nn.sigmoid(g[0].astype(F32)) * jnp.dot(o[...], w[...], preferred_element_type=F32)

    merged = branch(or_ref, wr_ref, gr_ref) + branch(oa_ref, wa_ref, ga_ref) + branch(os_ref, ws_ref, gs_ref)
    o_ref[...] += jnp.dot(merged.astype(BF16), wo_ref[...], preferred_element_type=F32)

    @pl.when(j == pl.num_programs(1) - 1)
    def _():
        o_ref[...] = x_ref[...] + o_ref[...]


def _merge(o_r, o_a, o_s, p_g, wr, wa, ws, wo, x2, *, tm):
    t, d = x2.shape
    tn = PROJ_TILE
    per_gate = d // tn

    def gspec(g):
        return pl.BlockSpec((1, tm, tn), lambda i, j: (g * per_gate + j, i, 0))

    return pl.pallas_call(
        _merge_kernel,
        out_shape=jax.ShapeDtypeStruct((t, d), F32),
        grid=(t // tm, per_gate),
        in_specs=[
            pl.BlockSpec((tm, RET_V_W), lambda i, j: (i, 0)),
            pl.BlockSpec((tm, ATT_W), lambda i, j: (i, 0)),
            pl.BlockSpec((tm, SG_W), lambda i, j: (i, 0)),
            gspec(0), gspec(1), gspec(2),
            pl.BlockSpec((RET_V_W, tn), lambda i, j: (0, j)),
            pl.BlockSpec((ATT_W, tn), lambda i, j: (0, j)),
            pl.BlockSpec((SG_W, tn), lambda i, j: (0, j)),
            pl.BlockSpec((tn, d), lambda i, j: (j, 0)),
            pl.BlockSpec((tm, d), lambda i, j: (i, 0)),
        ],
        out_specs=pl.BlockSpec((tm, d), lambda i, j: (i, 0)),
        compiler_params=_cparams(("parallel", "arbitrary")),
        name="merge",
    )(o_r, o_a, o_s, p_g, p_g, p_g, wr, wa, ws, wo, x2)


def _split_w_in_kernel(w_ref, wa_ref, wb_ref):
    w = w_ref[0]
    wa_ref[:, :W_A_COLS] = w[:, :W_A_COLS].astype(BF16)
    wa_ref[:, W_A_COLS:] = jnp.zeros((w.shape[0], W_A_PAD - W_A_COLS), BF16)
    wb_ref[...] = w[:, W_A_COLS:].astype(BF16)


def _split_w_in(w_in, layer, d_model, *, tr=64):
    _, rows, cols = w_in.shape
    assert cols == W_A_COLS + 2 * SG_W + 3 * d_model and A1_W + KV_W + QI_W == W_A_PAD
    w_b_cols = cols - W_A_COLS
    assert w_b_cols % PROJ_TILE == 0 and rows % tr == 0
    return pl.pallas_call(
        _split_w_in_kernel,
        out_shape=(jax.ShapeDtypeStruct((rows, W_A_PAD), BF16), jax.ShapeDtypeStruct((rows, w_b_cols), BF16)),
        grid=(rows // tr,),
        in_specs=[pl.BlockSpec((1, tr, cols), lambda i: (layer, i, 0))],
        out_specs=(pl.BlockSpec((tr, W_A_PAD), lambda i: (i, 0)), pl.BlockSpec((tr, w_b_cols), lambda i: (i, 0))),
        compiler_params=_cparams(("parallel",)),
        name="split_w_in",
    )(w_in)


def _pick(n, prefs):
    for p in prefs:
        if n % p == 0:
            return p
    return n


def kernel(x, positions, rel_table, ffn1_norm, ffn1_w_gate, ffn1_w_up, ffn1_w_down, mix_norm, w_in, ret_norm, sg_norm, sg_w, sg_b, w_br_ret, w_br_att, w_br_sg, w_out, ffn2_norm, ffn2_w_gate, ffn2_w_up, ffn2_w_down, final_norm):
    b, s, d = x.shape
    t = b * s
    depth = w_in.shape[0]
    d_ff = ffn1_w_gate.shape[2]
    assert s % RET_CHUNK == 0 and d % PROJ_TILE == 0

    tm = _pick(t, (512, 256, 128))
    tm_proj = _pick(t, (1024, 512, 256, 128))
    tm_ffn = _pick(t, (512, 256, 128))
    tf = _pick(d_ff, (512, 256, 128))
    bq = _pick(s, (256, 128))
    bk = _pick(s, (512, 256, 128))

    cos_t, sin_t = _rope_tables(positions, tm=_pick(s, (2048, 1024, 512, 256, 128)))
    consts = _retention_constants()
    x2 = x.reshape(t, d)
    for l in range(depth):
        x2 = _ffn(x2, ffn1_norm[l], ffn1_w_gate[l].astype(BF16), ffn1_w_up[l].astype(BF16),
                  ffn1_w_down[l].astype(BF16), tm=tm_ffn, tf=tf)
        p_a1, p_kv, p_qi, p_sg, p_g = _proj(x2, mix_norm[l], *_split_w_in(w_in, l, d), tm=tm_proj)
        p_a13 = p_a1.reshape(b, s, -1)
        o_r, o_s = _local(p_a13, p_sg.reshape(b, s, -1), cos_t, sin_t, ret_norm[l], sg_norm[l], sg_w[l], sg_b[l],
                          consts)
        o_a = _attn(p_a13, p_qi.reshape(b, s, -1), p_kv.reshape(b, s, -1), positions, rel_table, bq=bq, bk=bk)
        x2 = _merge(o_r.reshape(t, -1), o_a.reshape(t, -1), o_s.reshape(t, -1), p_g,
                    w_br_ret[l].astype(BF16), w_br_att[l].astype(BF16), w_br_sg[l].astype(BF16),
                    w_out[l].astype(BF16), x2, tm=tm)
        x2 = _ffn(x2, ffn2_norm[l], ffn2_w_gate[l].astype(BF16), ffn2_w_up[l].astype(BF16),
                  ffn2_w_down[l].astype(BF16), final_norm if l == depth - 1 else None, tm=tm_ffn, tf=tf)
    return x2.reshape(b, s, d)
```

```python
import functools
import math

import numpy as np
import jax
import jax.numpy as jnp
from jax import lax
from jax.experimental import pallas as pl
from jax.experimental.pallas import tpu as pltpu

F32 = jnp.float32
BF16 = jnp.bfloat16
I32 = jnp.int32

RET_HEADS, RET_QK_DIM, RET_V_DIM, RET_CHUNK = 6, 64, 128, 128
ATT_HEADS, ATT_HEAD_DIM = 6, 128
IDX_HEADS, IDX_DIM = 4, 64
TOPK_MAX = 256
SG_GROUPS, SG_GROUP_DIM, SG_CHUNK = 4, 128, 128
REL_BUCKETS, REL_MAX_DIST = 32, 128
ROPE_BASE = 10000.0
EPS = 1e-6

RET_QK_W = RET_HEADS * RET_QK_DIM
RET_V_W = RET_HEADS * RET_V_DIM
ATT_W = ATT_HEADS * ATT_HEAD_DIM
IDX_Q_W = IDX_HEADS * IDX_DIM
SG_W = SG_GROUPS * SG_GROUP_DIM

LANES = 128
COUNT_ROWS = 64
KV_BUFFERS = 3
SEARCH_PERIOD = 6
SOFTMAX_ROWS = 64
VMEM_LIMIT_BYTES = 56 * 1024 * 1024

PROJ_TILE = 512
W_A_COLS = 2 * RET_QK_W + 2 * RET_V_W + 3 * ATT_W + IDX_Q_W + IDX_DIM + IDX_HEADS
W_A_PAD = 5120
A1_QR, A1_KR, A1_VR, A1_GR, A1_QA = 0, 384, 768, 1536, 2304
A1_W = 3072
KV_K, KV_V = 0, 768
KV_W = 1536
QI_Q, QI_KIW = 0, 256
QI_W = 512
SG_U, SG_V = 0, 512
SG_PW = 1024

INT_MIN = np.int32(-2 ** 31)
INT_MAX = np.int32(2 ** 31 - 1)
NEG_INF = float("-inf")
LOG2E = math.log2(math.e)


def _cparams(sem):
    return pltpu.CompilerParams(dimension_semantics=sem, vmem_limit_bytes=VMEM_LIMIT_BYTES)


def _rms(x, g):
    return x * lax.rsqrt(jnp.mean(x * x, axis=-1, keepdims=True) + EPS) * g


def _ffn_kernel(x_ref, g_ref, wg_ref, wu_ref, wd_ref, *rest, final_norm):
    if final_norm:
        fg_ref, o_ref, xn_ref = rest
    else:
        o_ref, xn_ref = rest
    j = pl.program_id(1)

    @pl.when(j == 0)
    def _():
        xn_ref[...] = _rms(x_ref[...], g_ref[...]).astype(BF16)
        o_ref[...] = jnp.zeros_like(o_ref)

    xn = xn_ref[...]
    a = jnp.dot(xn, wg_ref[...], preferred_element_type=F32)
    b = jnp.dot(xn, wu_ref[...], preferred_element_type=F32)
    h = (a * jax.nn.sigmoid(a) * b).astype(BF16)
    o_ref[...] += jnp.dot(h, wd_ref[...], preferred_element_type=F32)

    @pl.when(j == pl.num_programs(1) - 1)
    def _():
        y = x_ref[...] + 0.5 * o_ref[...]
        if final_norm:
            y = _rms(y, fg_ref[...])
        o_ref[...] = y


def _ffn(x2, gain, wg, wu, wd, final_gain=None, *, tm, tf):
    t, d = x2.shape
    f = wg.shape[1]
    final_norm = final_gain is not None
    in_specs = [
        pl.BlockSpec((tm, d), lambda i, j: (i, 0)),
        pl.BlockSpec((1, d), lambda i, j: (0, 0)),
        pl.BlockSpec((d, tf), lambda i, j: (0, j)),
        pl.BlockSpec((d, tf), lambda i, j: (0, j)),
        pl.BlockSpec((tf, d), lambda i, j: (j, 0)),
    ]
    args = [x2, gain.reshape(1, d), wg, wu, wd]
    if final_norm:
        in_specs.append(pl.BlockSpec((1, d), lambda i, j: (0, 0)))
        args.append(final_gain.reshape(1, d))
    return pl.pallas_call(
        functools.partial(_ffn_kernel, final_norm=final_norm),
        out_shape=jax.ShapeDtypeStruct((t, d), F32),
        grid=(t // tm, f // tf),
        in_specs=in_specs,
        out_specs=pl.BlockSpec((tm, d), lambda i, j: (i, 0)),
        scratch_shapes=[pltpu.VMEM((tm, d), BF16)],
        compiler_params=_cparams(("parallel", "arbitrary")),
        name="ffn",
    )(*args)


def _proj_kernel(x_ref, g_ref, wa_ref, wb_ref, oa1_ref, okv_ref, oqi_ref, osg_ref, og_ref, xn_ref, *, ends):
    j = pl.program_id(1)
    e_a1, e_kv, e_qi, e_sg = ends

    @pl.when(j == 0)
    def _():
        xn_ref[...] = _rms(x_ref[...], g_ref[...]).astype(BF16)

    def tile(w_ref):
        return jnp.dot(xn_ref[...], w_ref[...], preferred_element_type=F32).astype(BF16)

    @pl.when(j < e_a1)
    def _():
        oa1_ref[...] = tile(wa_ref)

    @pl.when(jnp.logical_and(j >= e_a1, j < e_kv))
    def _():
        okv_ref[...] = tile(wa_ref)

    @pl.when(jnp.logical_and(j >= e_kv, j < e_qi))
    def _():
        oqi_ref[...] = tile(wa_ref)

    @pl.when(jnp.logical_and(j >= e_qi, j < e_sg))
    def _():
        osg_ref[...] = tile(wb_ref)

    @pl.when(j >= e_sg)
    def _():
        og_ref[0] = tile(wb_ref)


def _proj(x2, gain, w_a, w_b, *, tm):
    t, d = x2.shape
    tn = PROJ_TILE
    n_a, n_b = w_a.shape[1] // tn, w_b.shape[1] // tn
    e_a1 = A1_W // tn
    e_kv = e_a1 + KV_W // tn
    e_qi = e_kv + QI_W // tn
    e_sg = e_qi + SG_PW // tn
    assert e_qi == n_a
    t_g = n_a + n_b - e_sg

    def clampspec(first, count):
        return pl.BlockSpec((tm, tn), lambda i, j: (i, jnp.clip(j - first, 0, count - 1)))

    return pl.pallas_call(
        functools.partial(_proj_kernel, ends=(e_a1, e_kv, e_qi, e_sg)),
        out_shape=(jax.ShapeDtypeStruct((t, A1_W), BF16),
                   jax.ShapeDtypeStruct((t, KV_W), BF16),
                   jax.ShapeDtypeStruct((t, QI_W), BF16),
                   jax.ShapeDtypeStruct((t, SG_PW), BF16),
                   jax.ShapeDtypeStruct((t_g, t, tn), BF16)),
        grid=(t // tm, n_a + n_b),
        in_specs=[
            pl.BlockSpec((tm, d), lambda i, j: (i, 0)),
            pl.BlockSpec((1, d), lambda i, j: (0, 0)),
            pl.BlockSpec((d, tn), lambda i, j: (0, jnp.clip(j, 0, n_a - 1))),
            pl.BlockSpec((d, tn), lambda i, j: (0, jnp.clip(j - n_a, 0, n_b - 1))),
        ],
        out_specs=(clampspec(0, e_a1), clampspec(e_a1, e_kv - e_a1), clampspec(e_kv, e_qi - e_kv),
                   clampspec(e_qi, e_sg - e_qi),
                   pl.BlockSpec((1, tm, tn), lambda i, j: (jnp.clip(j - e_sg, 0, t_g - 1), i, 0))),
        scratch_shapes=[pltpu.VMEM((tm, d), BF16)],
        compiler_params=_cparams(("arbitrary", "arbitrary")),
        name="proj",
    )(x2, gain.reshape(1, d), w_a, w_b)


def _rope_table_kernel(pos_ref, invf_ref, sign_ref, c_ref, s_ref):
    ang = pos_ref[0].astype(F32) * invf_ref[...]
    c_ref[0] = jnp.cos(ang)
    s_ref[0] = jnp.sin(ang) * sign_ref[...]


def _rope_tables(positions, *, tm):
    b, s = positions.shape
    half = RET_QK_DIM // 2
    inv_freq = ROPE_BASE ** (-jnp.arange(0, RET_QK_DIM, 2, dtype=F32) / RET_QK_DIM)
    lane = np.arange(LANES)
    invf = inv_freq[lane % half].reshape(1, LANES)
    sign = jnp.asarray(np.where(lane % RET_QK_DIM < half, -1.0, 1.0), F32).reshape(1, LANES)
    out = jax.ShapeDtypeStruct((b, s, LANES), F32)
    return pl.pallas_call(
        _rope_table_kernel,
        out_shape=(out, out),
        grid=(b, s // tm),
        in_specs=[
            pl.BlockSpec((1, tm, 1), lambda i, j: (i, j, 0)),
            pl.BlockSpec((1, LANES), lambda i, j: (0, 0)),
            pl.BlockSpec((1, LANES), lambda i, j: (0, 0)),
        ],
        out_specs=(pl.BlockSpec((1, tm, LANES), lambda i, j: (i, j, 0)),
                   pl.BlockSpec((1, tm, LANES), lambda i, j: (i, j, 0))),
        compiler_params=_cparams(("parallel", "parallel")),
        name="rope_tables",
    )(positions.reshape(b, s, 1), invf, sign)


def _retention_constants():
    c = RET_CHUNK
    h = np.arange(RET_HEADS, dtype=np.float64)
    log_g = np.log(1.0 - 2.0 ** (-5.0 - h))
    idx = np.arange(c, dtype=np.float64)
    rel = idx[:, None] - idx[None, :]
    dmat = np.where(rel >= 0, np.exp(np.maximum(rel, 0.0) * log_g[:, None, None]), 0.0)
    k_decay = np.exp((c - 1 - idx)[None, :] * log_g[:, None])
    q_decay = np.exp((idx + 1)[None, :] * log_g[:, None])
    chunk_decay = np.exp(c * log_g)
    kd = np.repeat(k_decay.T, RET_QK_DIM, axis=1)
    qd = np.repeat(q_decay.T, RET_QK_DIM, axis=1) * RET_QK_DIM ** -0.5
    lane = np.arange(RET_QK_W)
    src = (lane // RET_QK_DIM) * RET_QK_DIM + (lane % RET_QK_DIM + RET_QK_DIM // 2) % RET_QK_DIM
    perm = np.zeros((RET_QK_W, RET_QK_W), np.float32)
    perm[src, lane] = 1.0
    return (jnp.asarray(dmat, F32), jnp.asarray(qd, F32), jnp.asarray(kd, F32),
            [float(np.float32(v)) for v in chunk_decay], jnp.asarray(perm, BF16))


def _gelu_tanh(x):
    return 0.5 * x * (1.0 + jnp.tanh(math.sqrt(2.0 / math.pi) * (x + 0.044715 * (x * x * x))))


def _local_kernel(p_ref, psg_ref, cos_ref, sin_ref,
                  perm_ref, dmat_ref, qd_ref, kd_ref, rn_ref, sn_ref, sgw_ref, sgb_ref,
                  or_ref, os_ref, state_ref, *, chunk_decay):
    @pl.when(pl.program_id(1) == 0)
    def _():
        state_ref[...] = jnp.zeros_like(state_ref)

    n_rep = RET_QK_W // LANES
    cos3 = jnp.concatenate([cos_ref[0]] * n_rep, axis=1)
    sin3 = jnp.concatenate([sin_ref[0]] * n_rep, axis=1)
    q = p_ref[0, :, A1_QR:A1_QR + RET_QK_W]
    k = p_ref[0, :, A1_KR:A1_KR + RET_QK_W]
    perm = perm_ref[...]
    qr = q.astype(F32) * cos3 + jnp.dot(q, perm, preferred_element_type=F32) * sin3
    kr = k.astype(F32) * cos3 + jnp.dot(k, perm, preferred_element_type=F32) * sin3
    q_in = (qr * (RET_QK_DIM ** -0.5)).astype(BF16)
    q_cr = (qr * qd_ref[...]).astype(BF16)
    k_b = kr.astype(BF16)
    k_dec_t = jnp.transpose(kr * kd_ref[...]).astype(BF16)
    v = p_ref[0, :, A1_VR:A1_VR + RET_V_W]
    outs = []
    for h in range(RET_HEADS):
        qs = slice(h * RET_QK_DIM, (h + 1) * RET_QK_DIM)
        vh = v[:, h * RET_V_DIM:(h + 1) * RET_V_DIM]
        a = lax.dot_general(q_in[:, qs], k_b[:, qs], (((1,), (1,)), ((), ())),
                            preferred_element_type=F32) * dmat_ref[h]
        prev = state_ref[h]
        o = (jnp.dot(a.astype(BF16), vh, preferred_element_type=F32)
             + jnp.dot(q_cr[:, qs], prev.astype(BF16), preferred_element_type=F32))
        state_ref[h] = prev * chunk_decay[h] + jnp.dot(k_dec_t[qs, :], vh, preferred_element_type=F32)
        mu = jnp.mean(o, axis=-1, keepdims=True)
        d = o - mu
        var = jnp.mean(d * d, axis=-1, keepdims=True)
        outs.append(d * lax.rsqrt(var + EPS))
    g = p_ref[0, :, A1_GR:A1_GR + RET_V_W].astype(F32)
    y = jnp.concatenate(outs, axis=1) * rn_ref[...]
    or_ref[0] = (y * (g * jax.nn.sigmoid(g))).astype(BF16)

    u = _gelu_tanh(psg_ref[0, :, SG_U:SG_U + SG_W].astype(F32))
    vn = _rms(_gelu_tanh(psg_ref[0, :, SG_V:SG_V + SG_W].astype(F32)), sn_ref[...])
    vn_b = vn.astype(BF16)
    c = SG_CHUNK
    tril = lax.broadcasted_iota(I32, (c, c), 0) >= lax.broadcasted_iota(I32, (c, c), 1)
    bias = sgb_ref[...]
    mixed = []
    for gi in range(SG_GROUPS):
        w = jnp.where(tril, sgw_ref[gi], jnp.zeros((), BF16))
        m = jnp.dot(w, vn_b[:, gi * SG_GROUP_DIM:(gi + 1) * SG_GROUP_DIM], preferred_element_type=F32)
        mixed.append(m + bias[:, gi:gi + 1])
    os_ref[0] = (u * jnp.concatenate(mixed, axis=1)).astype(BF16)


def _local(p_a13, p_sg3, cos_t, sin_t, ret_norm, sg_norm, sg_w, sg_b, consts):
    b, s, _ = p_a13.shape
    c = RET_CHUNK
    dmat, qd, kd, chunk_decay, perm = consts

    def whole(arr):
        nd = arr.ndim
        return pl.BlockSpec(arr.shape, lambda i, j: (0,) * nd)

    rn = ret_norm.reshape(1, RET_V_W)
    sn = sg_norm.reshape(1, SG_W)
    sgw = sg_w.astype(BF16)
    sgb_t = jnp.transpose(sg_b)
    tab = pl.BlockSpec((1, c, LANES), lambda i, j: (i, j, 0))
    return pl.pallas_call(
        functools.partial(_local_kernel, chunk_decay=chunk_decay),
        out_shape=(jax.ShapeDtypeStruct((b, s, RET_V_W), BF16),
                   jax.ShapeDtypeStruct((b, s, SG_W), BF16)),
        grid=(b, s // c),
        in_specs=[
            pl.BlockSpec((1, c, A1_W), lambda i, j: (i, j, 0)),
            pl.BlockSpec((1, c, SG_PW), lambda i, j: (i, j, 0)),
            tab, tab,
            whole(perm), whole(dmat), whole(qd), whole(kd), whole(rn), whole(sn), whole(sgw), whole(sgb_t),
        ],
        out_specs=(pl.BlockSpec((1, c, RET_V_W), lambda i, j: (i, j, 0)),
                   pl.BlockSpec((1, c, SG_W), lambda i, j: (i, j, 0))),
        scratch_shapes=[pltpu.VMEM((RET_HEADS, RET_QK_DIM, RET_V_DIM), F32)],
        compiler_params=_cparams(("parallel", "arbitrary")),
        name="local_mixers",
    )(p_a13, p_sg3, cos_t, sin_t, perm, dmat, qd, kd, rn, sn, sgw, sgb_t)


def _t5_bucket_table():
    max_exact = REL_BUCKETS // 2
    d = np.arange(REL_MAX_DIST)
    df = np.maximum(d, 1).astype(np.float32)
    large = max_exact + (np.log(df / max_exact) / np.float32(math.log(REL_MAX_DIST / max_exact))
                         * (REL_BUCKETS - max_exact)).astype(np.int32)
    large = np.minimum(large, REL_BUCKETS - 1)
    return np.where(d < max_exact, d, large)


def _attn_kernel(qmin_ref, kmax_ref, qmin_sub_ref, kmax_sub_ref,
                 qa_ref, pqi_ref, kiw_ref, kv_hbm, posq_ref, posk_ref, btab_ref,
                 o_ref, kv_buf, kv_sem, keys_ref, thr_ref, aux_ref, cnt_ref, m_ref, acc_ref, bias_ref,
                 s_ref, p_ref, alpha_ref,
                 *, bq, bk, n_keep, seq, far_dist):
    b = pl.program_id(0)
    qi = pl.program_id(1)
    n_chunks = (qi * bq + bq - 1) // bk + 1
    nlc = bk // LANES

    def kv_copy(cc, slot):
        return pltpu.make_async_copy(kv_hbm.at[b, pl.ds(cc * bk, bk), :], kv_buf.at[slot], kv_sem.at[slot])

    for i in range(KV_BUFFERS - 1):
        @pl.when(i < n_chunks)
        def _():
            kv_copy(i, i).start()
    nsub_q, nsub_k = bq // LANES, bk // LANES
    k_f = float(n_keep)

    def tile_lanes(x, n=None):
        return jnp.concatenate([x] * (nlc if n is None else n), axis=1)

    def lane_chunk(x, j):
        return x[:, j * LANES:(j + 1) * LANES]

    def _select():
        qiv = pqi_ref[0, :, QI_Q:QI_Q + IDX_Q_W]
        zpad = jnp.zeros((bq, LANES - IDX_DIM), BF16)
        q_heads = [jnp.concatenate([qiv[:, h * IDX_DIM:(h + 1) * IDX_DIM], zpad], axis=1)
                   for h in range(IDX_HEADS)]
        kiw_q = pqi_ref[0, :, QI_KIW:QI_KIW + LANES]
        w = kiw_q[:, IDX_DIM:IDX_DIM + IDX_HEADS].astype(F32)
        w_bc = [jnp.broadcast_to(w[:, h:h + 1], (bq, bk)) for h in range(IDX_HEADS)]
        row_t = qi * bq + lax.broadcasted_iota(I32, (bq, bk), 0)
        lane_s = lax.broadcasted_iota(I32, (bq, bk), 1)

        def score_body(cc, carry):
            off = pl.multiple_of(cc * bk, bk)
            kc = kiw_ref[0, pl.ds(off, bk), :]
            sc = jnp.zeros((bq, bk), F32)
            for h in range(IDX_HEADS):
                sh = lax.dot_general(q_heads[h], kc, (((1,), (1,)), ((), ())), preferred_element_type=F32)
                sc = sc + w_bc[h] * jnp.maximum(sh, 0.0)
            sc = jnp.where(sc == 0.0, 0.0, sc)
            bits = lax.bitcast_convert_type(sc, I32)
            key = jnp.where(bits < 0, bits ^ INT_MAX, bits)
            key = jnp.where(cc * bk + lane_s <= row_t, key, INT_MIN)
            keys_ref[cc] = key
            for r in range(bq // COUNT_ROWS):
                rows = slice(r * COUNT_ROWS, (r + 1) * COUNT_ROWS)
                m1 = thr_ref[rows, :]
                m2 = aux_ref[rows, :]
                for j in range(nlc):
                    x = key[rows, j * LANES:(j + 1) * LANES]
                    m2 = jnp.maximum(m2, jnp.minimum(m1, x))
                    m1 = jnp.maximum(m1, x)
                thr_ref[rows, :] = m1
                aux_ref[rows, :] = m2
            return carry

        thr_ref[...] = jnp.full((bq, LANES), INT_MIN, I32)
        aux_ref[...] = jnp.full((bq, LANES), INT_MIN, I32)
        lax.fori_loop(0, n_chunks, score_body, 0)
        top1, top2 = thr_ref[...], aux_ref[...]

        nt = bq // LANES

        def to_rows(c):
            return jnp.concatenate(
                [jnp.transpose(jnp.broadcast_to(c[k:k + 1, :], (LANES, LANES))) for k in range(nt)], axis=0)

        def from_rows(x, op):
            return jnp.concatenate(
                [op(jnp.transpose(x[k * LANES:(k + 1) * LANES, :]), axis=0, keepdims=True) for k in range(nt)], axis=0)

        def count_ge(thr):
            thr_ref[...] = to_rows(thr)
            cnt_ref[...] = jnp.zeros((bq, LANES), F32)

            def body(cc, carry):
                for r in range(bq // COUNT_ROWS):
                    rows = pl.ds(r * COUNT_ROWS, COUNT_ROWS)
                    t = keys_ref[cc, rows, :]
                    th = thr_ref[rows, :]
                    acc = cnt_ref[rows, :]
                    for j in range(nlc):
                        acc = acc + jnp.where(lane_chunk(t, j) >= th, 1.0, 0.0)
                    cnt_ref[rows, :] = acc
                return carry

            lax.fori_loop(0, n_chunks, body, 0)
            return from_rows(cnt_ref[...], jnp.sum)

        def key_value(k):
            return lax.bitcast_convert_type(jnp.where(k < 0, k ^ INT_MAX, k), F32)

        def value_key(v):
            bits = lax.bitcast_convert_type(v, I32)
            return jnp.where(bits < 0, bits ^ INT_MAX, bits)

        lo0 = from_rows(top2, jnp.min)
        top = from_rows(top1, jnp.max)
        hi0 = jnp.where(top == INT_MAX, INT_MAX, top + 1)
        c_lo0 = count_ge(lo0)
        log2_k = math.log2(k_f)

        def excess(cnt):
            return jnp.log2(jnp.maximum(cnt, 0.5)) - log2_k

        def resolved(lo, hi, c_lo):
            return jnp.logical_or(hi - 1 <= lo, c_lo == k_f)

        def probe(mid, lo, hi, c_lo, f_lo, f_hi, last):
            cnt = count_ge(mid)
            ok = cnt >= k_f
            f_new = excess(cnt)
            f_hi = jnp.where(jnp.logical_and(ok, last > 0.0), 0.5 * f_hi, f_hi)
            f_lo = jnp.where(jnp.logical_and(jnp.logical_not(ok), last < 0.0), 0.5 * f_lo, f_lo)
            return (jnp.where(ok, mid, lo), jnp.where(ok, hi, mid), jnp.where(ok, cnt, c_lo),
                    jnp.where(ok, f_new, f_lo), jnp.where(ok, f_hi, f_new), jnp.where(ok, 1.0, -1.0))

        def search_step(state):
            it, lo, hi, c_lo, f_lo, f_hi, last, _ = state
            done = resolved(lo, hi, c_lo)
            mid_b = (lo & hi) + ((lo ^ hi) >> 1)
            v_lo = key_value(lo)
            mid_i = jnp.clip(value_key(v_lo + f_lo / (f_lo - f_hi) * (key_value(hi) - v_lo)), lo + 1, hi - 1)
            mid = jnp.where(it % SEARCH_PERIOD == SEARCH_PERIOD - 1, mid_b, mid_i)
            mid = jnp.where(done, lo, mid)
            new = probe(mid, lo, hi, c_lo, f_lo, f_hi, last)
            new = tuple(jnp.where(done, old, upd) for old, upd in zip((lo, hi, c_lo, f_lo, f_hi, last), new))
            pending = jnp.max(jnp.where(resolved(new[0], new[1], new[2]), 0.0, 1.0))
            return (it + 1,) + new + (pending,)

        state = (lo0, hi0, c_lo0, excess(c_lo0), jnp.full((nt, LANES), excess(jnp.float32(0.0)), F32),
                 jnp.zeros((nt, LANES), F32))
        for z in (0, 1):
            lo, hi = state[0], state[1]
            zk = jnp.full((nt, LANES), z, I32)
            inside = jnp.logical_and(jnp.logical_and(zk > lo, zk < hi), jnp.logical_not(resolved(lo, hi, state[2])))
            new = probe(jnp.where(inside, zk, lo), *state)
            state = tuple(jnp.where(inside, upd, old) for old, upd in zip(state, new))
        pending0 = jnp.max(jnp.where(resolved(state[0], state[1], state[2]), 0.0, 1.0))
        state = lax.while_loop(lambda st: st[7] > 0.0, search_step, (jnp.int32(0),) + state + (pending0,))
        tau_c = state[1]
        c_gt = count_ge(tau_c + 1)
        need = to_rows(jnp.where(tau_c == INT_MIN, 0.0, k_f - c_gt))
        tau = to_rows(tau_c)
        upper = lax.broadcasted_iota(I32, (LANES, LANES), 0) <= lax.broadcasted_iota(I32, (LANES, LANES), 1)
        scan = jnp.concatenate([jnp.where(upper, 1.0, 0.0), jnp.ones((LANES, LANES), F32)], axis=1).astype(BF16)
        cnt_ref[...] = jnp.zeros((bq, LANES), F32)

        def build_mask(cc, carry):
            t = keys_ref[cc]
            cols = []
            seen = cnt_ref[...]
            for j in range(nlc):
                tj = lane_chunk(t, j)
                tie = tj == tau
                pt = jnp.dot(jnp.where(tie, 1.0, 0.0).astype(BF16), scan, preferred_element_type=F32)
                keep_tie = jnp.logical_and(tie, seen + pt[:, :LANES] <= need)
                cols.append(jnp.where(jnp.logical_or(tj > tau, keep_tie), 0.0, NEG_INF))
                seen = seen + pt[:, LANES:]
            cnt_ref[...] = seen
            keys_ref[cc] = lax.bitcast_convert_type(jnp.concatenate(cols, axis=1), I32)
            return carry

        lax.fori_loop(0, n_chunks, build_mask, 0)

        m_ref[...] = jnp.full(m_ref.shape, NEG_INF, F32)
        acc_ref[...] = jnp.zeros_like(acc_ref)

    _select()

    c1 = (ATT_HEAD_DIM ** -0.5) * LOG2E
    q = qa_ref[0]
    ones_v = jnp.ones((bk, ATT_HEAD_DIM), BF16)

    def heads(near, c, kk, vv):
        for h in range(ATT_HEADS):
            hs = slice(h * ATT_HEAD_DIM, (h + 1) * ATT_HEAD_DIM)
            s_ref[...] = lax.dot_general(q[:, hs], kk[:, hs], (((1,), (1,)), ((), ())),
                                         preferred_element_type=F32)
            for r in range(bq // SOFTMAX_ROWS):
                rows = slice(r * SOFTMAX_ROWS, (r + 1) * SOFTMAX_ROWS)
                t = s_ref[rows, :] * c1 + lax.bitcast_convert_type(keys_ref[c, rows, :], F32)
                if near:
                    t = t + bias_ref[h, rows, :]
                m_old = m_ref[h, rows, :]
                m_new = jnp.maximum(m_old, jnp.max(t, axis=1, keepdims=True))
                m_safe = jnp.where(m_new == NEG_INF, 0.0, m_new)
                alpha_ref[rows, :] = jnp.exp2(m_old - m_safe)
                p_ref[rows, :] = jnp.exp2(t - tile_lanes(m_safe)).astype(BF16)
                m_ref[h, rows, :] = m_new
            v_aug = jnp.concatenate([vv[:, hs], ones_v], axis=1)
            acc_ref[h] = (tile_lanes(alpha_ref[...], 2) * acc_ref[h]
                          + jnp.dot(p_ref[...], v_aug, preferred_element_type=F32))

    def attend(c, carry):
        slot = c % KV_BUFFERS
        nxt = c + (KV_BUFFERS - 1)

        @pl.when(nxt < n_chunks)
        def _():
            kv_copy(nxt, nxt % KV_BUFFERS).start()

        kv_copy(c, slot).wait()
        kk = kv_buf[slot, :, KV_K:KV_K + ATT_W]
        vv = kv_buf[slot, :, KV_V:KV_V + ATT_W]
        far = qmin_ref[b * (seq // bq) + qi] - kmax_ref[b * (seq // bk) + c] >= far_dist

        @pl.when(far)
        def _():
            heads(False, c, kk, vv)

        @pl.when(jnp.logical_not(far))
        def _():
            nsub = seq // LANES
            for r in range(nsub_q):
                for j in range(nsub_k):
                    rows = slice(r * LANES, (r + 1) * LANES)
                    cols = slice(j * LANES, (j + 1) * LANES)
                    sub_far = (qmin_sub_ref[b * nsub + qi * nsub_q + r]
                               - kmax_sub_ref[b * nsub + c * nsub_k + j]) >= far_dist

                    @pl.when(sub_far)
                    def _():
                        for h in range(ATT_HEADS):
                            bias_ref[h, rows, cols] = jnp.zeros((LANES, LANES), F32)

                    @pl.when(jnp.logical_not(sub_far))
                    def _():
                        half = LANES // 2
                        for hr in range(2):
                            rr = slice(r * LANES + hr * half, r * LANES + (hr + 1) * half)
                            dist = posq_ref[0, rr, :] - posk_ref[0, c, :, cols]
                            idx = jnp.clip(dist, 0, REL_MAX_DIST - 1)
                            for h in range(ATT_HEADS):
                                row = jnp.broadcast_to(btab_ref[h:h + 1, :], (half, LANES))
                                bias_ref[h, rr, cols] = jnp.take_along_axis(row, idx, axis=1)

            heads(True, c, kk, vv)

        return carry

    lax.fori_loop(0, n_chunks, attend, 0)

    outs = []
    for h in range(ATT_HEADS):
        a = acc_ref[h]
        outs.append(a[:, :ATT_HEAD_DIM] / a[:, ATT_HEAD_DIM:])
    o_ref[0] = jnp.concatenate(outs, axis=1).astype(BF16)


def _attn(p_a13, p_qi3, p_kv3, positions, rel_table, *, bq, bk):
    b, s, _ = p_qi3.shape
    assert bq % LANES == 0 and bk % LANES == 0
    n_keep = min(TOPK_MAX, s // 4)
    assert n_keep <= 2 * LANES
    nq, nkc, nsub = s // bq, s // bk, s // LANES
    qmin =jnp.min(positions.reshape(b, nq, bq), axis=-1).reshape(-1)
    kmax = jnp.max(positions.reshape(b, nkc, bk), axis=-1).reshape(-1)
    qmin_sub = jnp.min(positions.reshape(b, nsub, LANES), axis=-1).reshape(-1)
    kmax_sub = jnp.max(positions.reshape(b, nsub, LANES), axis=-1).reshape(-1)
    buckets = _t5_bucket_table()
    far_dist = REL_MAX_DIST - 1
    assert np.all(np.diff(buckets) >= 0) and buckets[far_dist] == REL_BUCKETS - 1
    relb = ((rel_table - rel_table[REL_BUCKETS - 1:REL_BUCKETS, :]) * LOG2E).astype(F32)
    btab = jnp.pad(jnp.transpose(relb[buckets, :]), ((0, 8 - ATT_HEADS), (0, 0)))

    grid_spec = pltpu.PrefetchScalarGridSpec(
        num_scalar_prefetch=4,
        grid=(b, nq),
        in_specs=[
            pl.BlockSpec((1, bq, ATT_W), lambda i, t, *_: (i, t, A1_QA // ATT_W)),
            pl.BlockSpec((1, bq, QI_W), lambda i, t, *_: (i, t, 0)),
            pl.BlockSpec((1, s, LANES), lambda i, t, *_: (i, 0, QI_KIW // LANES)),
            pl.BlockSpec(memory_space=pl.ANY),
            pl.BlockSpec((1, bq, 1), lambda i, t, *_: (i, t, 0)),
            pl.BlockSpec((1, nkc, 1, bk), lambda i, t, *_: (i, 0, 0, 0)),
            pl.BlockSpec((8, LANES), lambda i, t, *_: (0, 0)),
        ],
        out_specs=pl.BlockSpec((1, bq, ATT_W), lambda i, t, *_: (i, t, 0)),
        scratch_shapes=[
            pltpu.VMEM((KV_BUFFERS, bk, KV_W), BF16),
            pltpu.SemaphoreType.DMA((KV_BUFFERS,)),
            pltpu.VMEM((nkc, bq, bk), I32),
            pltpu.VMEM((bq, LANES), I32),
            pltpu.VMEM((bq, LANES), I32),
            pltpu.VMEM((bq, LANES), F32),
            pltpu.VMEM((ATT_HEADS, bq, LANES), F32),
            pltpu.VMEM((ATT_HEADS, bq, 2 * ATT_HEAD_DIM), F32),
            pltpu.VMEM((ATT_HEADS, bq, bk), F32),
            pltpu.VMEM((bq, bk), F32),
            pltpu.VMEM((bq, bk), BF16),
            pltpu.VMEM((bq, LANES), F32),
        ],
    )
    kern = functools.partial(_attn_kernel, bq=bq, bk=bk, n_keep=n_keep, seq=s,
                             far_dist=far_dist)
    return pl.pallas_call(
        kern,
        out_shape=jax.ShapeDtypeStruct((b, s, ATT_W), BF16),
        grid_spec=grid_spec,
        compiler_params=_cparams(("parallel", "arbitrary")),
        name="sparse_attn",
    )(qmin, kmax, qmin_sub, kmax_sub,
      p_a13, p_qi3, p_qi3, p_kv3, positions.reshape(b, s, 1), positions.reshape(b, nkc, 1, bk), btab)


def _merge_kernel(or_ref, oa_ref, os_ref, gr_ref, ga_ref, gs_ref, wr_ref, wa_ref, ws_ref, wo_ref, x_ref, o_ref):
    j = pl.program_id(1)

    @pl.when(j == 0)
    def _():
        o_ref[...] = jnp.zeros_like(o_ref)

    def branch(o, w, g):
        return jax.nn.sigmoid(g[0].astype(F32)) * jnp.dot(o[...], w[...], preferred_element_type=F32)

    merged = branch(or_ref, wr_ref, gr_ref) + branch(oa_ref, wa_ref, ga_ref) + branch(os_ref, ws_ref, gs_ref)
    o_ref[...] += jnp.dot(merged.astype(BF16), wo_ref[...], preferred_element_type=F32)

    @pl.when(j == pl.num_programs(1) - 1)
    def _():
        o_ref[...] = x_ref[...] + o_ref[...]


def _merge(o_r, o_a, o_s, p_g, wr, wa, ws, wo, x2, *, tm):
    t, d = x2.shape
    tn = PROJ_TILE
    per_gate = d // tn

    def gspec(g):
        return pl.BlockSpec((1, tm, tn), lambda i, j: (g * per_gate + j, i, 0))

    return pl.pallas_call(
        _merge_kernel,
        out_shape=jax.ShapeDtypeStruct((t, d), F32),
        grid=(t // tm, per_gate),
        in_specs=[
            pl.BlockSpec((tm, RET_V_W), lambda i, j: (i, 0)),
            pl.BlockSpec((tm, ATT_W), lambda i, j: (i, 0)),
            pl.BlockSpec((tm, SG_W), lambda i, j: (i, 0)),
            gspec(0), gspec(1), gspec(2),
            pl.BlockSpec((RET_V_W, tn), lambda i, j: (0, j)),
            pl.BlockSpec((ATT_W, tn), lambda i, j: (0, j)),
            pl.BlockSpec((SG_W, tn), lambda i, j: (0, j)),
            pl.BlockSpec((tn, d), lambda i, j: (j, 0)),
            pl.BlockSpec((tm, d), lambda i, j: (i, 0)),
        ],
        out_specs=pl.BlockSpec((tm, d), lambda i, j: (i, 0)),
        compiler_params=_cparams(("parallel", "arbitrary")),
        name="merge",
    )(o_r, o_a, o_s, p_g, p_g, p_g, wr, wa, ws, wo, x2)


def _split_w_in_kernel(w_ref, wa_ref, wb_ref):
    w = w_ref[0]
    wa_ref[:, :W_A_COLS] = w[:, :W_A_COLS].astype(BF16)
    wa_ref[:, W_A_COLS:] = jnp.zeros((w.shape[0], W_A_PAD - W_A_COLS), BF16)
    wb_ref[...] = w[:, W_A_COLS:].astype(BF16)


def _split_w_in(w_in, layer, d_model, *, tr=64):
    _, rows, cols = w_in.shape
    assert cols == W_A_COLS + 2 * SG_W + 3 * d_model and A1_W + KV_W + QI_W == W_A_PAD
    w_b_cols = cols - W_A_COLS
    assert w_b_cols % PROJ_TILE == 0 and rows % tr == 0
    return pl.pallas_call(
        _split_w_in_kernel,
        out_shape=(jax.ShapeDtypeStruct((rows, W_A_PAD), BF16), jax.ShapeDtypeStruct((rows, w_b_cols), BF16)),
        grid=(rows // tr,),
        in_specs=[pl.BlockSpec((1, tr, cols), lambda i: (layer, i, 0))],
        out_specs=(pl.BlockSpec((tr, W_A_PAD), lambda i: (i, 0)), pl.BlockSpec((tr, w_b_cols), lambda i: (i, 0))),
        compiler_params=_cparams(("parallel",)),
        name="split_w_in",
    )(w_in)


def _pick(n, prefs):
    for p in prefs:
        if n % p == 0:
            return p
    return n


def kernel(x, positions, rel_table, ffn1_norm, ffn1_w_gate, ffn1_w_up, ffn1_w_down, mix_norm, w_in, ret_norm, sg_norm, sg_w, sg_b, w_br_ret, w_br_att, w_br_sg, w_out, ffn2_norm, ffn2_w_gate, ffn2_w_up, ffn2_w_down, final_norm):
    b, s, d = x.shape
    t = b * s
    depth = w_in.shape[0]
    d_ff = ffn1_w_gate.shape[2]
    assert s % RET_CHUNK == 0 and d % PROJ_TILE == 0

    tm = _pick(t, (512, 256, 128))
    tm_proj = _pick(t, (1024, 512, 256, 128))
    tm_ffn = _pick(t, (512, 256, 128))
    tf = _pick(d_ff, (512, 256, 128))
    bq = _pick(s, (256, 128))
    bk = _pick(s, (512, 256, 128))

    cos_t, sin_t = _rope_tables(positions, tm=_pick(s, (2048, 1024, 512, 256, 128)))
    consts = _retention_constants()
    x2 = x.reshape(t, d)
    for l in range(depth):
        x2 = _ffn(x2, ffn1_norm[l], ffn1_w_gate[l].astype(BF16), ffn1_w_up[l].astype(BF16),
                  ffn1_w_down[l].astype(BF16), tm=tm_ffn, tf=tf)
        p_a1, p_kv, p_qi, p_sg, p_g = _proj(x2, mix_norm[l], *_split_w_in(w_in, l, d), tm=tm_proj)
        p_a13 = p_a1.reshape(b, s, -1)
        o_r, o_s = _local(p_a13, p_sg.reshape(b, s, -1), cos_t, sin_t, ret_norm[l], sg_norm[l], sg_w[l], sg_b[l],
                          consts)
        o_a = _attn(p_a13, p_qi.reshape(b, s, -1), p_kv.reshape(b, s, -1), positions, rel_table, bq=bq, bk=bk)
        x2 = _merge(o_r.reshape(t, -1), o_a.reshape(t, -1), o_s.reshape(t, -1), p_g,
                    w_br_ret[l].astype(BF16), w_br_att[l].astype(BF16), w_br_sg[l].astype(BF16),
                    w_out[l].astype(BF16), x2, tm=tm)
        x2 = _ffn(x2, ffn2_norm[l], ffn2_w_gate[l].astype(BF16), ffn2_w_up[l].astype(BF16),
                  ffn2_w_down[l].astype(BF16), final_norm if l == depth - 1 else None, tm=tm_ffn, tf=tf)
    return x2.reshape(b, s, d)
```

```python
import functools
import math

import numpy as np
import jax
import jax.numpy as jnp
from jax import lax
from jax.experimental import pallas as pl
from jax.experimental.pallas import tpu as pltpu

F32 = jnp.float32
BF16 = jnp.bfloat16
I32 = jnp.int32

RET_HEADS, RET_QK_DIM, RET_V_DIM, RET_CHUNK = 6, 64, 128, 128
ATT_HEADS, ATT_HEAD_DIM = 6, 128
IDX_HEADS, IDX_DIM = 4, 64
TOPK_MAX = 256
SG_GROUPS, SG_GROUP_DIM, SG_CHUNK = 4, 128, 128
REL_BUCKETS, REL_MAX_DIST = 32, 128
ROPE_BASE = 10000.0
EPS = 1e-6

RET_QK_W = RET_HEADS * RET_QK_DIM
RET_V_W = RET_HEADS * RET_V_DIM
ATT_W = ATT_HEADS * ATT_HEAD_DIM
IDX_Q_W = IDX_HEADS * IDX_DIM
SG_W = SG_GROUPS * SG_GROUP_DIM

LANES = 128
COUNT_ROWS = 64
KV_BUFFERS = 3
SEARCH_PERIOD = 6
ATTN_SPAN = 2
SOFTMAX_ROWS = 32
VMEM_LIMIT_BYTES = 56 * 1024 * 1024

PROJ_TILE = 512
W_A_COLS = 2 * RET_QK_W + 2 * RET_V_W + 3 * ATT_W + IDX_Q_W + IDX_DIM + IDX_HEADS
W_A_PAD = 5120
A1_QR, A1_KR, A1_VR, A1_GR, A1_QA = 0, 384, 768, 1536, 2304
A1_W = 3072
KV_K, KV_V = 0, 768
KV_W = 1536
QI_Q, QI_KIW = 0, 256
QI_W = 512
SG_U, SG_V = 0, 512
SG_PW = 1024

INT_MIN = np.int32(-2 ** 31)
INT_MAX = np.int32(2 ** 31 - 1)
NEG_INF = float("-inf")
LOG2E = math.log2(math.e)


def _cparams(sem):
    return pltpu.CompilerParams(dimension_semantics=sem, vmem_limit_bytes=VMEM_LIMIT_BYTES)


def _rms(x, g):
    return x * lax.rsqrt(jnp.mean(x * x, axis=-1, keepdims=True) + EPS) * g


def _ffn_kernel(x_ref, g_ref, wg_ref, wu_ref, wd_ref, *rest, final_norm):
    if final_norm:
        fg_ref, o_ref, xn_ref = rest
    else:
        o_ref, xn_ref = rest
    j = pl.program_id(1)

    @pl.when(j == 0)
    def _():
        xn_ref[...] = _rms(x_ref[...], g_ref[...]).astype(BF16)
        o_ref[...] = jnp.zeros_like(o_ref)

    xn = xn_ref[...]
    a = jnp.dot(xn, wg_ref[...], preferred_element_type=F32)
    b = jnp.dot(xn, wu_ref[...], preferred_element_type=F32)
    h = (a * jax.nn.sigmoid(a) * b).astype(BF16)
    o_ref[...] += jnp.dot(h, wd_ref[...], preferred_element_type=F32)

    @pl.when(j == pl.num_programs(1) - 1)
    def _():
        y = x_ref[...] + 0.5 * o_ref[...]
        if final_norm:
            y = _rms(y, fg_ref[...])
        o_ref[...] = y


def _ffn(x2, gain, wg, wu, wd, final_gain=None, *, tm, tf):
    t, d = x2.shape
    f = wg.shape[1]
    final_norm = final_gain is not None
    in_specs = [
        pl.BlockSpec((tm, d), lambda i, j: (i, 0)),
        pl.BlockSpec((1, d), lambda i, j: (0, 0)),
        pl.BlockSpec((d, tf), lambda i, j: (0, j)),
        pl.BlockSpec((d, tf), lambda i, j: (0, j)),
        pl.BlockSpec((tf, d), lambda i, j: (j, 0)),
    ]
    args = [x2, gain.reshape(1, d), wg, wu, wd]
    if final_norm:
        in_specs.append(pl.BlockSpec((1, d), lambda i, j: (0, 0)))
        args.append(final_gain.reshape(1, d))
    return pl.pallas_call(
        functools.partial(_ffn_kernel, final_norm=final_norm),
        out_shape=jax.ShapeDtypeStruct((t, d), F32),
        grid=(t // tm, f // tf),
        in_specs=in_specs,
        out_specs=pl.BlockSpec((tm, d), lambda i, j: (i, 0)),
        scratch_shapes=[pltpu.VMEM((tm, d), BF16)],
        compiler_params=_cparams(("parallel", "arbitrary")),
        name="ffn",
    )(*args)


def _proj_kernel(x_ref, g_ref, wa_ref, wb_ref, oa1_ref, okv_ref, oqi_ref, osg_ref, og_ref, xn_ref, *, ends):
    j = pl.program_id(1)
    e_a1, e_kv, e_qi, e_sg = ends

    @pl.when(j == 0)
    def _():
        xn_ref[...] = _rms(x_ref[...], g_ref[...]).astype(BF16)

    def tile(w_ref):
        return jnp.dot(xn_ref[...], w_ref[...], preferred_element_type=F32).astype(BF16)

    @pl.when(j < e_a1)
    def _():
        oa1_ref[...] = tile(wa_ref)

    @pl.when(jnp.logical_and(j >= e_a1, j < e_kv))
    def _():
        okv_ref[...] = tile(wa_ref)

    @pl.when(jnp.logical_and(j >= e_kv, j < e_qi))
    def _():
        oqi_ref[...] = tile(wa_ref)

    @pl.when(jnp.logical_and(j >= e_qi, j < e_sg))
    def _():
        osg_ref[...] = tile(wb_ref)

    @pl.when(j >= e_sg)
    def _():
        og_ref[0] = tile(wb_ref)


def _proj(x2, gain, w_a, w_b, *, tm):
    t, d = x2.shape
    tn = PROJ_TILE
    n_a, n_b = w_a.shape[1] // tn, w_b.shape[1] // tn
    e_a1 = A1_W // tn
    e_kv = e_a1 + KV_W // tn
    e_qi = e_kv + QI_W // tn
    e_sg = e_qi + SG_PW // tn
    assert e_qi == n_a
    t_g = n_a + n_b - e_sg

    def clampspec(first, count):
        return pl.BlockSpec((tm, tn), lambda i, j: (i, jnp.clip(j - first, 0, count - 1)))

    return pl.pallas_call(
        functools.partial(_proj_kernel, ends=(e_a1, e_kv, e_qi, e_sg)),
        out_shape=(jax.ShapeDtypeStruct((t, A1_W), BF16),
                   jax.ShapeDtypeStruct((t, KV_W), BF16),
                   jax.ShapeDtypeStruct((t, QI_W), BF16),
                   jax.ShapeDtypeStruct((t, SG_PW), BF16),
                   jax.ShapeDtypeStruct((t_g, t, tn), BF16)),
        grid=(t // tm, n_a + n_b),
        in_specs=[
            pl.BlockSpec((tm, d), lambda i, j: (i, 0)),
            pl.BlockSpec((1, d), lambda i, j: (0, 0)),
            pl.BlockSpec((d, tn), lambda i, j: (0, jnp.clip(j, 0, n_a - 1))),
            pl.BlockSpec((d, tn), lambda i, j: (0, jnp.clip(j - n_a, 0, n_b - 1))),
        ],
        out_specs=(clampspec(0, e_a1), clampspec(e_a1, e_kv - e_a1), clampspec(e_kv, e_qi - e_kv),
                   clampspec(e_qi, e_sg - e_qi),
                   pl.BlockSpec((1, tm, tn), lambda i, j: (jnp.clip(j - e_sg, 0, t_g - 1), i, 0))),
        scratch_shapes=[pltpu.VMEM((tm, d), BF16)],
        compiler_params=_cparams(("arbitrary", "arbitrary")),
        name="proj",
    )(x2, gain.reshape(1, d), w_a, w_b)


def _rope_table_kernel(pos_ref, invf_ref, sign_ref, c_ref, s_ref):
    ang = pos_ref[0].astype(F32) * invf_ref[...]
    c_ref[0] = jnp.cos(ang)
    s_ref[0] = jnp.sin(ang) * sign_ref[...]


def _rope_tables(positions, *, tm):
    b, s = positions.shape
    half = RET_QK_DIM // 2
    inv_freq = ROPE_BASE ** (-jnp.arange(0, RET_QK_DIM, 2, dtype=F32) / RET_QK_DIM)
    lane = np.arange(LANES)
    invf = inv_freq[lane % half].reshape(1, LANES)
    sign = jnp.asarray(np.where(lane % RET_QK_DIM < half, -1.0, 1.0), F32).reshape(1, LANES)
    out = jax.ShapeDtypeStruct((b, s, LANES), F32)
    return pl.pallas_call(
        _rope_table_kernel,
        out_shape=(out, out),
        grid=(b, s // tm),
        in_specs=[
            pl.BlockSpec((1, tm, 1), lambda i, j: (i, j, 0)),
            pl.BlockSpec((1, LANES), lambda i, j: (0, 0)),
            pl.BlockSpec((1, LANES), lambda i, j: (0, 0)),
        ],
        out_specs=(pl.BlockSpec((1, tm, LANES), lambda i, j: (i, j, 0)),
                   pl.BlockSpec((1, tm, LANES), lambda i, j: (i, j, 0))),
        compiler_params=_cparams(("parallel", "parallel")),
        name="rope_tables",
    )(positions.reshape(b, s, 1), invf, sign)


def _retention_constants():
    c = RET_CHUNK
    h = np.arange(RET_HEADS, dtype=np.float64)
    log_g = np.log(1.0 - 2.0 ** (-5.0 - h))
    idx = np.arange(c, dtype=np.float64)
    rel = idx[:, None] - idx[None, :]
    dmat = np.where(rel >= 0, np.exp(np.maximum(rel, 0.0) * log_g[:, None, None]), 0.0)
    k_decay = np.exp((c - 1 - idx)[None, :] * log_g[:, None])
    q_decay = np.exp((idx + 1)[None, :] * log_g[:, None])
    chunk_decay = np.exp(c * log_g)
    kd = np.repeat(k_decay.T, RET_QK_DIM, axis=1)
    qd = np.repeat(q_decay.T, RET_QK_DIM, axis=1) * RET_QK_DIM ** -0.5
    lane = np.arange(RET_QK_W)
    src = (lane // RET_QK_DIM) * RET_QK_DIM + (lane % RET_QK_DIM + RET_QK_DIM // 2) % RET_QK_DIM
    perm = np.zeros((RET_QK_W, RET_QK_W), np.float32)
    perm[src, lane] = 1.0
    return (jnp.asarray(dmat, F32), jnp.asarray(qd, F32), jnp.asarray(kd, F32),
            [float(np.float32(v)) for v in chunk_decay], jnp.asarray(perm, BF16))


def _gelu_tanh(x):
    return 0.5 * x * (1.0 + jnp.tanh(math.sqrt(2.0 / math.pi) * (x + 0.044715 * (x * x * x))))


def _local_kernel(p_ref, psg_ref, cos_ref, sin_ref,
                  perm_ref, dmat_ref, qd_ref, kd_ref, rn_ref, sn_ref, sgw_ref, sgb_ref,
                  or_ref, os_ref, state_ref, *, chunk_decay):
    @pl.when(pl.program_id(1) == 0)
    def _():
        state_ref[...] = jnp.zeros_like(state_ref)

    n_rep = RET_QK_W // LANES
    cos3 = jnp.concatenate([cos_ref[0]] * n_rep, axis=1)
    sin3 = jnp.concatenate([sin_ref[0]] * n_rep, axis=1)
    q = p_ref[0, :, A1_QR:A1_QR + RET_QK_W]
    k = p_ref[0, :, A1_KR:A1_KR + RET_QK_W]
    perm = perm_ref[...]
    qr = q.astype(F32) * cos3 + jnp.dot(q, perm, preferred_element_type=F32) * sin3
    kr = k.astype(F32) * cos3 + jnp.dot(k, perm, preferred_element_type=F32) * sin3
    q_in = (qr * (RET_QK_DIM ** -0.5)).astype(BF16)
    q_cr = (qr * qd_ref[...]).astype(BF16)
    k_b = kr.astype(BF16)
    k_dec_t = jnp.transpose(kr * kd_ref[...]).astype(BF16)
    v = p_ref[0, :, A1_VR:A1_VR + RET_V_W]
    outs = []
    for h in range(RET_HEADS):
        qs = slice(h * RET_QK_DIM, (h + 1) * RET_QK_DIM)
        vh = v[:, h * RET_V_DIM:(h + 1) * RET_V_DIM]
        a = lax.dot_general(q_in[:, qs], k_b[:, qs], (((1,), (1,)), ((), ())),
                            preferred_element_type=F32) * dmat_ref[h]
        prev = state_ref[h]
        o = (jnp.dot(a.astype(BF16), vh, preferred_element_type=F32)
             + jnp.dot(q_cr[:, qs], prev.astype(BF16), preferred_element_type=F32))
        state_ref[h] = prev * chunk_decay[h] + jnp.dot(k_dec_t[qs, :], vh, preferred_element_type=F32)
        mu = jnp.mean(o, axis=-1, keepdims=True)
        d = o - mu
        var = jnp.mean(d * d, axis=-1, keepdims=True)
        outs.append(d * lax.rsqrt(var + EPS))
    g = p_ref[0, :, A1_GR:A1_GR + RET_V_W].astype(F32)
    y = jnp.concatenate(outs, axis=1) * rn_ref[...]
    or_ref[0] = (y * (g * jax.nn.sigmoid(g))).astype(BF16)

    u = _gelu_tanh(psg_ref[0, :, SG_U:SG_U + SG_W].astype(F32))
    vn = _rms(_gelu_tanh(psg_ref[0, :, SG_V:SG_V + SG_W].astype(F32)), sn_ref[...])
    vn_b = vn.astype(BF16)
    c = SG_CHUNK
    tril = lax.broadcasted_iota(I32, (c, c), 0) >= lax.broadcasted_iota(I32, (c, c), 1)
    bias = sgb_ref[...]
    mixed = []
    for gi in range(SG_GROUPS):
        w = jnp.where(tril, sgw_ref[gi], jnp.zeros((), BF16))
        m = jnp.dot(w, vn_b[:, gi * SG_GROUP_DIM:(gi + 1) * SG_GROUP_DIM], preferred_element_type=F32)
        mixed.append(m + bias[:, gi:gi + 1])
    os_ref[0] = (u * jnp.concatenate(mixed, axis=1)).astype(BF16)


def _local(p_a13, p_sg3, cos_t, sin_t, ret_norm, sg_norm, sg_w, sg_b, consts):
    b, s, _ = p_a13.shape
    c = RET_CHUNK
    dmat, qd, kd, chunk_decay, perm = consts

    def whole(arr):
        nd = arr.ndim
        return pl.BlockSpec(arr.shape, lambda i, j: (0,) * nd)

    rn = ret_norm.reshape(1, RET_V_W)
    sn = sg_norm.reshape(1, SG_W)
    sgw = sg_w.astype(BF16)
    sgb_t = jnp.transpose(sg_b)
    tab = pl.BlockSpec((1, c, LANES), lambda i, j: (i, j, 0))
    return pl.pallas_call(
        functools.partial(_local_kernel, chunk_decay=chunk_decay),
        out_shape=(jax.ShapeDtypeStruct((b, s, RET_V_W), BF16),
                   jax.ShapeDtypeStruct((b, s, SG_W), BF16)),
        grid=(b, s // c),
        in_specs=[
            pl.BlockSpec((1, c, A1_W), lambda i, j: (i, j, 0)),
            pl.BlockSpec((1, c, SG_PW), lambda i, j: (i, j, 0)),
            tab, tab,
            whole(perm), whole(dmat), whole(qd), whole(kd), whole(rn), whole(sn), whole(sgw), whole(sgb_t),
        ],
        out_specs=(pl.BlockSpec((1, c, RET_V_W), lambda i, j: (i, j, 0)),
                   pl.BlockSpec((1, c, SG_W), lambda i, j: (i, j, 0))),
        scratch_shapes=[pltpu.VMEM((RET_HEADS, RET_QK_DIM, RET_V_DIM), F32)],
        compiler_params=_cparams(("parallel", "arbitrary")),
        name="local_mixers",
    )(p_a13, p_sg3, cos_t, sin_t, perm, dmat, qd, kd, rn, sn, sgw, sgb_t)


def _t5_bucket_table():
    max_exact = REL_BUCKETS // 2
    d = np.arange(REL_MAX_DIST)
    df = np.maximum(d, 1).astype(np.float32)
    large = max_exact + (np.log(df / max_exact) / np.float32(math.log(REL_MAX_DIST / max_exact))
                         * (REL_BUCKETS - max_exact)).astype(np.int32)
    large = np.minimum(large, REL_BUCKETS - 1)
    return np.where(d < max_exact, d, large)


def _attn_kernel(qmin_ref, kmax_ref, qmin_sub_ref, kmax_sub_ref,
                 qa_ref, pqi_ref, kiw_ref, kv_hbm, posq_ref, posk_ref, btab_ref,
                 o_ref, kv_buf, kv_sem, keys_ref, thr_ref, aux_ref, cnt_ref, m_ref, acc_ref, bias_ref,
                 s_ref, p_ref, alpha_ref,
                 *, bq, bk, n_keep, seq, far_dist):
    b = pl.program_id(0)
    qi = pl.program_id(1)
    n_chunks = (qi * bq + bq - 1) // bk + 1
    nlc = bk // LANES
    bka = bk * ATTN_SPAN
    n_att = (n_chunks + ATTN_SPAN - 1) // ATTN_SPAN

    def kv_copy(ca, slot):
        return pltpu.make_async_copy(kv_hbm.at[b, pl.ds(ca * bka, bka), :], kv_buf.at[slot], kv_sem.at[slot])

    for i in range(KV_BUFFERS - 1):
        @pl.when(i < n_att)
        def _():
            kv_copy(i, i).start()
    nsub_q, nsub_k = bq // LANES, bk // LANES
    k_f = float(n_keep)

    def tile_lanes(x, n=None):
        return jnp.concatenate([x] * (nlc if n is None else n), axis=1)

    def lane_chunk(x, j):
        return x[:, j * LANES:(j + 1) * LANES]

    def _select():
        qiv = pqi_ref[0, :, QI_Q:QI_Q + IDX_Q_W]
        zpad = jnp.zeros((bq, LANES - IDX_DIM), BF16)
        q_heads = [jnp.concatenate([qiv[:, h * IDX_DIM:(h + 1) * IDX_DIM], zpad], axis=1)
                   for h in range(IDX_HEADS)]
        kiw_q = pqi_ref[0, :, QI_KIW:QI_KIW + LANES]
        w = kiw_q[:, IDX_DIM:IDX_DIM + IDX_HEADS].astype(F32)
        w_bc = [jnp.broadcast_to(w[:, h:h + 1], (bq, bk)) for h in range(IDX_HEADS)]
        row_t = qi * bq + lax.broadcasted_iota(I32, (bq, bk), 0)
        lane_s = lax.broadcasted_iota(I32, (bq, bk), 1)

        def score_body(cc, carry):
            off = pl.multiple_of(cc * bk, bk)
            kc = kiw_ref[0, pl.ds(off, bk), :]
            sc = jnp.zeros((bq, bk), F32)
            for h in range(IDX_HEADS):
                sh = lax.dot_general(q_heads[h], kc, (((1,), (1,)), ((), ())), preferred_element_type=F32)
                sc = sc + w_bc[h] * jnp.maximum(sh, 0.0)
            sc = jnp.where(sc == 0.0, 0.0, sc)
            bits = lax.bitcast_convert_type(sc, I32)
            key = jnp.where(bits < 0, bits ^ INT_MAX, bits)
            key = jnp.where(cc * bk + lane_s <= row_t, key, INT_MIN)
            keys_ref[cc] = key
            for r in range(bq // COUNT_ROWS):
                rows = slice(r * COUNT_ROWS, (r + 1) * COUNT_ROWS)
                m1 = thr_ref[rows, :]
                m2 = aux_ref[rows, :]
                for j in range(nlc):
                    x = key[rows, j * LANES:(j + 1) * LANES]
                    m2 = jnp.maximum(m2, jnp.minimum(m1, x))
                    m1 = jnp.maximum(m1, x)
                thr_ref[rows, :] = m1
                aux_ref[rows, :] = m2
            return carry

        thr_ref[...] = jnp.full((bq, LANES), INT_MIN, I32)
        aux_ref[...] = jnp.full((bq, LANES), INT_MIN, I32)
        lax.fori_loop(0, n_chunks, score_body, 0)
        top1, top2 = thr_ref[...], aux_ref[...]

        nt = bq // LANES

        def to_rows(c):
            return jnp.concatenate(
                [jnp.transpose(jnp.broadcast_to(c[k:k + 1, :], (LANES, LANES))) for k in range(nt)], axis=0)

        def from_rows(x, op):
            return jnp.concatenate(
                [op(jnp.transpose(x[k * LANES:(k + 1) * LANES, :]), axis=0, keepdims=True) for k in range(nt)], axis=0)

        def count_ge(thr):
            thr_ref[...] = to_rows(thr)
            cnt_ref[...] = jnp.zeros((bq, LANES), F32)

            def body(cc, carry):
                for r in range(bq // COUNT_ROWS):
                    rows = pl.ds(r * COUNT_ROWS, COUNT_ROWS)
                    t = keys_ref[cc, rows, :]
                    th = thr_ref[rows, :]
                    acc = cnt_ref[rows, :]
                    for j in range(nlc):
                        acc = acc + jnp.where(lane_chunk(t, j) >= th, 1.0, 0.0)
                    cnt_ref[rows, :] = acc
                return carry

            lax.fori_loop(0, n_chunks, body, 0)
            return from_rows(cnt_ref[...], jnp.sum)

        def key_value(k):
            return lax.bitcast_convert_type(jnp.where(k < 0, k ^ INT_MAX, k), F32)

        def value_key(v):
            bits = lax.bitcast_convert_type(v, I32)
            return jnp.where(bits < 0, bits ^ INT_MAX, bits)

        lo0 = from_rows(top2, jnp.min)
        top = from_rows(top1, jnp.max)
        hi0 = jnp.where(top == INT_MAX, INT_MAX, top + 1)
        c_lo0 = count_ge(lo0)
        log2_k = math.log2(k_f)

        def excess(cnt):
            return jnp.log2(jnp.maximum(cnt, 0.5)) - log2_k

        def resolved(lo, hi, c_lo):
            return jnp.logical_or(hi - 1 <= lo, c_lo == k_f)

        def probe(mid, lo, hi, c_lo, f_lo, f_hi, last):
            cnt = count_ge(mid)
            ok = cnt >= k_f
            f_new = excess(cnt)
            f_hi = jnp.where(jnp.logical_and(ok, last > 0.0), 0.5 * f_hi, f_hi)
            f_lo = jnp.where(jnp.logical_and(jnp.logical_not(ok), last < 0.0), 0.5 * f_lo, f_lo)
            return (jnp.where(ok, mid, lo), jnp.where(ok, hi, mid), jnp.where(ok, cnt, c_lo),
                    jnp.where(ok, f_new, f_lo), jnp.where(ok, f_hi, f_new), jnp.where(ok, 1.0, -1.0))

        def search_step(state):
            it, lo, hi, c_lo, f_lo, f_hi, last, _ = state
            done = resolved(lo, hi, c_lo)
            mid_b = (lo & hi) + ((lo ^ hi) >> 1)
            v_lo = key_value(lo)
            mid_i = jnp.clip(value_key(v_lo + f_lo / (f_lo - f_hi) * (key_value(hi) - v_lo)), lo + 1, hi - 1)
            mid = jnp.where(it % SEARCH_PERIOD == SEARCH_PERIOD - 1, mid_b, mid_i)
            mid = jnp.where(done, lo, mid)
            new = probe(mid, lo, hi, c_lo, f_lo, f_hi, last)
            new = tuple(jnp.where(done, old, upd) for old, upd in zip((lo, hi, c_lo, f_lo, f_hi, last), new))
            pending = jnp.max(jnp.where(resolved(new[0], new[1], new[2]), 0.0, 1.0))
            return (it + 1,) + new + (pending,)

        state = (lo0, hi0, c_lo0, excess(c_lo0), jnp.full((nt, LANES), excess(jnp.float32(0.0)), F32),
                 jnp.zeros((nt, LANES), F32))
        for z in (0, 1):
            lo, hi = state[0], state[1]
            zk = jnp.full((nt, LANES), z, I32)
            inside = jnp.logical_and(jnp.logical_and(zk > lo, zk < hi), jnp.logical_not(resolved(lo, hi, state[2])))
            new = probe(jnp.where(inside, zk, lo), *state)
            state = tuple(jnp.where(inside, upd, old) for old, upd in zip(state, new))
        pending0 = jnp.max(jnp.where(resolved(state[0], state[1], state[2]), 0.0, 1.0))
        state = lax.while_loop(lambda st: st[7] > 0.0, search_step, (jnp.int32(0),) + state + (pending0,))
        tau_c = state[1]
        c_gt = count_ge(tau_c + 1)
        need = to_rows(jnp.where(tau_c == INT_MIN, 0.0, k_f - c_gt))
        tau = to_rows(tau_c)
        upper = lax.broadcasted_iota(I32, (LANES, LANES), 0) <= lax.broadcasted_iota(I32, (LANES, LANES), 1)
        scan = jnp.concatenate([jnp.where(upper, 1.0, 0.0), jnp.ones((LANES, LANES), F32)], axis=1).astype(BF16)
        cnt_ref[...] = jnp.zeros((bq, LANES), F32)

        def build_mask(cc, carry):
            t = keys_ref[cc]
            cols = []
            seen = cnt_ref[...]
            for j in range(nlc):
                tj = lane_chunk(t, j)
                tie = tj == tau
                pt = jnp.dot(jnp.where(tie, 1.0, 0.0).astype(BF16), scan, preferred_element_type=F32)
                keep_tie = jnp.logical_and(tie, seen + pt[:, :LANES] <= need)
                cols.append(jnp.where(jnp.logical_or(tj > tau, keep_tie), 0.0, NEG_INF))
                seen = seen + pt[:, LANES:]
            cnt_ref[...] = seen
            keys_ref[cc] = lax.bitcast_convert_type(jnp.concatenate(cols, axis=1), I32)
            return carry

        lax.fori_loop(0, n_chunks, build_mask, 0)

        def mask_out(cc, carry):
            keys_ref[cc] = lax.bitcast_convert_type(jnp.full((bq, bk), NEG_INF, F32), I32)
            return carry

        lax.fori_loop(n_chunks, n_att * ATTN_SPAN, mask_out, 0)

        m_ref[...] = jnp.full(m_ref.shape, NEG_INF, F32)
        acc_ref[...] = jnp.zeros_like(acc_ref)

    _select()

    c1 = (ATT_HEAD_DIM ** -0.5) * LOG2E
    q = qa_ref[0]
    ones_v = jnp.ones((bka, ATT_HEAD_DIM), BF16)

    def heads(near, ca, kk, vv):
        for h in range(ATT_HEADS):
            hs = slice(h * ATT_HEAD_DIM, (h + 1) * ATT_HEAD_DIM)
            s_ref[...] = lax.dot_general(q[:, hs], kk[:, hs], (((1,), (1,)), ((), ())),
                                         preferred_element_type=F32)
            for r in range(bq // SOFTMAX_ROWS):
                rows = slice(r * SOFTMAX_ROWS, (r + 1) * SOFTMAX_ROWS)
                mask = jnp.concatenate([keys_ref[ca * ATTN_SPAN + i, rows, :] for i in range(ATTN_SPAN)], axis=1)
                t = s_ref[rows, :] * c1 + lax.bitcast_convert_type(mask, F32)
                if near:
                    t = t + bias_ref[h, rows, :]
                m_old = m_ref[h, rows, :]
                m_new = jnp.maximum(m_old, jnp.max(t, axis=1, keepdims=True))
                m_safe = jnp.where(m_new == NEG_INF, 0.0, m_new)
                alpha_ref[rows, :] = jnp.exp2(m_old - m_safe)
                p_ref[rows, :] = jnp.exp2(t - tile_lanes(m_safe, bka // LANES)).astype(BF16)
                m_ref[h, rows, :] = m_new
            v_aug = jnp.concatenate([vv[:, hs], ones_v], axis=1)
            acc_ref[h] = (tile_lanes(alpha_ref[...], 2) * acc_ref[h]
                          + jnp.dot(p_ref[...], v_aug, preferred_element_type=F32))

    def attend(ca, carry):
        slot = ca % KV_BUFFERS
        nxt = ca + (KV_BUFFERS - 1)

        @pl.when(nxt < n_att)
        def _():
            kv_copy(nxt, nxt % KV_BUFFERS).start()

        kv_copy(ca, slot).wait()
        kk = kv_buf[slot, :, KV_K:KV_K + ATT_W]
        vv = kv_buf[slot, :, KV_V:KV_V + ATT_W]
        kmax = kmax_ref[b * (seq // bk) + ca * ATTN_SPAN]
        for i in range(1, ATTN_SPAN):
            kmax = jnp.maximum(kmax, kmax_ref[b * (seq // bk) + ca * ATTN_SPAN + i])
        far = qmin_ref[b * (seq // bq) + qi] - kmax >= far_dist

        @pl.when(far)
        def _():
            heads(False, ca, kk, vv)

        @pl.when(jnp.logical_not(far))
        def _():
            nsub = seq // LANES
            for r in range(nsub_q):
                for j in range(ATTN_SPAN * nsub_k):
                    rows = slice(r * LANES, (r + 1) * LANES)
                    cols = slice(j * LANES, (j + 1) * LANES)
                    pcols = slice((j % nsub_k) * LANES, (j % nsub_k + 1) * LANES)
                    sub_far = (qmin_sub_ref[b * nsub + qi * nsub_q + r]
                               - kmax_sub_ref[b * nsub + ca * ATTN_SPAN * nsub_k + j]) >= far_dist

                    @pl.when(sub_far)
                    def _():
                        for h in range(ATT_HEADS):
                            bias_ref[h, rows, cols] = jnp.zeros((LANES, LANES), F32)

                    @pl.when(jnp.logical_not(sub_far))
                    def _():
                        half = LANES // 2
                        for hr in range(2):
                            rr = slice(r * LANES + hr * half, r * LANES + (hr + 1) * half)
                            dist = posq_ref[0, rr, :] - posk_ref[0, ca * ATTN_SPAN + j // nsub_k, :, pcols]
                            idx = jnp.clip(dist, 0, REL_MAX_DIST - 1)
                            for h in range(ATT_HEADS):
                                row = jnp.broadcast_to(btab_ref[h:h + 1, :], (half, LANES))
                                bias_ref[h, rr, cols] = jnp.take_along_axis(row, idx, axis=1)

            heads(True, ca, kk, vv)

        return carry

    lax.fori_loop(0, n_att, attend, 0)

    outs = []
    for h in range(ATT_HEADS):
        a = acc_ref[h]
        outs.append(a[:, :ATT_HEAD_DIM] / a[:, ATT_HEAD_DIM:])
    o_ref[0] = jnp.concatenate(outs, axis=1).astype(BF16)


def _attn(p_a13, p_qi3, p_kv3, positions, rel_table, *, bq, bk):
    b, s, _ = p_qi3.shape
    assert bq % LANES == 0 and bk % LANES == 0
    n_keep = min(TOPK_MAX, s // 4)
    assert n_keep <= 2 * LANES
    assert (s // bk) % ATTN_SPAN == 0
    bka = bk * ATTN_SPAN
    nq, nkc, nsub = s // bq, s // bk, s // LANES
    qmin =jnp.min(positions.reshape(b, nq, bq), axis=-1).reshape(-1)
    kmax = jnp.max(positions.reshape(b, nkc, bk), axis=-1).reshape(-1)
    qmin_sub = jnp.min(positions.reshape(b, nsub, LANES), axis=-1).reshape(-1)
    kmax_sub = jnp.max(positions.reshape(b, nsub, LANES), axis=-1).reshape(-1)
    buckets = _t5_bucket_table()
    far_dist = REL_MAX_DIST - 1
    assert np.all(np.diff(buckets) >= 0) and buckets[far_dist] == REL_BUCKETS - 1
    relb = ((rel_table - rel_table[REL_BUCKETS - 1:REL_BUCKETS, :]) * LOG2E).astype(F32)
    btab = jnp.pad(jnp.transpose(relb[buckets, :]), ((0, 8 - ATT_HEADS), (0, 0)))

    grid_spec = pltpu.PrefetchScalarGridSpec(
        num_scalar_prefetch=4,
        grid=(b, nq),
        in_specs=[
            pl.BlockSpec((1, bq, ATT_W), lambda i, t, *_: (i, t, A1_QA // ATT_W)),
            pl.BlockSpec((1, bq, QI_W), lambda i, t, *_: (i, t, 0)),
            pl.BlockSpec((1, s, LANES), lambda i, t, *_: (i, 0, QI_KIW // LANES)),
            pl.BlockSpec(memory_space=pl.ANY),
            pl.BlockSpec((1, bq, 1), lambda i, t, *_: (i, t, 0)),
            pl.BlockSpec((1, nkc, 1, bk), lambda i, t, *_: (i, 0, 0, 0)),
            pl.BlockSpec((8, LANES), lambda i, t, *_: (0, 0)),
        ],
        out_specs=pl.BlockSpec((1, bq, ATT_W), lambda i, t, *_: (i, t, 0)),
        scratch_shapes=[
            pltpu.VMEM((KV_BUFFERS, bka, KV_W), BF16),
            pltpu.SemaphoreType.DMA((KV_BUFFERS,)),
            pltpu.VMEM((nkc, bq, bk), I32),
            pltpu.VMEM((bq, LANES), I32),
            pltpu.VMEM((bq, LANES), I32),
            pltpu.VMEM((bq, LANES), F32),
            pltpu.VMEM((ATT_HEADS, bq, LANES), F32),
            pltpu.VMEM((ATT_HEADS, bq, 2 * ATT_HEAD_DIM), F32),
            pltpu.VMEM((ATT_HEADS, bq, bka), F32),
            pltpu.VMEM((bq, bka), F32),
            pltpu.VMEM((bq, bka), BF16),
            pltpu.VMEM((bq, LANES), F32),
        ],
    )
    kern = functools.partial(_attn_kernel, bq=bq, bk=bk, n_keep=n_keep, seq=s,
                             far_dist=far_dist)
    return pl.pallas_call(
        kern,
        out_shape=jax.ShapeDtypeStruct((b, s, ATT_W), BF16),
        grid_spec=grid_spec,
        compiler_params=_cparams(("parallel", "arbitrary")),
        name="sparse_attn",
    )(qmin, kmax, qmin_sub, kmax_sub,
      p_a13, p_qi3, p_qi3, p_kv3, positions.reshape(b, s, 1), positions.reshape(b, nkc, 1, bk), btab)


def _merge_kernel(or_ref, oa_ref, os_ref, gr_ref, ga_ref, gs_ref, wr_ref, wa_ref, ws_ref, wo_ref, x_ref, o_ref):
    j = pl.program_id(1)

    @pl.when(j == 0)
    def _():
        o_ref[...] = jnp.zeros_like(o_ref)

    def branch(o, w, g):
        return jax.nn.sigmoid(g[0].astype(F32)) * jnp.dot(o[...], w[...], preferred_element_type=F32)

    merged = branch(or_ref, wr_ref, gr_ref) + branch(oa_ref, wa_ref, ga_ref) + branch(os_ref, ws_ref, gs_ref)
    o_ref[...] += jnp.dot(merged.astype(BF16), wo_ref[...], preferred_element_type=F32)

    @pl.when(j == pl.num_programs(1) - 1)
    def _():
        o_ref[...] = x_ref[...] + o_ref[...]


def _merge(o_r, o_a, o_s, p_g, wr, wa, ws, wo, x2, *, tm):
    t, d = x2.shape
    tn = PROJ_TILE
    per_gate = d // tn

    def gspec(g):
        return pl.BlockSpec((1, tm, tn), lambda i, j: (g * per_gate + j, i, 0))

    return pl.pallas_call(
        _merge_kernel,
        out_shape=jax.ShapeDtypeStruct((t, d), F32),
        grid=(t // tm, per_gate),
        in_specs=[
            pl.BlockSpec((tm, RET_V_W), lambda i, j: (i, 0)),
            pl.BlockSpec((tm, ATT_W), lambda i, j: (i, 0)),
            pl.BlockSpec((tm, SG_W), lambda i, j: (i, 0)),
            gspec(0), gspec(1), gspec(2),
            pl.BlockSpec((RET_V_W, tn), lambda i, j: (0, j)),
            pl.BlockSpec((ATT_W, tn), lambda i, j: (0, j)),
            pl.BlockSpec((SG_W, tn), lambda i, j: (0, j)),
            pl.BlockSpec((tn, d), lambda i, j: (j, 0)),
            pl.BlockSpec((tm, d), lambda i, j: (i, 0)),
        ],
        out_specs=pl.BlockSpec((tm, d), lambda i, j: (i, 0)),
        compiler_params=_cparams(("parallel", "arbitrary")),
        name="merge",
    )(o_r, o_a, o_s, p_g, p_g, p_g, wr, wa, ws, wo, x2)


def _split_w_in_kernel(w_ref, wa_ref, wb_ref):
    w = w_ref[0]
    wa_ref[:, :W_A_COLS] = w[:, :W_A_COLS].astype(BF16)
    wa_ref[:, W_A_COLS:] = jnp.zeros((w.shape[0], W_A_PAD - W_A_COLS), BF16)
    wb_ref[...] = w[:, W_A_COLS:].astype(BF16)


def _split_w_in(w_in, layer, d_model, *, tr=64):
    _, rows, cols = w_in.shape
    assert cols == W_A_COLS + 2 * SG_W + 3 * d_model and A1_W + KV_W + QI_W == W_A_PAD
    w_b_cols = cols - W_A_COLS
    assert w_b_cols % PROJ_TILE == 0 and rows % tr == 0
    return pl.pallas_call(
        _split_w_in_kernel,
        out_shape=(jax.ShapeDtypeStruct((rows, W_A_PAD), BF16), jax.ShapeDtypeStruct((rows, w_b_cols), BF16)),
        grid=(rows // tr,),
        in_specs=[pl.BlockSpec((1, tr, cols), lambda i: (layer, i, 0))],
        out_specs=(pl.BlockSpec((tr, W_A_PAD), lambda i: (i, 0)), pl.BlockSpec((tr, w_b_cols), lambda i: (i, 0))),
        compiler_params=_cparams(("parallel",)),
        name="split_w_in",
    )(w_in)


def _pick(n, prefs):
    for p in prefs:
        if n % p == 0:
            return p
    return n


def kernel(x, positions, rel_table, ffn1_norm, ffn1_w_gate, ffn1_w_up, ffn1_w_down, mix_norm, w_in, ret_norm, sg_norm, sg_w, sg_b, w_br_ret, w_br_att, w_br_sg, w_out, ffn2_norm, ffn2_w_gate, ffn2_w_up, ffn2_w_down, final_norm):
    b, s, d = x.shape
    t = b * s
    depth = w_in.shape[0]
    d_ff = ffn1_w_gate.shape[2]
    assert s % RET_CHUNK == 0 and d % PROJ_TILE == 0

    tm = _pick(t, (512, 256, 128))
    tm_proj = _pick(t, (1024, 512, 256, 128))
    tm_ffn = _pick(t, (512, 256, 128))
    tf = _pick(d_ff, (512, 256, 128))
    bq = _pick(s, (256, 128))
    bk = _pick(s, (512, 256, 128))

    cos_t, sin_t = _rope_tables(positions, tm=_pick(s, (2048, 1024, 512, 256, 128)))
    consts = _retention_constants()
    x2 = x.reshape(t, d)
    for l in range(depth):
        x2 = _ffn(x2, ffn1_norm[l], ffn1_w_gate[l].astype(BF16), ffn1_w_up[l].astype(BF16),
                  ffn1_w_down[l].astype(BF16), tm=tm_ffn, tf=tf)
        p_a1, p_kv, p_qi, p_sg, p_g = _proj(x2, mix_norm[l], *_split_w_in(w_in, l, d), tm=tm_proj)
        p_a13 = p_a1.reshape(b, s, -1)
        o_r, o_s = _local(p_a13, p_sg.reshape(b, s, -1), cos_t, sin_t, ret_norm[l], sg_norm[l], sg_w[l], sg_b[l],
                          consts)
        o_a = _attn(p_a13, p_qi.reshape(b, s, -1), p_kv.reshape(b, s, -1), positions, rel_table, bq=bq, bk=bk)
        x2 = _merge(o_r.reshape(t, -1), o_a.reshape(t, -1), o_s.reshape(t, -1), p_g,
                    w_br_ret[l].astype(BF16), w_br_att[l].astype(BF16), w_br_sg[l].astype(BF16),
                    w_out[l].astype(BF16), x2, tm=tm)
        x2 = _ffn(x2, ffn2_norm[l], ffn2_w_gate[l].astype(BF16), ffn2_w_up[l].astype(BF16),
                  ffn2_w_down[l].astype(BF16), final_norm if l == depth - 1 else None, tm=tm_ffn, tf=tf)
    return x2.reshape(b, s, d)
```

```python
import functools
import math

import numpy as np
import jax
import jax.numpy as jnp
from jax import lax
from jax.experimental import pallas as pl
from jax.experimental.pallas import tpu as pltpu

F32 = jnp.float32
BF16 = jnp.bfloat16
I32 = jnp.int32

RET_HEADS, RET_QK_DIM, RET_V_DIM, RET_CHUNK = 6, 64, 128, 128
ATT_HEADS, ATT_HEAD_DIM = 6, 128
IDX_HEADS, IDX_DIM = 4, 64
TOPK_MAX = 256
SG_GROUPS, SG_GROUP_DIM, SG_CHUNK = 4, 128, 128
REL_BUCKETS, REL_MAX_DIST = 32, 128
ROPE_BASE = 10000.0
EPS = 1e-6

RET_QK_W = RET_HEADS * RET_QK_DIM
RET_V_W = RET_HEADS * RET_V_DIM
ATT_W = ATT_HEADS * ATT_HEAD_DIM
IDX_Q_W = IDX_HEADS * IDX_DIM
SG_W = SG_GROUPS * SG_GROUP_DIM

LANES = 128
COUNT_ROWS = 64
KV_BUFFERS = 3
SEARCH_PERIOD = 6
ATTN_SPAN = 1
SOFTMAX_ROWS = 64
VMEM_LIMIT_BYTES = 56 * 1024 * 1024

PROJ_TILE = 512
W_A_COLS = 2 * RET_QK_W + 2 * RET_V_W + 3 * ATT_W + IDX_Q_W + IDX_DIM + IDX_HEADS
W_A_PAD = 5120
A1_QR, A1_KR, A1_VR, A1_GR, A1_QA = 0, 384, 768, 1536, 2304
A1_W = 3072
KV_K, KV_V = 0, 768
KV_W = 1536
QI_Q, QI_KIW = 0, 256
QI_W = 512
SG_U, SG_V = 0, 512
SG_PW = 1024

INT_MIN = np.int32(-2 ** 31)
INT_MAX = np.int32(2 ** 31 - 1)
NEG_INF = float("-inf")
LOG2E = math.log2(math.e)


def _cparams(sem):
    return pltpu.CompilerParams(dimension_semantics=sem, vmem_limit_bytes=VMEM_LIMIT_BYTES)


def _rms(x, g):
    return x * lax.rsqrt(jnp.mean(x * x, axis=-1, keepdims=True) + EPS) * g


def _ffn_kernel(x_ref, g_ref, wg_ref, wu_ref, wd_ref, *rest, final_norm):
    if final_norm:
        fg_ref, o_ref, xn_ref = rest
    else:
        o_ref, xn_ref = rest
    j = pl.program_id(1)

    @pl.when(j == 0)
    def _():
        xn_ref[...] = _rms(x_ref[...], g_ref[...]).astype(BF16)
        o_ref[...] = jnp.zeros_like(o_ref)

    xn = xn_ref[...]
    a = jnp.dot(xn, wg_ref[...], preferred_element_type=F32)
    b = jnp.dot(xn, wu_ref[...], preferred_element_type=F32)
    h = (a * jax.nn.sigmoid(a) * b).astype(BF16)
    o_ref[...] += jnp.dot(h, wd_ref[...], preferred_element_type=F32)

    @pl.when(j == pl.num_programs(1) - 1)
    def _():
        y = x_ref[...] + 0.5 * o_ref[...]
        if final_norm:
            y = _rms(y, fg_ref[...])
        o_ref[...] = y


def _ffn(x2, gain, wg, wu, wd, final_gain=None, *, tm, tf):
    t, d = x2.shape
    f = wg.shape[1]
    final_norm = final_gain is not None
    in_specs = [
        pl.BlockSpec((tm, d), lambda i, j: (i, 0)),
        pl.BlockSpec((1, d), lambda i, j: (0, 0)),
        pl.BlockSpec((d, tf), lambda i, j: (0, j)),
        pl.BlockSpec((d, tf), lambda i, j: (0, j)),
        pl.BlockSpec((tf, d), lambda i, j: (j, 0)),
    ]
    args = [x2, gain.reshape(1, d), wg, wu, wd]
    if final_norm:
        in_specs.append(pl.BlockSpec((1, d), lambda i, j: (0, 0)))
        args.append(final_gain.reshape(1, d))
    return pl.pallas_call(
        functools.partial(_ffn_kernel, final_norm=final_norm),
        out_shape=jax.ShapeDtypeStruct((t, d), F32),
        grid=(t // tm, f // tf),
        in_specs=in_specs,
        out_specs=pl.BlockSpec((tm, d), lambda i, j: (i, 0)),
        scratch_shapes=[pltpu.VMEM((tm, d), BF16)],
        compiler_params=_cparams(("parallel", "arbitrary")),
        name="ffn",
    )(*args)


def _proj_kernel(x_ref, g_ref, wa_ref, wb_ref, oa1_ref, okv_ref, oqi_ref, osg_ref, og_ref, xn_ref, *, ends):
    j = pl.program_id(1)
    e_a1, e_kv, e_qi, e_sg = ends

    @pl.when(j == 0)
    def _():
        xn_ref[...] = _rms(x_ref[...], g_ref[...]).astype(BF16)

    def tile(w_ref):
        return jnp.dot(xn_ref[...], w_ref[...], preferred_element_type=F32).astype(BF16)

    @pl.when(j < e_a1)
    def _():
        oa1_ref[...] = tile(wa_ref)

    @pl.when(jnp.logical_and(j >= e_a1, j < e_kv))
    def _():
        okv_ref[...] = tile(wa_ref)

    @pl.when(jnp.logical_and(j >= e_kv, j < e_qi))
    def _():
        oqi_ref[...] = tile(wa_ref)

    @pl.when(jnp.logical_and(j >= e_qi, j < e_sg))
    def _():
        osg_ref[...] = tile(wb_ref)

    @pl.when(j >= e_sg)
    def _():
        og_ref[0] = tile(wb_ref)


def _proj(x2, gain, w_a, w_b, *, tm):
    t, d = x2.shape
    tn = PROJ_TILE
    n_a, n_b = w_a.shape[1] // tn, w_b.shape[1] // tn
    e_a1 = A1_W // tn
    e_kv = e_a1 + KV_W // tn
    e_qi = e_kv + QI_W // tn
    e_sg = e_qi + SG_PW // tn
    assert e_qi == n_a
    t_g = n_a + n_b - e_sg

    def clampspec(first, count):
        return pl.BlockSpec((tm, tn), lambda i, j: (i, jnp.clip(j - first, 0, count - 1)))

    return pl.pallas_call(
        functools.partial(_proj_kernel, ends=(e_a1, e_kv, e_qi, e_sg)),
        out_shape=(jax.ShapeDtypeStruct((t, A1_W), BF16),
                   jax.ShapeDtypeStruct((t, KV_W), BF16),
                   jax.ShapeDtypeStruct((t, QI_W), BF16),
                   jax.ShapeDtypeStruct((t, SG_PW), BF16),
                   jax.ShapeDtypeStruct((t_g, t, tn), BF16)),
        grid=(t // tm, n_a + n_b),
        in_specs=[
            pl.BlockSpec((tm, d), lambda i, j: (i, 0)),
            pl.BlockSpec((1, d), lambda i, j: (0, 0)),
            pl.BlockSpec((d, tn), lambda i, j: (0, jnp.clip(j, 0, n_a - 1))),
            pl.BlockSpec((d, tn), lambda i, j: (0, jnp.clip(j - n_a, 0, n_b - 1))),
        ],
        out_specs=(clampspec(0, e_a1), clampspec(e_a1, e_kv - e_a1), clampspec(e_kv, e_qi - e_kv),
                   clampspec(e_qi, e_sg - e_qi),
                   pl.BlockSpec((1, tm, tn), lambda i, j: (jnp.clip(j - e_sg, 0, t_g - 1), i, 0))),
        scratch_shapes=[pltpu.VMEM((tm, d), BF16)],
        compiler_params=_cparams(("arbitrary", "arbitrary")),
        name="proj",
    )(x2, gain.reshape(1, d), w_a, w_b)


def _rope_table_kernel(pos_ref, invf_ref, sign_ref, c_ref, s_ref):
    ang = pos_ref[0].astype(F32) * invf_ref[...]
    c_ref[0] = jnp.cos(ang)
    s_ref[0] = jnp.sin(ang) * sign_ref[...]


def _rope_tables(positions, *, tm):
    b, s = positions.shape
    half = RET_QK_DIM // 2
    inv_freq = ROPE_BASE ** (-jnp.arange(0, RET_QK_DIM, 2, dtype=F32) / RET_QK_DIM)
    lane = np.arange(LANES)
    invf = inv_freq[lane % half].reshape(1, LANES)
    sign = jnp.asarray(np.where(lane % RET_QK_DIM < half, -1.0, 1.0), F32).reshape(1, LANES)
    out = jax.ShapeDtypeStruct((b, s, LANES), F32)
    return pl.pallas_call(
        _rope_table_kernel,
        out_shape=(out, out),
        grid=(b, s // tm),
        in_specs=[
            pl.BlockSpec((1, tm, 1), lambda i, j: (i, j, 0)),
            pl.BlockSpec((1, LANES), lambda i, j: (0, 0)),
            pl.BlockSpec((1, LANES), lambda i, j: (0, 0)),
        ],
        out_specs=(pl.BlockSpec((1, tm, LANES), lambda i, j: (i, j, 0)),
                   pl.BlockSpec((1, tm, LANES), lambda i, j: (i, j, 0))),
        compiler_params=_cparams(("parallel", "parallel")),
        name="rope_tables",
    )(positions.reshape(b, s, 1), invf, sign)


def _retention_constants():
    c = RET_CHUNK
    h = np.arange(RET_HEADS, dtype=np.float64)
    log_g = np.log(1.0 - 2.0 ** (-5.0 - h))
    idx = np.arange(c, dtype=np.float64)
    rel = idx[:, None] - idx[None, :]
    dmat = np.where(rel >= 0, np.exp(np.maximum(rel, 0.0) * log_g[:, None, None]), 0.0)
    k_decay = np.exp((c - 1 - idx)[None, :] * log_g[:, None])
    q_decay = np.exp((idx + 1)[None, :] * log_g[:, None])
    chunk_decay = np.exp(c * log_g)
    kd = np.repeat(k_decay.T, RET_QK_DIM, axis=1)
    qd = np.repeat(q_decay.T, RET_QK_DIM, axis=1) * RET_QK_DIM ** -0.5
    lane = np.arange(RET_QK_W)
    src = (lane // RET_QK_DIM) * RET_QK_DIM + (lane % RET_QK_DIM + RET_QK_DIM // 2) % RET_QK_DIM
    perm = np.zeros((RET_QK_W, RET_QK_W), np.float32)
    perm[src, lane] = 1.0
    return (jnp.asarray(dmat, F32), jnp.asarray(qd, F32), jnp.asarray(kd, F32),
            [float(np.float32(v)) for v in chunk_decay], jnp.asarray(perm, BF16))


def _gelu_tanh(x):
    return 0.5 * x * (1.0 + jnp.tanh(math.sqrt(2.0 / math.pi) * (x + 0.044715 * (x * x * x))))


def _local_kernel(p_ref, psg_ref, cos_ref, sin_ref,
                  perm_ref, dmat_ref, qd_ref, kd_ref, rn_ref, sn_ref, sgw_ref, sgb_ref,
                  or_ref, os_ref, state_ref, *, chunk_decay):
    @pl.when(pl.program_id(1) == 0)
    def _():
        state_ref[...] = jnp.zeros_like(state_ref)

    n_rep = RET_QK_W // LANES
    cos3 = jnp.concatenate([cos_ref[0]] * n_rep, axis=1)
    sin3 = jnp.concatenate([sin_ref[0]] * n_rep, axis=1)
    q = p_ref[0, :, A1_QR:A1_QR + RET_QK_W]
    k = p_ref[0, :, A1_KR:A1_KR + RET_QK_W]
    perm = perm_ref[...]
    qr = q.astype(F32) * cos3 + jnp.dot(q, perm, preferred_element_type=F32) * sin3
    kr = k.astype(F32) * cos3 + jnp.dot(k, perm, preferred_element_type=F32) * sin3
    q_in = (qr * (RET_QK_DIM ** -0.5)).astype(BF16)
    q_cr = (qr * qd_ref[...]).astype(BF16)
    k_b = kr.astype(BF16)
    k_dec_t = jnp.transpose(kr * kd_ref[...]).astype(BF16)
    v = p_ref[0, :, A1_VR:A1_VR + RET_V_W]
    outs = []
    for h in range(RET_HEADS):
        qs = slice(h * RET_QK_DIM, (h + 1) * RET_QK_DIM)
        vh = v[:, h * RET_V_DIM:(h + 1) * RET_V_DIM]
        a = lax.dot_general(q_in[:, qs], k_b[:, qs], (((1,), (1,)), ((), ())),
                            preferred_element_type=F32) * dmat_ref[h]
        prev = state_ref[h]
        o = (jnp.dot(a.astype(BF16), vh, preferred_element_type=F32)
             + jnp.dot(q_cr[:, qs], prev.astype(BF16), preferred_element_type=F32))
        state_ref[h] = prev * chunk_decay[h] + jnp.dot(k_dec_t[qs, :], vh, preferred_element_type=F32)
        mu = jnp.mean(o, axis=-1, keepdims=True)
        d = o - mu
        var = jnp.mean(d * d, axis=-1, keepdims=True)
        outs.append(d * lax.rsqrt(var + EPS))
    g = p_ref[0, :, A1_GR:A1_GR + RET_V_W].astype(F32)
    y = jnp.concatenate(outs, axis=1) * rn_ref[...]
    or_ref[0] = (y * (g * jax.nn.sigmoid(g))).astype(BF16)

    u = _gelu_tanh(psg_ref[0, :, SG_U:SG_U + SG_W].astype(F32))
    vn = _rms(_gelu_tanh(psg_ref[0, :, SG_V:SG_V + SG_W].astype(F32)), sn_ref[...])
    vn_b = vn.astype(BF16)
    c = SG_CHUNK
    tril = lax.broadcasted_iota(I32, (c, c), 0) >= lax.broadcasted_iota(I32, (c, c), 1)
    bias = sgb_ref[...]
    mixed = []
    for gi in range(SG_GROUPS):
        w = jnp.where(tril, sgw_ref[gi], jnp.zeros((), BF16))
        m = jnp.dot(w, vn_b[:, gi * SG_GROUP_DIM:(gi + 1) * SG_GROUP_DIM], preferred_element_type=F32)
        mixed.append(m + bias[:, gi:gi + 1])
    os_ref[0] = (u * jnp.concatenate(mixed, axis=1)).astype(BF16)


def _local(p_a13, p_sg3, cos_t, sin_t, ret_norm, sg_norm, sg_w, sg_b, consts):
    b, s, _ = p_a13.shape
    c = RET_CHUNK
    dmat, qd, kd, chunk_decay, perm = consts

    def whole(arr):
        nd = arr.ndim
        return pl.BlockSpec(arr.shape, lambda i, j: (0,) * nd)

    rn = ret_norm.reshape(1, RET_V_W)
    sn = sg_norm.reshape(1, SG_W)
    sgw = sg_w.astype(BF16)
    sgb_t = jnp.transpose(sg_b)
    tab = pl.BlockSpec((1, c, LANES), lambda i, j: (i, j, 0))
    return pl.pallas_call(
        functools.partial(_local_kernel, chunk_decay=chunk_decay),
        out_shape=(jax.ShapeDtypeStruct((b, s, RET_V_W), BF16),
                   jax.ShapeDtypeStruct((b, s, SG_W), BF16)),
        grid=(b, s // c),
        in_specs=[
            pl.BlockSpec((1, c, A1_W), lambda i, j: (i, j, 0)),
            pl.BlockSpec((1, c, SG_PW), lambda i, j: (i, j, 0)),
            tab, tab,
            whole(perm), whole(dmat), whole(qd), whole(kd), whole(rn), whole(sn), whole(sgw), whole(sgb_t),
        ],
        out_specs=(pl.BlockSpec((1, c, RET_V_W), lambda i, j: (i, j, 0)),
                   pl.BlockSpec((1, c, SG_W), lambda i, j: (i, j, 0))),
        scratch_shapes=[pltpu.VMEM((RET_HEADS, RET_QK_DIM, RET_V_DIM), F32)],
        compiler_params=_cparams(("parallel", "arbitrary")),
        name="local_mixers",
    )(p_a13, p_sg3, cos_t, sin_t, perm, dmat, qd, kd, rn, sn, sgw, sgb_t)


def _t5_bucket_table():
    max_exact = REL_BUCKETS // 2
    d = np.arange(REL_MAX_DIST)
    df = np.maximum(d, 1).astype(np.float32)
    large = max_exact + (np.log(df / max_exact) / np.float32(math.log(REL_MAX_DIST / max_exact))
                         * (REL_BUCKETS - max_exact)).astype(np.int32)
    large = np.minimum(large, REL_BUCKETS - 1)
    return np.where(d < max_exact, d, large)


def _attn_kernel(qmin_ref, kmax_ref, qmin_sub_ref, kmax_sub_ref,
                 qa_ref, pqi_ref, kiw_ref, kv_hbm, posq_ref, posk_ref, btab_ref,
                 o_ref, kv_buf, kv_sem, keys_ref, thr_ref, aux_ref, cnt_ref, m_ref, acc_ref, bias_ref,
                 s_ref, p_ref, alpha_ref,
                 *, bq, bk, n_keep, seq, far_dist):
    b = pl.program_id(0)
    qi = pl.program_id(1)
    n_chunks = (qi * bq + bq - 1) // bk + 1
    nlc = bk // LANES
    bka = bk * ATTN_SPAN
    n_att = (n_chunks + ATTN_SPAN - 1) // ATTN_SPAN

    def kv_copy(ca, slot):
        return pltpu.make_async_copy(kv_hbm.at[b, pl.ds(ca * bka, bka), :], kv_buf.at[slot], kv_sem.at[slot])

    for i in range(KV_BUFFERS - 1):
        @pl.when(i < n_att)
        def _():
            kv_copy(i, i).start()
    nsub_q, nsub_k = bq // LANES, bk // LANES
    k_f = float(n_keep)

    def tile_lanes(x, n=None):
        return jnp.concatenate([x] * (nlc if n is None else n), axis=1)

    def lane_chunk(x, j):
        return x[:, j * LANES:(j + 1) * LANES]

    def _select():
        qiv = pqi_ref[0, :, QI_Q:QI_Q + IDX_Q_W]
        zpad = jnp.zeros((bq, LANES - IDX_DIM), BF16)
        q_heads = [jnp.concatenate([qiv[:, h * IDX_DIM:(h + 1) * IDX_DIM], zpad], axis=1)
                   for h in range(IDX_HEADS)]
        kiw_q = pqi_ref[0, :, QI_KIW:QI_KIW + LANES]
        w = kiw_q[:, IDX_DIM:IDX_DIM + IDX_HEADS].astype(F32)
        w_bc = [jnp.broadcast_to(w[:, h:h + 1], (bq, bk)) for h in range(IDX_HEADS)]
        row_t = qi * bq + lax.broadcasted_iota(I32, (bq, bk), 0)
        lane_s = lax.broadcasted_iota(I32, (bq, bk), 1)

        def score_body(cc, carry):
            off = pl.multiple_of(cc * bk, bk)
            kc = kiw_ref[0, pl.ds(off, bk), :]
            sc = jnp.zeros((bq, bk), F32)
            for h in range(IDX_HEADS):
                sh = lax.dot_general(q_heads[h], kc, (((1,), (1,)), ((), ())), preferred_element_type=F32)
                sc = sc + w_bc[h] * jnp.maximum(sh, 0.0)
            sc = jnp.where(sc == 0.0, 0.0, sc)
            bits = lax.bitcast_convert_type(sc, I32)
            key = jnp.where(bits < 0, bits ^ INT_MAX, bits)
            key = jnp.where(cc * bk + lane_s <= row_t, key, INT_MIN)
            keys_ref[cc] = key
            for r in range(bq // COUNT_ROWS):
                rows = slice(r * COUNT_ROWS, (r + 1) * COUNT_ROWS)
                m1 = thr_ref[rows, :]
                m2 = aux_ref[rows, :]
                for j in range(nlc):
                    x = key[rows, j * LANES:(j + 1) * LANES]
                    m2 = jnp.maximum(m2, jnp.minimum(m1, x))
                    m1 = jnp.maximum(m1, x)
                thr_ref[rows, :] = m1
                aux_ref[rows, :] = m2
            return carry

        thr_ref[...] = jnp.full((bq, LANES), INT_MIN, I32)
        aux_ref[...] = jnp.full((bq, LANES), INT_MIN, I32)
        lax.fori_loop(0, n_chunks, score_body, 0)
        top1, top2 = thr_ref[...], aux_ref[...]

        nt = bq // LANES

        def to_rows(c):
            return jnp.concatenate(
                [jnp.transpose(jnp.broadcast_to(c[k:k + 1, :], (LANES, LANES))) for k in range(nt)], axis=0)

        def from_rows(x, op):
            return jnp.concatenate(
                [op(jnp.transpose(x[k * LANES:(k + 1) * LANES, :]), axis=0, keepdims=True) for k in range(nt)], axis=0)

        def count_ge(thr):
            thr_ref[...] = to_rows(thr)
            cnt_ref[...] = jnp.zeros((bq, LANES), F32)

            def body(cc, carry):
                for r in range(bq // COUNT_ROWS):
                    rows = pl.ds(r * COUNT_ROWS, COUNT_ROWS)
                    t = keys_ref[cc, rows, :]
                    th = thr_ref[rows, :]
                    acc = cnt_ref[rows, :]
                    for j in range(nlc):
                        acc = acc + jnp.where(lane_chunk(t, j) >= th, 1.0, 0.0)
                    cnt_ref[rows, :] = acc
                return carry

            lax.fori_loop(0, n_chunks, body, 0)
            return from_rows(cnt_ref[...], jnp.sum)

        def key_value(k):
            return lax.bitcast_convert_type(jnp.where(k < 0, k ^ INT_MAX, k), F32)

        def value_key(v):
            bits = lax.bitcast_convert_type(v, I32)
            return jnp.where(bits < 0, bits ^ INT_MAX, bits)

        lo0 = from_rows(top2, jnp.min)
        top = from_rows(top1, jnp.max)
        hi0 = jnp.where(top == INT_MAX, INT_MAX, top + 1)
        log2_k = math.log2(k_f)

        def excess(cnt):
            return jnp.log2(jnp.maximum(cnt, 0.5)) - log2_k

        def resolved(lo, hi, c_lo):
            return jnp.logical_or(hi - 1 <= lo, c_lo == k_f)

        def probe(mid, lo, hi, c_lo, c_hi, f_lo, f_hi, last):
            cnt = count_ge(mid)
            ok = cnt >= k_f
            f_new = excess(cnt)
            f_hi = jnp.where(jnp.logical_and(ok, last > 0.0), 0.5 * f_hi, f_hi)
            f_lo = jnp.where(jnp.logical_and(jnp.logical_not(ok), last < 0.0), 0.5 * f_lo, f_lo)
            return (jnp.where(ok, mid, lo), jnp.where(ok, hi, mid), jnp.where(ok, cnt, c_lo), jnp.where(ok, c_hi, cnt),
                    jnp.where(ok, f_new, f_lo), jnp.where(ok, f_hi, f_new), jnp.where(ok, 1.0, -1.0))

        def search_step(state):
            it, lo, hi, c_lo, c_hi, f_lo, f_hi, last, _ = state
            done = resolved(lo, hi, c_lo)
            mid_b = (lo & hi) + ((lo ^ hi) >> 1)
            v_lo = key_value(lo)
            mid_i = jnp.clip(value_key(v_lo + f_lo / (f_lo - f_hi) * (key_value(hi) - v_lo)), lo + 1, hi - 1)
            mid = jnp.where(it % SEARCH_PERIOD == SEARCH_PERIOD - 1, mid_b, mid_i)
            mid = jnp.where(done, lo, mid)
            old = (lo, hi, c_lo, c_hi, f_lo, f_hi, last)
            new = tuple(jnp.where(done, o, u) for o, u in zip(old, probe(mid, *old)))
            pending = jnp.max(jnp.where(resolved(new[0], new[1], new[2]), 0.0, 1.0))
            return (it + 1,) + new + (pending,)

        c_nominal = jnp.full((nt, LANES), 4.0 * k_f, F32)
        zero_c = jnp.zeros((nt, LANES), F32)
        state = (lo0, hi0, c_nominal, zero_c, excess(c_nominal), excess(zero_c), zero_c)
        for z in (0, 1):
            lo, hi = state[0], state[1]
            zk = jnp.full((nt, LANES), z, I32)
            inside = jnp.logical_and(jnp.logical_and(zk > lo, zk < hi), jnp.logical_not(resolved(lo, hi, state[2])))
            new = probe(jnp.where(inside, zk, lo), *state)
            state = tuple(jnp.where(inside, upd, old) for old, upd in zip(state, new))
        pending0 = jnp.max(jnp.where(resolved(state[0], state[1], state[2]), 0.0, 1.0))
        state = lax.while_loop(lambda st: st[8] > 0.0, search_step, (jnp.int32(0),) + state + (pending0,))
        tau_c, hi_c, c_hi = state[1], state[2], state[4]
        need_c = jnp.where(hi_c - 1 <= tau_c, k_f - c_hi, float(seq))
        need = to_rows(jnp.where(tau_c == INT_MIN, 0.0, need_c))
        tau = to_rows(tau_c)
        upper = lax.broadcasted_iota(I32, (LANES, LANES), 0) <= lax.broadcasted_iota(I32, (LANES, LANES), 1)
        scan = jnp.concatenate([jnp.where(upper, 1.0, 0.0), jnp.ones((LANES, LANES), F32)], axis=1).astype(BF16)
        cnt_ref[...] = jnp.zeros((bq, LANES), F32)

        def build_mask(cc, carry):
            t = keys_ref[cc]
            cols = []
            seen = cnt_ref[...]
            for j in range(nlc):
                tj = lane_chunk(t, j)
                tie = tj == tau
                pt = jnp.dot(jnp.where(tie, 1.0, 0.0).astype(BF16), scan, preferred_element_type=F32)
                keep_tie = jnp.logical_and(tie, seen + pt[:, :LANES] <= need)
                cols.append(jnp.where(jnp.logical_or(tj > tau, keep_tie), 0.0, NEG_INF))
                seen = seen + pt[:, LANES:]
            cnt_ref[...] = seen
            keys_ref[cc] = lax.bitcast_convert_type(jnp.concatenate(cols, axis=1), I32)
            return carry

        lax.fori_loop(0, n_chunks, build_mask, 0)

        def mask_out(cc, carry):
            keys_ref[cc] = lax.bitcast_convert_type(jnp.full((bq, bk), NEG_INF, F32), I32)
            return carry

        lax.fori_loop(n_chunks, n_att * ATTN_SPAN, mask_out, 0)

        m_ref[...] = jnp.full(m_ref.shape, NEG_INF, F32)
        acc_ref[...] = jnp.zeros_like(acc_ref)

    _select()

    c1 = (ATT_HEAD_DIM ** -0.5) * LOG2E
    q = qa_ref[0]
    ones_v = jnp.ones((bka, ATT_HEAD_DIM), BF16)

    def heads(near, ca, kk, vv):
        for h in range(ATT_HEADS):
            hs = slice(h * ATT_HEAD_DIM, (h + 1) * ATT_HEAD_DIM)
            s_ref[...] = lax.dot_general(q[:, hs], kk[:, hs], (((1,), (1,)), ((), ())),
                                         preferred_element_type=F32)
            for r in range(bq // SOFTMAX_ROWS):
                rows = slice(r * SOFTMAX_ROWS, (r + 1) * SOFTMAX_ROWS)
                mask = jnp.concatenate([keys_ref[ca * ATTN_SPAN + i, rows, :] for i in range(ATTN_SPAN)], axis=1)
                t = s_ref[rows, :] * c1 + lax.bitcast_convert_type(mask, F32)
                if near:
                    t = t + bias_ref[h, rows, :]
                m_old = m_ref[h, rows, :]
                m_new = jnp.maximum(m_old, jnp.max(t, axis=1, keepdims=True))
                m_safe = jnp.where(m_new == NEG_INF, 0.0, m_new)
                alpha_ref[rows, :] = jnp.exp2(m_old - m_safe)
                p_ref[rows, :] = jnp.exp2(t - tile_lanes(m_safe, bka // LANES)).astype(BF16)
                m_ref[h, rows, :] = m_new
            v_aug = jnp.concatenate([vv[:, hs], ones_v], axis=1)
            acc_ref[h] = (tile_lanes(alpha_ref[...], 2) * acc_ref[h]
                          + jnp.dot(p_ref[...], v_aug, preferred_element_type=F32))

    def attend(ca, carry):
        slot = ca % KV_BUFFERS
        nxt = ca + (KV_BUFFERS - 1)

        @pl.when(nxt < n_att)
        def _():
            kv_copy(nxt, nxt % KV_BUFFERS).start()

        kv_copy(ca, slot).wait()
        kk = kv_buf[slot, :, KV_K:KV_K + ATT_W]
        vv = kv_buf[slot, :, KV_V:KV_V + ATT_W]
        kmax = kmax_ref[b * (seq // bk) + ca * ATTN_SPAN]
        for i in range(1, ATTN_SPAN):
            kmax = jnp.maximum(kmax, kmax_ref[b * (seq // bk) + ca * ATTN_SPAN + i])
        far = qmin_ref[b * (seq // bq) + qi] - kmax >= far_dist

        @pl.when(far)
        def _():
            heads(False, ca, kk, vv)

        @pl.when(jnp.logical_not(far))
        def _():
            nsub = seq // LANES
            for r in range(nsub_q):
                for j in range(ATTN_SPAN * nsub_k):
                    rows = slice(r * LANES, (r + 1) * LANES)
                    cols = slice(j * LANES, (j + 1) * LANES)
                    pcols = slice((j % nsub_k) * LANES, (j % nsub_k + 1) * LANES)
                    sub_far = (qmin_sub_ref[b * nsub + qi * nsub_q + r]
                               - kmax_sub_ref[b * nsub + ca * ATTN_SPAN * nsub_k + j]) >= far_dist

                    @pl.when(sub_far)
                    def _():
                        for h in range(ATT_HEADS):
                            bias_ref[h, rows, cols] = jnp.zeros((LANES, LANES), F32)

                    @pl.when(jnp.logical_not(sub_far))
                    def _():
                        half = LANES // 2
                        for hr in range(2):
                            rr = slice(r * LANES + hr * half, r * LANES + (hr + 1) * half)
                            dist = posq_ref[0, rr, :] - posk_ref[0, ca * ATTN_SPAN + j // nsub_k, :, pcols]
                            idx = jnp.clip(dist, 0, REL_MAX_DIST - 1)
                            for h in range(ATT_HEADS):
                                row = jnp.broadcast_to(btab_ref[h:h + 1, :], (half, LANES))
                                bias_ref[h, rr, cols] = jnp.take_along_axis(row, idx, axis=1)

            heads(True, ca, kk, vv)

        return carry

    lax.fori_loop(0, n_att, attend, 0)

    outs = []
    for h in range(ATT_HEADS):
        a = acc_ref[h]
        outs.append(a[:, :ATT_HEAD_DIM] / a[:, ATT_HEAD_DIM:])
    o_ref[0] = jnp.concatenate(outs, axis=1).astype(BF16)


def _attn(p_a13, p_qi3, p_kv3, positions, rel_table, *, bq, bk):
    b, s, _ = p_qi3.shape
    assert bq % LANES == 0 and bk % LANES == 0
    n_keep = min(TOPK_MAX, s // 4)
    assert n_keep <= 2 * LANES
    assert (s // bk) % ATTN_SPAN == 0
    bka = bk * ATTN_SPAN
    nq, nkc, nsub = s // bq, s // bk, s // LANES
    qmin =jnp.min(positions.reshape(b, nq, bq), axis=-1).reshape(-1)
    kmax = jnp.max(positions.reshape(b, nkc, bk), axis=-1).reshape(-1)
    qmin_sub = jnp.min(positions.reshape(b, nsub, LANES), axis=-1).reshape(-1)
    kmax_sub = jnp.max(positions.reshape(b, nsub, LANES), axis=-1).reshape(-1)
    buckets = _t5_bucket_table()
    far_dist = REL_MAX_DIST - 1
    assert np.all(np.diff(buckets) >= 0) and buckets[far_dist] == REL_BUCKETS - 1
    relb = ((rel_table - rel_table[REL_BUCKETS - 1:REL_BUCKETS, :]) * LOG2E).astype(F32)
    btab = jnp.pad(jnp.transpose(relb[buckets, :]), ((0, 8 - ATT_HEADS), (0, 0)))

    grid_spec = pltpu.PrefetchScalarGridSpec(
        num_scalar_prefetch=4,
        grid=(b, nq),
        in_specs=[
            pl.BlockSpec((1, bq, ATT_W), lambda i, t, *_: (i, t, A1_QA // ATT_W)),
            pl.BlockSpec((1, bq, QI_W), lambda i, t, *_: (i, t, 0)),
            pl.BlockSpec((1, s, LANES), lambda i, t, *_: (i, 0, QI_KIW // LANES)),
            pl.BlockSpec(memory_space=pl.ANY),
            pl.BlockSpec((1, bq, 1), lambda i, t, *_: (i, t, 0)),
            pl.BlockSpec((1, nkc, 1, bk), lambda i, t, *_: (i, 0, 0, 0)),
            pl.BlockSpec((8, LANES), lambda i, t, *_: (0, 0)),
        ],
        out_specs=pl.BlockSpec((1, bq, ATT_W), lambda i, t, *_: (i, t, 0)),
        scratch_shapes=[
            pltpu.VMEM((KV_BUFFERS, bka, KV_W), BF16),
            pltpu.SemaphoreType.DMA((KV_BUFFERS,)),
            pltpu.VMEM((nkc, bq, bk), I32),
            pltpu.VMEM((bq, LANES), I32),
            pltpu.VMEM((bq, LANES), I32),
            pltpu.VMEM((bq, LANES), F32),
            pltpu.VMEM((ATT_HEADS, bq, LANES), F32),
            pltpu.VMEM((ATT_HEADS, bq, 2 * ATT_HEAD_DIM), F32),
            pltpu.VMEM((ATT_HEADS, bq, bka), F32),
            pltpu.VMEM((bq, bka), F32),
            pltpu.VMEM((bq, bka), BF16),
            pltpu.VMEM((bq, LANES), F32),
        ],
    )
    kern = functools.partial(_attn_kernel, bq=bq, bk=bk, n_keep=n_keep, seq=s,
                             far_dist=far_dist)
    return pl.pallas_call(
        kern,
        out_shape=jax.ShapeDtypeStruct((b, s, ATT_W), BF16),
        grid_spec=grid_spec,
        compiler_params=_cparams(("parallel", "arbitrary")),
        name="sparse_attn",
    )(qmin, kmax, qmin_sub, kmax_sub,
      p_a13, p_qi3, p_qi3, p_kv3, positions.reshape(b, s, 1), positions.reshape(b, nkc, 1, bk), btab)


def _merge_kernel(or_ref, oa_ref, os_ref, gr_ref, ga_ref, gs_ref, wr_ref, wa_ref, ws_ref, wo_ref, x_ref, o_ref):
    j = pl.program_id(1)

    @pl.when(j == 0)
    def _():
        o_ref[...] = jnp.zeros_like(o_ref)

    def branch(o, w, g):
        return jax.nn.sigmoid(g[0].astype(F32)) * jnp.dot(o[...], w[...], preferred_element_type=F32)

    merged = branch(or_ref, wr_ref, gr_ref) + branch(oa_ref, wa_ref, ga_ref) + branch(os_ref, ws_ref, gs_ref)
    o_ref[...] += jnp.dot(merged.astype(BF16), wo_ref[...], preferred_element_type=F32)

    @pl.when(j == pl.num_programs(1) - 1)
    def _():
        o_ref[...] = x_ref[...] + o_ref[...]


def _merge(o_r, o_a, o_s, p_g, wr, wa, ws, wo, x2, *, tm):
    t, d = x2.shape
    tn = PROJ_TILE
    per_gate = d // tn

    def gspec(g):
        return pl.BlockSpec((1, tm, tn), lambda i, j: (g * per_gate + j, i, 0))

    return pl.pallas_call(
        _merge_kernel,
        out_shape=jax.ShapeDtypeStruct((t, d), F32),
        grid=(t // tm, per_gate),
        in_specs=[
            pl.BlockSpec((tm, RET_V_W), lambda i, j: (i, 0)),
            pl.BlockSpec((tm, ATT_W), lambda i, j: (i, 0)),
            pl.BlockSpec((tm, SG_W), lambda i, j: (i, 0)),
            gspec(0), gspec(1), gspec(2),
            pl.BlockSpec((RET_V_W, tn), lambda i, j: (0, j)),
            pl.BlockSpec((ATT_W, tn), lambda i, j: (0, j)),
            pl.BlockSpec((SG_W, tn), lambda i, j: (0, j)),
            pl.BlockSpec((tn, d), lambda i, j: (j, 0)),
            pl.BlockSpec((tm, d), lambda i, j: (i, 0)),
        ],
        out_specs=pl.BlockSpec((tm, d), lambda i, j: (i, 0)),
        compiler_params=_cparams(("parallel", "arbitrary")),
        name="merge",
    )(o_r, o_a, o_s, p_g, p_g, p_g, wr, wa, ws, wo, x2)


def _split_w_in_kernel(w_ref, wa_ref, wb_ref):
    w = w_ref[0]
    wa_ref[:, :W_A_COLS] = w[:, :W_A_COLS].astype(BF16)
    wa_ref[:, W_A_COLS:] = jnp.zeros((w.shape[0], W_A_PAD - W_A_COLS), BF16)
    wb_ref[...] = w[:, W_A_COLS:].astype(BF16)


def _split_w_in(w_in, layer, d_model, *, tr=64):
    _, rows, cols = w_in.shape
    assert cols == W_A_COLS + 2 * SG_W + 3 * d_model and A1_W + KV_W + QI_W == W_A_PAD
    w_b_cols = cols - W_A_COLS
    assert w_b_cols % PROJ_TILE == 0 and rows % tr == 0
    return pl.pallas_call(
        _split_w_in_kernel,
        out_shape=(jax.ShapeDtypeStruct((rows, W_A_PAD), BF16), jax.ShapeDtypeStruct((rows, w_b_cols), BF16)),
        grid=(rows // tr,),
        in_specs=[pl.BlockSpec((1, tr, cols), lambda i: (layer, i, 0))],
        out_specs=(pl.BlockSpec((tr, W_A_PAD), lambda i: (i, 0)), pl.BlockSpec((tr, w_b_cols), lambda i: (i, 0))),
        compiler_params=_cparams(("parallel",)),
        name="split_w_in",
    )(w_in)


def _pick(n, prefs):
    for p in prefs:
        if n % p == 0:
            return p
    return n


def kernel(x, positions, rel_table, ffn1_norm, ffn1_w_gate, ffn1_w_up, ffn1_w_down, mix_norm, w_in, ret_norm, sg_norm, sg_w, sg_b, w_br_ret, w_br_att, w_br_sg, w_out, ffn2_norm, ffn2_w_gate, ffn2_w_up, ffn2_w_down, final_norm):
    b, s, d = x.shape
    t = b * s
    depth = w_in.shape[0]
    d_ff = ffn1_w_gate.shape[2]
    assert s % RET_CHUNK == 0 and d % PROJ_TILE == 0

    tm = _pick(t, (512, 256, 128))
    tm_proj = _pick(t, (1024, 512, 256, 128))
    tm_ffn = _pick(t, (512, 256, 128))
    tf = _pick(d_ff, (512, 256, 128))
    bq = _pick(s, (256, 128))
    bk = _pick(s, (512, 256, 128))

    cos_t, sin_t = _rope_tables(positions, tm=_pick(s, (2048, 1024, 512, 256, 128)))
    consts = _retention_constants()
    x2 = x.reshape(t, d)
    for l in range(depth):
        x2 = _ffn(x2, ffn1_norm[l], ffn1_w_gate[l].astype(BF16), ffn1_w_up[l].astype(BF16),
                  ffn1_w_down[l].astype(BF16), tm=tm_ffn, tf=tf)
        p_a1, p_kv, p_qi, p_sg, p_g = _proj(x2, mix_norm[l], *_split_w_in(w_in, l, d), tm=tm_proj)
        p_a13 = p_a1.reshape(b, s, -1)
        o_r, o_s = _local(p_a13, p_sg.reshape(b, s, -1), cos_t, sin_t, ret_norm[l], sg_norm[l], sg_w[l], sg_b[l],
                          consts)
        o_a = _attn(p_a13, p_qi.reshape(b, s, -1), p_kv.reshape(b, s, -1), positions, rel_table, bq=bq, bk=bk)
        x2 = _merge(o_r.reshape(t, -1), o_a.reshape(t, -1), o_s.reshape(t, -1), p_g,
                    w_br_ret[l].astype(BF16), w_br_att[l].astype(BF16), w_br_sg[l].astype(BF16),
                    w_out[l].astype(BF16), x2, tm=tm)
        x2 = _ffn(x2, ffn2_norm[l], ffn2_w_gate[l].astype(BF16), ffn2_w_up[l].astype(BF16),
                  ffn2_w_down[l].astype(BF16), final_norm if l == depth - 1 else None, tm=tm_ffn, tf=tf)
    return x2.reshape(b, s, d)
```

```python
import functools
import math

import numpy as np
import jax
import jax.numpy as jnp
from jax import lax
from jax.experimental import pallas as pl
from jax.experimental.pallas import tpu as pltpu

F32 = jnp.float32
BF16 = jnp.bfloat16
I32 = jnp.int32

RET_HEADS, RET_QK_DIM, RET_V_DIM, RET_CHUNK = 6, 64, 128, 128
ATT_HEADS, ATT_HEAD_DIM = 6, 128
IDX_HEADS, IDX_DIM = 4, 64
TOPK_MAX = 256
SG_GROUPS, SG_GROUP_DIM, SG_CHUNK = 4, 128, 128
REL_BUCKETS, REL_MAX_DIST = 32, 128
ROPE_BASE = 10000.0
EPS = 1e-6

RET_QK_W = RET_HEADS * RET_QK_DIM
RET_V_W = RET_HEADS * RET_V_DIM
ATT_W = ATT_HEADS * ATT_HEAD_DIM
IDX_Q_W = IDX_HEADS * IDX_DIM
SG_W = SG_GROUPS * SG_GROUP_DIM

LANES = 128
COUNT_ROWS = 64
KV_BUFFERS = 3
SEARCH_PERIOD = 8
ATTN_SPAN = 1
SOFTMAX_ROWS = 64
VMEM_LIMIT_BYTES = 56 * 1024 * 1024

PROJ_TILE = 512
W_A_COLS = 2 * RET_QK_W + 2 * RET_V_W + 3 * ATT_W + IDX_Q_W + IDX_DIM + IDX_HEADS
W_A_PAD = 5120
A1_QR, A1_KR, A1_VR, A1_GR, A1_QA = 0, 384, 768, 1536, 2304
A1_W = 3072
KV_K, KV_V = 0, 768
KV_W = 1536
QI_Q, QI_KIW = 0, 256
QI_W = 512
SG_U, SG_V = 0, 512
SG_PW = 1024

INT_MIN = np.int32(-2 ** 31)
INT_MAX = np.int32(2 ** 31 - 1)
NEG_INF = float("-inf")
LOG2E = math.log2(math.e)


def _cparams(sem):
    return pltpu.CompilerParams(dimension_semantics=sem, vmem_limit_bytes=VMEM_LIMIT_BYTES)


def _rms(x, g):
    return x * lax.rsqrt(jnp.mean(x * x, axis=-1, keepdims=True) + EPS) * g


def _ffn_kernel(x_ref, g_ref, wg_ref, wu_ref, wd_ref, *rest, final_norm):
    if final_norm:
        fg_ref, o_ref, xn_ref = rest
    else:
        o_ref, xn_ref = rest
    j = pl.program_id(1)

    @pl.when(j == 0)
    def _():
        xn_ref[...] = _rms(x_ref[...], g_ref[...]).astype(BF16)
        o_ref[...] = jnp.zeros_like(o_ref)

    xn = xn_ref[...]
    a = jnp.dot(xn, wg_ref[...], preferred_element_type=F32)
    b = jnp.dot(xn, wu_ref[...], preferred_element_type=F32)
    h = (a * jax.nn.sigmoid(a) * b).astype(BF16)
    o_ref[...] += jnp.dot(h, wd_ref[...], preferred_element_type=F32)

    @pl.when(j == pl.num_programs(1) - 1)
    def _():
        y = x_ref[...] + 0.5 * o_ref[...]
        if final_norm:
            y = _rms(y, fg_ref[...])
        o_ref[...] = y


def _ffn(x2, gain, wg, wu, wd, final_gain=None, *, tm, tf):
    t, d = x2.shape
    f = wg.shape[1]
    final_norm = final_gain is not None
    in_specs = [
        pl.BlockSpec((tm, d), lambda i, j: (i, 0)),
        pl.BlockSpec((1, d), lambda i, j: (0, 0)),
        pl.BlockSpec((d, tf), lambda i, j: (0, j)),
        pl.BlockSpec((d, tf), lambda i, j: (0, j)),
        pl.BlockSpec((tf, d), lambda i, j: (j, 0)),
    ]
    args = [x2, gain.reshape(1, d), wg, wu, wd]
    if final_norm:
        in_specs.append(pl.BlockSpec((1, d), lambda i, j: (0, 0)))
        args.append(final_gain.reshape(1, d))
    return pl.pallas_call(
        functools.partial(_ffn_kernel, final_norm=final_norm),
        out_shape=jax.ShapeDtypeStruct((t, d), F32),
        grid=(t // tm, f // tf),
        in_specs=in_specs,
        out_specs=pl.BlockSpec((tm, d), lambda i, j: (i, 0)),
        scratch_shapes=[pltpu.VMEM((tm, d), BF16)],
        compiler_params=_cparams(("parallel", "arbitrary")),
        name="ffn",
    )(*args)


def _proj_kernel(x_ref, g_ref, wa_ref, wb_ref, oa1_ref, okv_ref, oqi_ref, osg_ref, og_ref, xn_ref, *, ends):
    j = pl.program_id(1)
    e_a1, e_kv, e_qi, e_sg = ends

    @pl.when(j == 0)
    def _():
        xn_ref[...] = _rms(x_ref[...], g_ref[...]).astype(BF16)

    def tile(w_ref):
        return jnp.dot(xn_ref[...], w_ref[...], preferred_element_type=F32).astype(BF16)

    @pl.when(j < e_a1)
    def _():
        oa1_ref[...] = tile(wa_ref)

    @pl.when(jnp.logical_and(j >= e_a1, j < e_kv))
    def _():
        okv_ref[...] = tile(wa_ref)

    @pl.when(jnp.logical_and(j >= e_kv, j < e_qi))
    def _():
        oqi_ref[...] = tile(wa_ref)

    @pl.when(jnp.logical_and(j >= e_qi, j < e_sg))
    def _():
        osg_ref[...] = tile(wb_ref)

    @pl.when(j >= e_sg)
    def _():
        og_ref[0] = tile(wb_ref)


def _proj(x2, gain, w_a, w_b, *, tm):
    t, d = x2.shape
    tn = PROJ_TILE
    n_a, n_b = w_a.shape[1] // tn, w_b.shape[1] // tn
    e_a1 = A1_W // tn
    e_kv = e_a1 + KV_W // tn
    e_qi = e_kv + QI_W // tn
    e_sg = e_qi + SG_PW // tn
    assert e_qi == n_a
    t_g = n_a + n_b - e_sg

    def clampspec(first, count):
        return pl.BlockSpec((tm, tn), lambda i, j: (i, jnp.clip(j - first, 0, count - 1)))

    return pl.pallas_call(
        functools.partial(_proj_kernel, ends=(e_a1, e_kv, e_qi, e_sg)),
        out_shape=(jax.ShapeDtypeStruct((t, A1_W), BF16),
                   jax.ShapeDtypeStruct((t, KV_W), BF16),
                   jax.ShapeDtypeStruct((t, QI_W), BF16),
                   jax.ShapeDtypeStruct((t, SG_PW), BF16),
                   jax.ShapeDtypeStruct((t_g, t, tn), BF16)),
        grid=(t // tm, n_a + n_b),
        in_specs=[
            pl.BlockSpec((tm, d), lambda i, j: (i, 0)),
            pl.BlockSpec((1, d), lambda i, j: (0, 0)),
            pl.BlockSpec((d, tn), lambda i, j: (0, jnp.clip(j, 0, n_a - 1))),
            pl.BlockSpec((d, tn), lambda i, j: (0, jnp.clip(j - n_a, 0, n_b - 1))),
        ],
        out_specs=(clampspec(0, e_a1), clampspec(e_a1, e_kv - e_a1), clampspec(e_kv, e_qi - e_kv),
                   clampspec(e_qi, e_sg - e_qi),
                   pl.BlockSpec((1, tm, tn), lambda i, j: (jnp.clip(j - e_sg, 0, t_g - 1), i, 0))),
        scratch_shapes=[pltpu.VMEM((tm, d), BF16)],
        compiler_params=_cparams(("arbitrary", "arbitrary")),
        name="proj",
    )(x2, gain.reshape(1, d), w_a, w_b)


def _rope_table_kernel(pos_ref, invf_ref, sign_ref, c_ref, s_ref):
    ang = pos_ref[0].astype(F32) * invf_ref[...]
    c_ref[0] = jnp.cos(ang)
    s_ref[0] = jnp.sin(ang) * sign_ref[...]


def _rope_tables(positions, *, tm):
    b, s = positions.shape
    half = RET_QK_DIM // 2
    inv_freq = ROPE_BASE ** (-jnp.arange(0, RET_QK_DIM, 2, dtype=F32) / RET_QK_DIM)
    lane = np.arange(LANES)
    invf = inv_freq[lane % half].reshape(1, LANES)
    sign = jnp.asarray(np.where(lane % RET_QK_DIM < half, -1.0, 1.0), F32).reshape(1, LANES)
    out = jax.ShapeDtypeStruct((b, s, LANES), F32)
    return pl.pallas_call(
        _rope_table_kernel,
        out_shape=(out, out),
        grid=(b, s // tm),
        in_specs=[
            pl.BlockSpec((1, tm, 1), lambda i, j: (i, j, 0)),
            pl.BlockSpec((1, LANES), lambda i, j: (0, 0)),
            pl.BlockSpec((1, LANES), lambda i, j: (0, 0)),
        ],
        out_specs=(pl.BlockSpec((1, tm, LANES), lambda i, j: (i, j, 0)),
                   pl.BlockSpec((1, tm, LANES), lambda i, j: (i, j, 0))),
        compiler_params=_cparams(("parallel", "parallel")),
        name="rope_tables",
    )(positions.reshape(b, s, 1), invf, sign)


def _retention_constants():
    c = RET_CHUNK
    h = np.arange(RET_HEADS, dtype=np.float64)
    log_g = np.log(1.0 - 2.0 ** (-5.0 - h))
    idx = np.arange(c, dtype=np.float64)
    rel = idx[:, None] - idx[None, :]
    dmat = np.where(rel >= 0, np.exp(np.maximum(rel, 0.0) * log_g[:, None, None]), 0.0)
    k_decay = np.exp((c - 1 - idx)[None, :] * log_g[:, None])
    q_decay = np.exp((idx + 1)[None, :] * log_g[:, None])
    chunk_decay = np.exp(c * log_g)
    kd = np.repeat(k_decay.T, RET_QK_DIM, axis=1)
    qd = np.repeat(q_decay.T, RET_QK_DIM, axis=1) * RET_QK_DIM ** -0.5
    lane = np.arange(RET_QK_W)
    src = (lane // RET_QK_DIM) * RET_QK_DIM + (lane % RET_QK_DIM + RET_QK_DIM // 2) % RET_QK_DIM
    perm = np.zeros((RET_QK_W, RET_QK_W), np.float32)
    perm[src, lane] = 1.0
    return (jnp.asarray(dmat, F32), jnp.asarray(qd, F32), jnp.asarray(kd, F32),
            [float(np.float32(v)) for v in chunk_decay], jnp.asarray(perm, BF16))


def _gelu_tanh(x):
    return 0.5 * x * (1.0 + jnp.tanh(math.sqrt(2.0 / math.pi) * (x + 0.044715 * (x * x * x))))


def _local_kernel(p_ref, psg_ref, cos_ref, sin_ref,
                  perm_ref, dmat_ref, qd_ref, kd_ref, rn_ref, sn_ref, sgw_ref, sgb_ref,
                  or_ref, os_ref, state_ref, *, chunk_decay):
    @pl.when(pl.program_id(1) == 0)
    def _():
        state_ref[...] = jnp.zeros_like(state_ref)

    n_rep = RET_QK_W // LANES
    cos3 = jnp.concatenate([cos_ref[0]] * n_rep, axis=1)
    sin3 = jnp.concatenate([sin_ref[0]] * n_rep, axis=1)
    q = p_ref[0, :, A1_QR:A1_QR + RET_QK_W]
    k = p_ref[0, :, A1_KR:A1_KR + RET_QK_W]
    perm = perm_ref[...]
    qr = q.astype(F32) * cos3 + jnp.dot(q, perm, preferred_element_type=F32) * sin3
    kr = k.astype(F32) * cos3 + jnp.dot(k, perm, preferred_element_type=F32) * sin3
    q_in = (qr * (RET_QK_DIM ** -0.5)).astype(BF16)
    q_cr = (qr * qd_ref[...]).astype(BF16)
    k_b = kr.astype(BF16)
    k_dec_t = jnp.transpose(kr * kd_ref[...]).astype(BF16)
    v = p_ref[0, :, A1_VR:A1_VR + RET_V_W]
    outs = []
    for h in range(RET_HEADS):
        qs = slice(h * RET_QK_DIM, (h + 1) * RET_QK_DIM)
        vh = v[:, h * RET_V_DIM:(h + 1) * RET_V_DIM]
        a = lax.dot_general(q_in[:, qs], k_b[:, qs], (((1,), (1,)), ((), ())),
                            preferred_element_type=F32) * dmat_ref[h]
        prev = state_ref[h]
        o = (jnp.dot(a.astype(BF16), vh, preferred_element_type=F32)
             + jnp.dot(q_cr[:, qs], prev.astype(BF16), preferred_element_type=F32))
        state_ref[h] = prev * chunk_decay[h] + jnp.dot(k_dec_t[qs, :], vh, preferred_element_type=F32)
        mu = jnp.mean(o, axis=-1, keepdims=True)
        d = o - mu
        var = jnp.mean(d * d, axis=-1, keepdims=True)
        outs.append(d * lax.rsqrt(var + EPS))
    g = p_ref[0, :, A1_GR:A1_GR + RET_V_W].astype(F32)
    y = jnp.concatenate(outs, axis=1) * rn_ref[...]
    or_ref[0] = (y * (g * jax.nn.sigmoid(g))).astype(BF16)

    u = _gelu_tanh(psg_ref[0, :, SG_U:SG_U + SG_W].astype(F32))
    vn = _rms(_gelu_tanh(psg_ref[0, :, SG_V:SG_V + SG_W].astype(F32)), sn_ref[...])
    vn_b = vn.astype(BF16)
    c = SG_CHUNK
    tril = lax.broadcasted_iota(I32, (c, c), 0) >= lax.broadcasted_iota(I32, (c, c), 1)
    bias = sgb_ref[...]
    mixed = []
    for gi in range(SG_GROUPS):
        w = jnp.where(tril, sgw_ref[gi], jnp.zeros((), BF16))
        m = jnp.dot(w, vn_b[:, gi * SG_GROUP_DIM:(gi + 1) * SG_GROUP_DIM], preferred_element_type=F32)
        mixed.append(m + bias[:, gi:gi + 1])
    os_ref[0] = (u * jnp.concatenate(mixed, axis=1)).astype(BF16)


def _local(p_a13, p_sg3, cos_t, sin_t, ret_norm, sg_norm, sg_w, sg_b, consts):
    b, s, _ = p_a13.shape
    c = RET_CHUNK
    dmat, qd, kd, chunk_decay, perm = consts

    def whole(arr):
        nd = arr.ndim
        return pl.BlockSpec(arr.shape, lambda i, j: (0,) * nd)

    rn = ret_norm.reshape(1, RET_V_W)
    sn = sg_norm.reshape(1, SG_W)
    sgw = sg_w.astype(BF16)
    sgb_t = jnp.transpose(sg_b)
    tab = pl.BlockSpec((1, c, LANES), lambda i, j: (i, j, 0))
    return pl.pallas_call(
        functools.partial(_local_kernel, chunk_decay=chunk_decay),
        out_shape=(jax.ShapeDtypeStruct((b, s, RET_V_W), BF16),
                   jax.ShapeDtypeStruct((b, s, SG_W), BF16)),
        grid=(b, s // c),
        in_specs=[
            pl.BlockSpec((1, c, A1_W), lambda i, j: (i, j, 0)),
            pl.BlockSpec((1, c, SG_PW), lambda i, j: (i, j, 0)),
            tab, tab,
            whole(perm), whole(dmat), whole(qd), whole(kd), whole(rn), whole(sn), whole(sgw), whole(sgb_t),
        ],
        out_specs=(pl.BlockSpec((1, c, RET_V_W), lambda i, j: (i, j, 0)),
                   pl.BlockSpec((1, c, SG_W), lambda i, j: (i, j, 0))),
        scratch_shapes=[pltpu.VMEM((RET_HEADS, RET_QK_DIM, RET_V_DIM), F32)],
        compiler_params=_cparams(("parallel", "arbitrary")),
        name="local_mixers",
    )(p_a13, p_sg3, cos_t, sin_t, perm, dmat, qd, kd, rn, sn, sgw, sgb_t)


def _t5_bucket_table():
    max_exact = REL_BUCKETS // 2
    d = np.arange(REL_MAX_DIST)
    df = np.maximum(d, 1).astype(np.float32)
    large = max_exact + (np.log(df / max_exact) / np.float32(math.log(REL_MAX_DIST / max_exact))
                         * (REL_BUCKETS - max_exact)).astype(np.int32)
    large = np.minimum(large, REL_BUCKETS - 1)
    return np.where(d < max_exact, d, large)


def _attn_kernel(qmin_ref, kmax_ref, qmin_sub_ref, kmax_sub_ref,
                 qa_ref, pqi_ref, kiw_ref, kv_hbm, posq_ref, posk_ref, btab_ref,
                 o_ref, kv_buf, kv_sem, keys_ref, thr_ref, aux_ref, cnt_ref, m_ref, acc_ref, bias_ref,
                 s_ref, p_ref, alpha_ref,
                 *, bq, bk, n_keep, seq, far_dist):
    b = pl.program_id(0)
    qi = pl.program_id(1)
    n_chunks = (qi * bq + bq - 1) // bk + 1
    nlc = bk // LANES
    bka = bk * ATTN_SPAN
    n_att = (n_chunks + ATTN_SPAN - 1) // ATTN_SPAN

    def kv_copy(ca, slot):
        return pltpu.make_async_copy(kv_hbm.at[b, pl.ds(ca * bka, bka), :], kv_buf.at[slot], kv_sem.at[slot])

    for i in range(KV_BUFFERS - 1):
        @pl.when(i < n_att)
        def _():
            kv_copy(i, i).start()
    nsub_q, nsub_k = bq // LANES, bk // LANES
    k_f = float(n_keep)

    def tile_lanes(x, n=None):
        return jnp.concatenate([x] * (nlc if n is None else n), axis=1)

    def lane_chunk(x, j):
        return x[:, j * LANES:(j + 1) * LANES]

    def _select():
        qiv = pqi_ref[0, :, QI_Q:QI_Q + IDX_Q_W]
        zpad = jnp.zeros((bq, LANES - IDX_DIM), BF16)
        q_heads = [jnp.concatenate([qiv[:, h * IDX_DIM:(h + 1) * IDX_DIM], zpad], axis=1)
                   for h in range(IDX_HEADS)]
        kiw_q = pqi_ref[0, :, QI_KIW:QI_KIW + LANES]
        w = kiw_q[:, IDX_DIM:IDX_DIM + IDX_HEADS].astype(F32)
        w_bc = [jnp.broadcast_to(w[:, h:h + 1], (bq, bk)) for h in range(IDX_HEADS)]
        row_t = qi * bq + lax.broadcasted_iota(I32, (bq, bk), 0)
        lane_s = lax.broadcasted_iota(I32, (bq, bk), 1)

        def score_body(cc, carry):
            off = pl.multiple_of(cc * bk, bk)
            kc = kiw_ref[0, pl.ds(off, bk), :]
            sc = jnp.zeros((bq, bk), F32)
            for h in range(IDX_HEADS):
                sh = lax.dot_general(q_heads[h], kc, (((1,), (1,)), ((), ())), preferred_element_type=F32)
                sc = sc + w_bc[h] * jnp.maximum(sh, 0.0)
            sc = jnp.where(sc == 0.0, 0.0, sc)
            bits = lax.bitcast_convert_type(sc, I32)
            key = jnp.where(bits < 0, bits ^ INT_MAX, bits)
            key = jnp.where(cc * bk + lane_s <= row_t, key, INT_MIN)
            keys_ref[cc] = key
            for r in range(bq // COUNT_ROWS):
                rows = slice(r * COUNT_ROWS, (r + 1) * COUNT_ROWS)
                m1 = thr_ref[rows, :]
                m2 = aux_ref[rows, :]
                for j in range(nlc):
                    x = key[rows, j * LANES:(j + 1) * LANES]
                    m2 = jnp.maximum(m2, jnp.minimum(m1, x))
                    m1 = jnp.maximum(m1, x)
                thr_ref[rows, :] = m1
                aux_ref[rows, :] = m2
            return carry

        thr_ref[...] = jnp.full((bq, LANES), INT_MIN, I32)
        aux_ref[...] = jnp.full((bq, LANES), INT_MIN, I32)
        lax.fori_loop(0, n_chunks, score_body, 0)
        top1, top2 = thr_ref[...], aux_ref[...]

        nt = bq // LANES

        def to_rows(c):
            return jnp.concatenate(
                [jnp.transpose(jnp.broadcast_to(c[k:k + 1, :], (LANES, LANES))) for k in range(nt)], axis=0)

        def from_rows(x, op):
            return jnp.concatenate(
                [op(jnp.transpose(x[k * LANES:(k + 1) * LANES, :]), axis=0, keepdims=True) for k in range(nt)], axis=0)

        def count_ge(thr):
            thr_ref[...] = to_rows(thr)
            cnt_ref[...] = jnp.zeros((bq, LANES), F32)

            def body(cc, carry):
                for r in range(bq // COUNT_ROWS):
                    rows = pl.ds(r * COUNT_ROWS, COUNT_ROWS)
                    t = keys_ref[cc, rows, :]
                    th = thr_ref[rows, :]
                    acc = cnt_ref[rows, :]
                    for j in range(nlc):
                        acc = acc + jnp.where(lane_chunk(t, j) >= th, 1.0, 0.0)
                    cnt_ref[rows, :] = acc
                return carry

            lax.fori_loop(0, n_chunks, body, 0)
            return from_rows(cnt_ref[...], jnp.sum)

        def key_value(k):
            return lax.bitcast_convert_type(jnp.where(k < 0, k ^ INT_MAX, k), F32)

        def value_key(v):
            bits = lax.bitcast_convert_type(v, I32)
            return jnp.where(bits < 0, bits ^ INT_MAX, bits)

        lo0 = from_rows(top2, jnp.min)
        top = from_rows(top1, jnp.max)
        hi0 = jnp.where(top == INT_MAX, INT_MAX, top + 1)
        log2_k = math.log2(k_f)

        def excess(cnt):
            return jnp.log2(jnp.maximum(cnt, 0.5)) - log2_k

        def resolved(lo, hi, c_lo):
            return jnp.logical_or(hi - 1 <= lo, c_lo == k_f)

        def probe(mid, lo, hi, c_lo, c_hi, f_lo, f_hi, last):
            cnt = count_ge(mid)
            ok = cnt >= k_f
            f_new = excess(cnt)
            f_hi = jnp.where(jnp.logical_and(ok, last > 0.0), 0.5 * f_hi, f_hi)
            f_lo = jnp.where(jnp.logical_and(jnp.logical_not(ok), last < 0.0), 0.5 * f_lo, f_lo)
            return (jnp.where(ok, mid, lo), jnp.where(ok, hi, mid), jnp.where(ok, cnt, c_lo), jnp.where(ok, c_hi, cnt),
                    jnp.where(ok, f_new, f_lo), jnp.where(ok, f_hi, f_new), jnp.where(ok, 1.0, -1.0))

        def search_step(state):
            it, lo, hi, c_lo, c_hi, f_lo, f_hi, last, _ = state
            done = resolved(lo, hi, c_lo)
            mid_b = (lo & hi) + ((lo ^ hi) >> 1)
            v_lo = key_value(lo)
            mid_i = jnp.clip(value_key(v_lo + f_lo / (f_lo - f_hi) * (key_value(hi) - v_lo)), lo + 1, hi - 1)
            mid = jnp.where(it % SEARCH_PERIOD == SEARCH_PERIOD - 1, mid_b, mid_i)
            mid = jnp.where(done, lo, mid)
            old = (lo, hi, c_lo, c_hi, f_lo, f_hi, last)
            new = tuple(jnp.where(done, o, u) for o, u in zip(old, probe(mid, *old)))
            pending = jnp.max(jnp.where(resolved(new[0], new[1], new[2]), 0.0, 1.0))
            return (it + 1,) + new + (pending,)

        c_nominal = jnp.full((nt, LANES), 8.0 * k_f, F32)
        zero_c = jnp.zeros((nt, LANES), F32)
        state = (lo0, hi0, c_nominal, zero_c, excess(c_nominal), excess(zero_c), zero_c)
        for z in (0, 1):
            lo, hi = state[0], state[1]
            zk = jnp.full((nt, LANES), z, I32)
            inside = jnp.logical_and(jnp.logical_and(zk > lo, zk < hi), jnp.logical_not(resolved(lo, hi, state[2])))
            new = probe(jnp.where(inside, zk, lo), *state)
            state = tuple(jnp.where(inside, upd, old) for old, upd in zip(state, new))
        pending0 = jnp.max(jnp.where(resolved(state[0], state[1], state[2]), 0.0, 1.0))
        state = lax.while_loop(lambda st: st[8] > 0.0, search_step, (jnp.int32(0),) + state + (pending0,))
        tau_c, hi_c, c_hi = state[1], state[2], state[4]
        need_c = jnp.where(hi_c - 1 <= tau_c, k_f - c_hi, float(seq))
        need = to_rows(jnp.where(tau_c == INT_MIN, 0.0, need_c))
        tau = to_rows(tau_c)
        upper = lax.broadcasted_iota(I32, (LANES, LANES), 0) <= lax.broadcasted_iota(I32, (LANES, LANES), 1)
        scan = jnp.concatenate([jnp.where(upper, 1.0, 0.0), jnp.ones((LANES, LANES), F32)], axis=1).astype(BF16)
        cnt_ref[...] = jnp.zeros((bq, LANES), F32)

        def build_mask(cc, carry):
            t = keys_ref[cc]
            cols = []
            seen = cnt_ref[...]
            for j in range(nlc):
                tj = lane_chunk(t, j)
                tie = tj == tau
                pt = jnp.dot(jnp.where(tie, 1.0, 0.0).astype(BF16), scan, preferred_element_type=F32)
                keep_tie = jnp.logical_and(tie, seen + pt[:, :LANES] <= need)
                cols.append(jnp.where(jnp.logical_or(tj > tau, keep_tie), 0.0, NEG_INF))
                seen = seen + pt[:, LANES:]
            cnt_ref[...] = seen
            keys_ref[cc] = lax.bitcast_convert_type(jnp.concatenate(cols, axis=1), I32)
            return carry

        lax.fori_loop(0, n_chunks, build_mask, 0)

        def mask_out(cc, carry):
            keys_ref[cc] = lax.bitcast_convert_type(jnp.full((bq, bk), NEG_INF, F32), I32)
            return carry

        lax.fori_loop(n_chunks, n_att * ATTN_SPAN, mask_out, 0)

        m_ref[...] = jnp.full(m_ref.shape, NEG_INF, F32)
        acc_ref[...] = jnp.zeros_like(acc_ref)

    _select()

    c1 = (ATT_HEAD_DIM ** -0.5) * LOG2E
    q = qa_ref[0]
    ones_v = jnp.ones((bka, ATT_HEAD_DIM), BF16)

    def heads(near, ca, kk, vv):
        for h in range(ATT_HEADS):
            hs = slice(h * ATT_HEAD_DIM, (h + 1) * ATT_HEAD_DIM)
            s_ref[...] = lax.dot_general(q[:, hs], kk[:, hs], (((1,), (1,)), ((), ())),
                                         preferred_element_type=F32)
            for r in range(bq // SOFTMAX_ROWS):
                rows = slice(r * SOFTMAX_ROWS, (r + 1) * SOFTMAX_ROWS)
                mask = jnp.concatenate([keys_ref[ca * ATTN_SPAN + i, rows, :] for i in range(ATTN_SPAN)], axis=1)
                t = s_ref[rows, :] * c1 + lax.bitcast_convert_type(mask, F32)
                if near:
                    t = t + bias_ref[h, rows, :]
                m_old = m_ref[h, rows, :]
                m_new = jnp.maximum(m_old, jnp.max(t, axis=1, keepdims=True))
                m_safe = jnp.where(m_new == NEG_INF, 0.0, m_new)
                alpha_ref[rows, :] = jnp.exp2(m_old - m_safe)
                p_ref[rows, :] = jnp.exp2(t - tile_lanes(m_safe, bka // LANES)).astype(BF16)
                m_ref[h, rows, :] = m_new
            v_aug = jnp.concatenate([vv[:, hs], ones_v], axis=1)
            acc_ref[h] = (tile_lanes(alpha_ref[...], 2) * acc_ref[h]
                          + jnp.dot(p_ref[...], v_aug, preferred_element_type=F32))

    def attend(ca, carry):
        slot = ca % KV_BUFFERS
        nxt = ca + (KV_BUFFERS - 1)

        @pl.when(nxt < n_att)
        def _():
            kv_copy(nxt, nxt % KV_BUFFERS).start()

        kv_copy(ca, slot).wait()
        kk = kv_buf[slot, :, KV_K:KV_K + ATT_W]
        vv = kv_buf[slot, :, KV_V:KV_V + ATT_W]
        kmax = kmax_ref[b * (seq // bk) + ca * ATTN_SPAN]
        for i in range(1, ATTN_SPAN):
            kmax = jnp.maximum(kmax, kmax_ref[b * (seq // bk) + ca * ATTN_SPAN + i])
        far = qmin_ref[b * (seq // bq) + qi] - kmax >= far_dist

        @pl.when(far)
        def _():
            heads(False, ca, kk, vv)

        @pl.when(jnp.logical_not(far))
        def _():
            nsub = seq // LANES
            for r in range(nsub_q):
                for j in range(ATTN_SPAN * nsub_k):
                    rows = slice(r * LANES, (r + 1) * LANES)
                    cols = slice(j * LANES, (j + 1) * LANES)
                    pcols = slice((j % nsub_k) * LANES, (j % nsub_k + 1) * LANES)
                    sub_far = (qmin_sub_ref[b * nsub + qi * nsub_q + r]
                               - kmax_sub_ref[b * nsub + ca * ATTN_SPAN * nsub_k + j]) >= far_dist

                    @pl.when(sub_far)
                    def _():
                        for h in range(ATT_HEADS):
                            bias_ref[h, rows, cols] = jnp.zeros((LANES, LANES), F32)

                    @pl.when(jnp.logical_not(sub_far))
                    def _():
                        half = LANES // 2
                        for hr in range(2):
                            rr = slice(r * LANES + hr * half, r * LANES + (hr + 1) * half)
                            dist = posq_ref[0, rr, :] - posk_ref[0, ca * ATTN_SPAN + j // nsub_k, :, pcols]
                            idx = jnp.clip(dist, 0, REL_MAX_DIST - 1)
                            for h in range(ATT_HEADS):
                                row = jnp.broadcast_to(btab_ref[h:h + 1, :], (half, LANES))
                                bias_ref[h, rr, cols] = jnp.take_along_axis(row, idx, axis=1)

            heads(True, ca, kk, vv)

        return carry

    lax.fori_loop(0, n_att, attend, 0)

    outs = []
    for h in range(ATT_HEADS):
        a = acc_ref[h]
        outs.append(a[:, :ATT_HEAD_DIM] / a[:, ATT_HEAD_DIM:])
    o_ref[0] = jnp.concatenate(outs, axis=1).astype(BF16)


def _attn(p_a13, p_qi3, p_kv3, positions, rel_table, *, bq, bk):
    b, s, _ = p_qi3.shape
    assert bq % LANES == 0 and bk % LANES == 0
    n_keep = min(TOPK_MAX, s // 4)
    assert n_keep <= 2 * LANES
    assert (s // bk) % ATTN_SPAN == 0
    bka = bk * ATTN_SPAN
    nq, nkc, nsub = s // bq, s // bk, s // LANES
    qmin =jnp.min(positions.reshape(b, nq, bq), axis=-1).reshape(-1)
    kmax = jnp.max(positions.reshape(b, nkc, bk), axis=-1).reshape(-1)
    qmin_sub = jnp.min(positions.reshape(b, nsub, LANES), axis=-1).reshape(-1)
    kmax_sub = jnp.max(positions.reshape(b, nsub, LANES), axis=-1).reshape(-1)
    buckets = _t5_bucket_table()
    far_dist = REL_MAX_DIST - 1
    assert np.all(np.diff(buckets) >= 0) and buckets[far_dist] == REL_BUCKETS - 1
    relb = ((rel_table - rel_table[REL_BUCKETS - 1:REL_BUCKETS, :]) * LOG2E).astype(F32)
    btab = jnp.pad(jnp.transpose(relb[buckets, :]), ((0, 8 - ATT_HEADS), (0, 0)))

    grid_spec = pltpu.PrefetchScalarGridSpec(
        num_scalar_prefetch=4,
        grid=(b, nq),
        in_specs=[
            pl.BlockSpec((1, bq, ATT_W), lambda i, t, *_: (i, t, A1_QA // ATT_W)),
            pl.BlockSpec((1, bq, QI_W), lambda i, t, *_: (i, t, 0)),
            pl.BlockSpec((1, s, LANES), lambda i, t, *_: (i, 0, QI_KIW // LANES)),
            pl.BlockSpec(memory_space=pl.ANY),
            pl.BlockSpec((1, bq, 1), lambda i, t, *_: (i, t, 0)),
            pl.BlockSpec((1, nkc, 1, bk), lambda i, t, *_: (i, 0, 0, 0)),
            pl.BlockSpec((8, LANES), lambda i, t, *_: (0, 0)),
        ],
        out_specs=pl.BlockSpec((1, bq, ATT_W), lambda i, t, *_: (i, t, 0)),
        scratch_shapes=[
            pltpu.VMEM((KV_BUFFERS, bka, KV_W), BF16),
            pltpu.SemaphoreType.DMA((KV_BUFFERS,)),
            pltpu.VMEM((nkc, bq, bk), I32),
            pltpu.VMEM((bq, LANES), I32),
            pltpu.VMEM((bq, LANES), I32),
            pltpu.VMEM((bq, LANES), F32),
            pltpu.VMEM((ATT_HEADS, bq, LANES), F32),
            pltpu.VMEM((ATT_HEADS, bq, 2 * ATT_HEAD_DIM), F32),
            pltpu.VMEM((ATT_HEADS, bq, bka), F32),
            pltpu.VMEM((bq, bka), F32),
            pltpu.VMEM((bq, bka), BF16),
            pltpu.VMEM((bq, LANES), F32),
        ],
    )
    kern = functools.partial(_attn_kernel, bq=bq, bk=bk, n_keep=n_keep, seq=s,
                             far_dist=far_dist)
    return pl.pallas_call(
        kern,
        out_shape=jax.ShapeDtypeStruct((b, s, ATT_W), BF16),
        grid_spec=grid_spec,
        compiler_params=_cparams(("parallel", "arbitrary")),
        name="sparse_attn",
    )(qmin, kmax, qmin_sub, kmax_sub,
      p_a13, p_qi3, p_qi3, p_kv3, positions.reshape(b, s, 1), positions.reshape(b, nkc, 1, bk), btab)


def _merge_kernel(or_ref, oa_ref, os_ref, gr_ref, ga_ref, gs_ref, wr_ref, wa_ref, ws_ref, wo_ref, x_ref, o_ref):
    j = pl.program_id(1)

    @pl.when(j == 0)
    def _():
        o_ref[...] = jnp.zeros_like(o_ref)

    def branch(o, w, g):
        return jax.nn.sigmoid(g[0].astype(F32)) * jnp.dot(o[...], w[...], preferred_element_type=F32)

    merged = branch(or_ref, wr_ref, gr_ref) + branch(oa_ref, wa_ref, ga_ref) + branch(os_ref, ws_ref, gs_ref)
    o_ref[...] += jnp.dot(merged.astype(BF16), wo_ref[...], preferred_element_type=F32)

    @pl.when(j == pl.num_programs(1) - 1)
    def _():
        o_ref[...] = x_ref[...] + o_ref[...]


def _merge(o_r, o_a, o_s, p_g, wr, wa, ws, wo, x2, *, tm):
    t, d = x2.shape
    tn = PROJ_TILE
    per_gate = d // tn

    def gspec(g):
        return pl.BlockSpec((1, tm, tn), lambda i, j: (g * per_gate + j, i, 0))

    return pl.pallas_call(
        _merge_kernel,
        out_shape=jax.ShapeDtypeStruct((t, d), F32),
        grid=(t // tm, per_gate),
        in_specs=[
            pl.BlockSpec((tm, RET_V_W), lambda i, j: (i, 0)),
            pl.BlockSpec((tm, ATT_W), lambda i, j: (i, 0)),
            pl.BlockSpec((tm, SG_W), lambda i, j: (i, 0)),
            gspec(0), gspec(1), gspec(2),
            pl.BlockSpec((RET_V_W, tn), lambda i, j: (0, j)),
            pl.BlockSpec((ATT_W, tn), lambda i, j: (0, j)),
            pl.BlockSpec((SG_W, tn), lambda i, j: (0, j)),
            pl.BlockSpec((tn, d), lambda i, j: (j, 0)),
            pl.BlockSpec((tm, d), lambda i, j: (i, 0)),
        ],
        out_specs=pl.BlockSpec((tm, d), lambda i, j: (i, 0)),
        compiler_params=_cparams(("parallel", "arbitrary")),
        name="merge",
    )(o_r, o_a, o_s, p_g, p_g, p_g, wr, wa, ws, wo, x2)


def _split_w_in_kernel(w_ref, wa_ref, wb_ref):
    w = w_ref[0]
    wa_ref[:, :W_A_COLS] = w[:, :W_A_COLS].astype(BF16)
    wa_ref[:, W_A_COLS:] = jnp.zeros((w.shape[0], W_A_PAD - W_A_COLS), BF16)
    wb_ref[...] = w[:, W_A_COLS:].astype(BF16)


def _split_w_in(w_in, layer, d_model, *, tr=64):
    _, rows, cols = w_in.shape
    assert cols == W_A_COLS + 2 * SG_W + 3 * d_model and A1_W + KV_W + QI_W == W_A_PAD
    w_b_cols = cols - W_A_COLS
    assert w_b_cols % PROJ_TILE == 0 and rows % tr == 0
    return pl.pallas_call(
        _split_w_in_kernel,
        out_shape=(jax.ShapeDtypeStruct((rows, W_A_PAD), BF16), jax.ShapeDtypeStruct((rows, w_b_cols), BF16)),
        grid=(rows // tr,),
        in_specs=[pl.BlockSpec((1, tr, cols), lambda i: (layer, i, 0))],
        out_specs=(pl.BlockSpec((tr, W_A_PAD), lambda i: (i, 0)), pl.BlockSpec((tr, w_b_cols), lambda i: (i, 0))),
        compiler_params=_cparams(("parallel",)),
        name="split_w_in",
    )(w_in)


def _pick(n, prefs):
    for p in prefs:
        if n % p == 0:
            return p
    return n


def kernel(x, positions, rel_table, ffn1_norm, ffn1_w_gate, ffn1_w_up, ffn1_w_down, mix_norm, w_in, ret_norm, sg_norm, sg_w, sg_b, w_br_ret, w_br_att, w_br_sg, w_out, ffn2_norm, ffn2_w_gate, ffn2_w_up, ffn2_w_down, final_norm):
    b, s, d = x.shape
    t = b * s
    depth = w_in.shape[0]
    d_ff = ffn1_w_gate.shape[2]
    assert s % RET_CHUNK == 0 and d % PROJ_TILE == 0

    tm = _pick(t, (512, 256, 128))
    tm_proj = _pick(t, (1024, 512, 256, 128))
    tm_ffn = _pick(t, (512, 256, 128))
    tf = _pick(d_ff, (512, 256, 128))
    bq = _pick(s, (256, 128))
    bk = _pick(s, (512, 256, 128))

    cos_t, sin_t = _rope_tables(positions, tm=_pick(s, (2048, 1024, 512, 256, 128)))
    consts = _retention_constants()
    x2 = x.reshape(t, d)
    for l in range(depth):
        x2 = _ffn(x2, ffn1_norm[l], ffn1_w_gate[l].astype(BF16), ffn1_w_up[l].astype(BF16),
                  ffn1_w_down[l].astype(BF16), tm=tm_ffn, tf=tf)
        p_a1, p_kv, p_qi, p_sg, p_g = _proj(x2, mix_norm[l], *_split_w_in(w_in, l, d), tm=tm_proj)
        p_a13 = p_a1.reshape(b, s, -1)
        o_r, o_s = _local(p_a13, p_sg.reshape(b, s, -1), cos_t, sin_t, ret_norm[l], sg_norm[l], sg_w[l], sg_b[l],
                          consts)
        o_a = _attn(p_a13, p_qi.reshape(b, s, -1), p_kv.reshape(b, s, -1), positions, rel_table, bq=bq, bk=bk)
        x2 = _merge(o_r.reshape(t, -1), o_a.reshape(t, -1), o_s.reshape(t, -1), p_g,
                    w_br_ret[l].astype(BF16), w_br_att[l].astype(BF16), w_br_sg[l].astype(BF16),
                    w_out[l].astype(BF16), x2, tm=tm)
        x2 = _ffn(x2, ffn2_norm[l], ffn2_w_gate[l].astype(BF16), ffn2_w_up[l].astype(BF16),
                  ffn2_w_down[l].astype(BF16), final_norm if l == depth - 1 else None, tm=tm_ffn, tf=tf)
    return x2.reshape(b, s, d)
```

```python
import functools
import math

import numpy as np
import jax
import jax.numpy as jnp
from jax import lax
from jax.experimental import pallas as pl
from jax.experimental.pallas import tpu as pltpu

F32 = jnp.float32
BF16 = jnp.bfloat16
I32 = jnp.int32

RET_HEADS, RET_QK_DIM, RET_V_DIM, RET_CHUNK = 6, 64, 128, 128
ATT_HEADS, ATT_HEAD_DIM = 6, 128
IDX_HEADS, IDX_DIM = 4, 64
TOPK_MAX = 256
SG_GROUPS, SG_GROUP_DIM, SG_CHUNK = 4, 128, 128
REL_BUCKETS, REL_MAX_DIST = 32, 128
ROPE_BASE = 10000.0
EPS = 1e-6

RET_QK_W = RET_HEADS * RET_QK_DIM
RET_V_W = RET_HEADS * RET_V_DIM
ATT_W = ATT_HEADS * ATT_HEAD_DIM
IDX_Q_W = IDX_HEADS * IDX_DIM
SG_W = SG_GROUPS * SG_GROUP_DIM

LANES = 128
COUNT_ROWS = 64
KV_BUFFERS = 3
SEARCH_PERIOD = 8
ATTN_SPAN = 1
SOFTMAX_ROWS = 64
VMEM_LIMIT_BYTES = 56 * 1024 * 1024

PROJ_TILE = 512
W_A_COLS = 2 * RET_QK_W + 2 * RET_V_W + 3 * ATT_W + IDX_Q_W + IDX_DIM + IDX_HEADS
W_A_PAD = 5120
A1_QR, A1_KR, A1_VR, A1_GR, A1_QA = 0, 384, 768, 1536, 2304
A1_W = 3072
KV_K, KV_V = 0, 768
KV_W = 1536
QI_Q, QI_KIW = 0, 256
QI_W = 512
SG_U, SG_V = 0, 512
SG_PW = 1024

INT_MIN = np.int32(-2 ** 31)
INT_MAX = np.int32(2 ** 31 - 1)
NEG_INF = float("-inf")
LOG2E = math.log2(math.e)


def _cparams(sem):
    return pltpu.CompilerParams(dimension_semantics=sem, vmem_limit_bytes=VMEM_LIMIT_BYTES)


def _rms(x, g):
    return x * lax.rsqrt(jnp.mean(x * x, axis=-1, keepdims=True) + EPS) * g


def _ffn_kernel(x_ref, g_ref, wg_ref, wu_ref, wd_ref, *rest, final_norm):
    if final_norm:
        fg_ref, o_ref, xn_ref = rest
    else:
        o_ref, xn_ref = rest
    j = pl.program_id(1)

    @pl.when(j == 0)
    def _():
        xn_ref[...] = _rms(x_ref[...], g_ref[...]).astype(BF16)
        o_ref[...] = jnp.zeros_like(o_ref)

    xn = xn_ref[...]
    a = jnp.dot(xn, wg_ref[...], preferred_element_type=F32)
    b = jnp.dot(xn, wu_ref[...], preferred_element_type=F32)
    h = (a * jax.nn.sigmoid(a) * b).astype(BF16)
    o_ref[...] += jnp.dot(h, wd_ref[...], preferred_element_type=F32)

    @pl.when(j == pl.num_programs(1) - 1)
    def _():
        y = x_ref[...] + 0.5 * o_ref[...]
        if final_norm:
            y = _rms(y, fg_ref[...])
        o_ref[...] = y


def _ffn(x2, gain, wg, wu, wd, final_gain=None, *, tm, tf):
    t, d = x2.shape
    f = wg.shape[1]
    final_norm = final_gain is not None
    in_specs = [
        pl.BlockSpec((tm, d), lambda i, j: (i, 0)),
        pl.BlockSpec((1, d), lambda i, j: (0, 0)),
        pl.BlockSpec((d, tf), lambda i, j: (0, j)),
        pl.BlockSpec((d, tf), lambda i, j: (0, j)),
        pl.BlockSpec((tf, d), lambda i, j: (j, 0)),
    ]
    args = [x2, gain.reshape(1, d), wg, wu, wd]
    if final_norm:
        in_specs.append(pl.BlockSpec((1, d), lambda i, j: (0, 0)))
        args.append(final_gain.reshape(1, d))
    return pl.pallas_call(
        functools.partial(_ffn_kernel, final_norm=final_norm),
        out_shape=jax.ShapeDtypeStruct((t, d), F32),
        grid=(t // tm, f // tf),
        in_specs=in_specs,
        out_specs=pl.BlockSpec((tm, d), lambda i, j: (i, 0)),
        scratch_shapes=[pltpu.VMEM((tm, d), BF16)],
        compiler_params=_cparams(("parallel", "arbitrary")),
        name="ffn",
    )(*args)


def _proj_kernel(x_ref, g_ref, wa_ref, wb_ref, oa1_ref, okv_ref, oqi_ref, osg_ref, og_ref, xn_ref, *, ends):
    j = pl.program_id(1)
    e_a1, e_kv, e_qi, e_sg = ends

    @pl.when(j == 0)
    def _():
        xn_ref[...] = _rms(x_ref[...], g_ref[...]).astype(BF16)

    def tile(w_ref):
        return jnp.dot(xn_ref[...], w_ref[...], preferred_element_type=F32).astype(BF16)

    @pl.when(j < e_a1)
    def _():
        oa1_ref[...] = tile(wa_ref)

    @pl.when(jnp.logical_and(j >= e_a1, j < e_kv))
    def _():
        okv_ref[...] = tile(wa_ref)

    @pl.when(jnp.logical_and(j >= e_kv, j < e_qi))
    def _():
        oqi_ref[...] = tile(wa_ref)

    @pl.when(jnp.logical_and(j >= e_qi, j < e_sg))
    def _():
        osg_ref[...] = tile(wb_ref)

    @pl.when(j >= e_sg)
    def _():
        og_ref[0] = tile(wb_ref)


def _proj(x2, gain, w_a, w_b, *, tm):
    t, d = x2.shape
    tn = PROJ_TILE
    n_a, n_b = w_a.shape[1] // tn, w_b.shape[1] // tn
    e_a1 = A1_W // tn
    e_kv = e_a1 + KV_W // tn
    e_qi = e_kv + QI_W // tn
    e_sg = e_qi + SG_PW // tn
    assert e_qi == n_a
    t_g = n_a + n_b - e_sg

    def clampspec(first, count):
        return pl.BlockSpec((tm, tn), lambda i, j: (i, jnp.clip(j - first, 0, count - 1)))

    return pl.pallas_call(
        functools.partial(_proj_kernel, ends=(e_a1, e_kv, e_qi, e_sg)),
        out_shape=(jax.ShapeDtypeStruct((t, A1_W), BF16),
                   jax.ShapeDtypeStruct((t, KV_W), BF16),
                   jax.ShapeDtypeStruct((t, QI_W), BF16),
                   jax.ShapeDtypeStruct((t, SG_PW), BF16),
                   jax.ShapeDtypeStruct((t_g, t, tn), BF16)),
        grid=(t // tm, n_a + n_b),
        in_specs=[
            pl.BlockSpec((tm, d), lambda i, j: (i, 0)),
            pl.BlockSpec((1, d), lambda i, j: (0, 0)),
            pl.BlockSpec((d, tn), lambda i, j: (0, jnp.clip(j, 0, n_a - 1))),
            pl.BlockSpec((d, tn), lambda i, j: (0, jnp.clip(j - n_a, 0, n_b - 1))),
        ],
        out_specs=(clampspec(0, e_a1), clampspec(e_a1, e_kv - e_a1), clampspec(e_kv, e_qi - e_kv),
                   clampspec(e_qi, e_sg - e_qi),
                   pl.BlockSpec((1, tm, tn), lambda i, j: (jnp.clip(j - e_sg, 0, t_g - 1), i, 0))),
        scratch_shapes=[pltpu.VMEM((tm, d), BF16)],
        compiler_params=_cparams(("arbitrary", "arbitrary")),
        name="proj",
    )(x2, gain.reshape(1, d), w_a, w_b)


def _rope_table_kernel(pos_ref, invf_ref, sign_ref, c_ref, s_ref):
    ang = pos_ref[0].astype(F32) * invf_ref[...]
    c_ref[0] = jnp.cos(ang)
    s_ref[0] = jnp.sin(ang) * sign_ref[...]


def _rope_tables(positions, *, tm):
    b, s = positions.shape
    half = RET_QK_DIM // 2
    inv_freq = ROPE_BASE ** (-jnp.arange(0, RET_QK_DIM, 2, dtype=F32) / RET_QK_DIM)
    lane = np.arange(LANES)
    invf = inv_freq[lane % half].reshape(1, LANES)
    sign = jnp.asarray(np.where(lane % RET_QK_DIM < half, -1.0, 1.0), F32).reshape(1, LANES)
    out = jax.ShapeDtypeStruct((b, s, LANES), F32)
    return pl.pallas_call(
        _rope_table_kernel,
        out_shape=(out, out),
        grid=(b, s // tm),
        in_specs=[
            pl.BlockSpec((1, tm, 1), lambda i, j: (i, j, 0)),
            pl.BlockSpec((1, LANES), lambda i, j: (0, 0)),
            pl.BlockSpec((1, LANES), lambda i, j: (0, 0)),
        ],
        out_specs=(pl.BlockSpec((1, tm, LANES), lambda i, j: (i, j, 0)),
                   pl.BlockSpec((1, tm, LANES), lambda i, j: (i, j, 0))),
        compiler_params=_cparams(("parallel", "parallel")),
        name="rope_tables",
    )(positions.reshape(b, s, 1), invf, sign)


def _retention_constants():
    c = RET_CHUNK
    h = np.arange(RET_HEADS, dtype=np.float64)
    log_g = np.log(1.0 - 2.0 ** (-5.0 - h))
    idx = np.arange(c, dtype=np.float64)
    rel = idx[:, None] - idx[None, :]
    dmat = np.where(rel >= 0, np.exp(np.maximum(rel, 0.0) * log_g[:, None, None]), 0.0)
    k_decay = np.exp((c - 1 - idx)[None, :] * log_g[:, None])
    q_decay = np.exp((idx + 1)[None, :] * log_g[:, None])
    chunk_decay = np.exp(c * log_g)
    kd = np.repeat(k_decay.T, RET_QK_DIM, axis=1)
    qd = np.repeat(q_decay.T, RET_QK_DIM, axis=1) * RET_QK_DIM ** -0.5
    lane = np.arange(RET_QK_W)
    src = (lane // RET_QK_DIM) * RET_QK_DIM + (lane % RET_QK_DIM + RET_QK_DIM // 2) % RET_QK_DIM
    perm = np.zeros((RET_QK_W, RET_QK_W), np.float32)
    perm[src, lane] = 1.0
    return (jnp.asarray(dmat, F32), jnp.asarray(qd, F32), jnp.asarray(kd, F32),
            [float(np.float32(v)) for v in chunk_decay], jnp.asarray(perm, BF16))


def _gelu_tanh(x):
    return 0.5 * x * (1.0 + jnp.tanh(math.sqrt(2.0 / math.pi) * (x + 0.044715 * (x * x * x))))


def _local_kernel(p_ref, psg_ref, cos_ref, sin_ref,
                  perm_ref, dmat_ref, qd_ref, kd_ref, rn_ref, sn_ref, sgw_ref, sgb_ref,
                  or_ref, os_ref, state_ref, *, chunk_decay):
    @pl.when(pl.program_id(1) == 0)
    def _():
        state_ref[...] = jnp.zeros_like(state_ref)

    n_rep = RET_QK_W // LANES
    cos3 = jnp.concatenate([cos_ref[0]] * n_rep, axis=1)
    sin3 = jnp.concatenate([sin_ref[0]] * n_rep, axis=1)
    q = p_ref[0, :, A1_QR:A1_QR + RET_QK_W]
    k = p_ref[0, :, A1_KR:A1_KR + RET_QK_W]
    perm = perm_ref[...]
    qr = q.astype(F32) * cos3 + jnp.dot(q, perm, preferred_element_type=F32) * sin3
    kr = k.astype(F32) * cos3 + jnp.dot(k, perm, preferred_element_type=F32) * sin3
    q_in = (qr * (RET_QK_DIM ** -0.5)).astype(BF16)
    q_cr = (qr * qd_ref[...]).astype(BF16)
    k_b = kr.astype(BF16)
    k_dec_t = jnp.transpose(kr * kd_ref[...]).astype(BF16)
    v = p_ref[0, :, A1_VR:A1_VR + RET_V_W]
    outs = []
    for h in range(RET_HEADS):
        qs = slice(h * RET_QK_DIM, (h + 1) * RET_QK_DIM)
        vh = v[:, h * RET_V_DIM:(h + 1) * RET_V_DIM]
        a = lax.dot_general(q_in[:, qs], k_b[:, qs], (((1,), (1,)), ((), ())),
                            preferred_element_type=F32) * dmat_ref[h]
        prev = state_ref[h]
        o = (jnp.dot(a.astype(BF16), vh, preferred_element_type=F32)
             + jnp.dot(q_cr[:, qs], prev.astype(BF16), preferred_element_type=F32))
        state_ref[h] = prev * chunk_decay[h] + jnp.dot(k_dec_t[qs, :], vh, preferred_element_type=F32)
        mu = jnp.mean(o, axis=-1, keepdims=True)
        d = o - mu
        var = jnp.mean(d * d, axis=-1, keepdims=True)
        outs.append(d * lax.rsqrt(var + EPS))
    g = p_ref[0, :, A1_GR:A1_GR + RET_V_W].astype(F32)
    y = jnp.concatenate(outs, axis=1) * rn_ref[...]
    or_ref[0] = (y * (g * jax.nn.sigmoid(g))).astype(BF16)

    u = _gelu_tanh(psg_ref[0, :, SG_U:SG_U + SG_W].astype(F32))
    vn = _rms(_gelu_tanh(psg_ref[0, :, SG_V:SG_V + SG_W].astype(F32)), sn_ref[...])
    vn_b = vn.astype(BF16)
    c = SG_CHUNK
    tril = lax.broadcasted_iota(I32, (c, c), 0) >= lax.broadcasted_iota(I32, (c, c), 1)
    bias = sgb_ref[...]
    mixed = []
    for gi in range(SG_GROUPS):
        w = jnp.where(tril, sgw_ref[gi], jnp.zeros((), BF16))
        m = jnp.dot(w, vn_b[:, gi * SG_GROUP_DIM:(gi + 1) * SG_GROUP_DIM], preferred_element_type=F32)
        mixed.append(m + bias[:, gi:gi + 1])
    os_ref[0] = (u * jnp.concatenate(mixed, axis=1)).astype(BF16)


def _local(p_a13, p_sg3, cos_t, sin_t, ret_norm, sg_norm, sg_w, sg_b, consts):
    b, s, _ = p_a13.shape
    c = RET_CHUNK
    dmat, qd, kd, chunk_decay, perm = consts

    def whole(arr):
        nd = arr.ndim
        return pl.BlockSpec(arr.shape, lambda i, j: (0,) * nd)

    rn = ret_norm.reshape(1, RET_V_W)
    sn = sg_norm.reshape(1, SG_W)
    sgw = sg_w.astype(BF16)
    sgb_t = jnp.transpose(sg_b)
    tab = pl.BlockSpec((1, c, LANES), lambda i, j: (i, j, 0))
    return pl.pallas_call(
        functools.partial(_local_kernel, chunk_decay=chunk_decay),
        out_shape=(jax.ShapeDtypeStruct((b, s, RET_V_W), BF16),
                   jax.ShapeDtypeStruct((b, s, SG_W), BF16)),
        grid=(b, s // c),
        in_specs=[
            pl.BlockSpec((1, c, A1_W), lambda i, j: (i, j, 0)),
            pl.BlockSpec((1, c, SG_PW), lambda i, j: (i, j, 0)),
            tab, tab,
            whole(perm), whole(dmat), whole(qd), whole(kd), whole(rn), whole(sn), whole(sgw), whole(sgb_t),
        ],
        out_specs=(pl.BlockSpec((1, c, RET_V_W), lambda i, j: (i, j, 0)),
                   pl.BlockSpec((1, c, SG_W), lambda i, j: (i, j, 0))),
        scratch_shapes=[pltpu.VMEM((RET_HEADS, RET_QK_DIM, RET_V_DIM), F32)],
        compiler_params=_cparams(("parallel", "arbitrary")),
        name="local_mixers",
    )(p_a13, p_sg3, cos_t, sin_t, perm, dmat, qd, kd, rn, sn, sgw, sgb_t)


def _t5_bucket_table():
    max_exact = REL_BUCKETS // 2
    d = np.arange(REL_MAX_DIST)
    df = np.maximum(d, 1).astype(np.float32)
    large = max_exact + (np.log(df / max_exact) / np.float32(math.log(REL_MAX_DIST / max_exact))
                         * (REL_BUCKETS - max_exact)).astype(np.int32)
    large = np.minimum(large, REL_BUCKETS - 1)
    return np.where(d < max_exact, d, large)


def _attn_kernel(qmin_ref, kmax_ref, qmin_sub_ref, kmax_sub_ref,
                 qa_ref, pqi_ref, kiw_ref, kv_hbm, posq_ref, posk_ref, btab_ref,
                 o_ref, kv_buf, kv_sem, keys_ref, thr_ref, aux_ref, cnt_ref, m_ref, acc_ref, bias_ref,
                 s_ref, p_ref, alpha_ref,
                 *, bq, bk, n_keep, seq, far_dist):
    b = pl.program_id(0)
    qi = pl.program_id(1)
    n_chunks = (qi * bq + bq - 1) // bk + 1
    nlc = bk // LANES
    bka = bk * ATTN_SPAN
    n_att = (n_chunks + ATTN_SPAN - 1) // ATTN_SPAN

    def kv_copy(ca, slot):
        return pltpu.make_async_copy(kv_hbm.at[b, pl.ds(ca * bka, bka), :], kv_buf.at[slot], kv_sem.at[slot])

    for i in range(KV_BUFFERS - 1):
        @pl.when(i < n_att)
        def _():
            kv_copy(i, i).start()
    nsub_q, nsub_k = bq // LANES, bk // LANES
    k_f = float(n_keep)

    def tile_lanes(x, n=None):
        return jnp.concatenate([x] * (nlc if n is None else n), axis=1)

    def lane_chunk(x, j):
        return x[:, j * LANES:(j + 1) * LANES]

    def _select():
        qiv = pqi_ref[0, :, QI_Q:QI_Q + IDX_Q_W]
        zpad = jnp.zeros((bq, LANES - IDX_DIM), BF16)
        q_heads = [jnp.concatenate([qiv[:, h * IDX_DIM:(h + 1) * IDX_DIM], zpad], axis=1)
                   for h in range(IDX_HEADS)]
        kiw_q = pqi_ref[0, :, QI_KIW:QI_KIW + LANES]
        w = kiw_q[:, IDX_DIM:IDX_DIM + IDX_HEADS].astype(F32)
        w_bc = [jnp.broadcast_to(w[:, h:h + 1], (bq, bk)) for h in range(IDX_HEADS)]
        row_t = qi * bq + lax.broadcasted_iota(I32, (bq, bk), 0)
        lane_s = lax.broadcasted_iota(I32, (bq, bk), 1)

        def score_body(cc, carry, diagonal):
            off = pl.multiple_of(cc * bk, bk)
            kc = kiw_ref[0, pl.ds(off, bk), :]
            sc = jnp.zeros((bq, bk), F32)
            for h in range(IDX_HEADS):
                sh = lax.dot_general(q_heads[h], kc, (((1,), (1,)), ((), ())), preferred_element_type=F32)
                sc = sc + w_bc[h] * jnp.maximum(sh, 0.0)
            sc = jnp.where(sc == 0.0, 0.0, sc)
            bits = lax.bitcast_convert_type(sc, I32)
            key = jnp.where(bits < 0, bits ^ INT_MAX, bits)
            if diagonal:
                key = jnp.where(cc * bk + lane_s <= row_t, key, INT_MIN)
            keys_ref[cc] = key
            for r in range(bq // COUNT_ROWS):
                rows = slice(r * COUNT_ROWS, (r + 1) * COUNT_ROWS)
                m1 = thr_ref[rows, :]
                m2 = aux_ref[rows, :]
                for j in range(nlc):
                    x = key[rows, j * LANES:(j + 1) * LANES]
                    m2 = jnp.maximum(m2, jnp.minimum(m1, x))
                    m1 = jnp.maximum(m1, x)
                thr_ref[rows, :] = m1
                aux_ref[rows, :] = m2
            return carry

        thr_ref[...] = jnp.full((bq, LANES), INT_MIN, I32)
        aux_ref[...] = jnp.full((bq, LANES), INT_MIN, I32)
        n_below = jnp.minimum((qi * bq + 1) // bk, n_chunks)
        lax.fori_loop(0, n_below, functools.partial(score_body, diagonal=False), 0)
        lax.fori_loop(n_below, n_chunks, functools.partial(score_body, diagonal=True), 0)
        top1, top2 = thr_ref[...], aux_ref[...]

        nt = bq // LANES

        def to_rows(c):
            return jnp.concatenate(
                [jnp.transpose(jnp.broadcast_to(c[k:k + 1, :], (LANES, LANES))) for k in range(nt)], axis=0)

        def from_rows(x, op):
            return jnp.concatenate(
                [op(jnp.transpose(x[k * LANES:(k + 1) * LANES, :]), axis=0, keepdims=True) for k in range(nt)], axis=0)

        def count_ge(thr):
            thr_ref[...] = to_rows(thr)
            cnt_ref[...] = jnp.zeros((bq, LANES), F32)

            def body(cc, carry):
                for r in range(bq // COUNT_ROWS):
                    rows = pl.ds(r * COUNT_ROWS, COUNT_ROWS)
                    t = keys_ref[cc, rows, :]
                    th = thr_ref[rows, :]
                    acc = cnt_ref[rows, :]
                    for j in range(nlc):
                        acc = acc + jnp.where(lane_chunk(t, j) >= th, 1.0, 0.0)
                    cnt_ref[rows, :] = acc
                return carry

            lax.fori_loop(0, n_chunks, body, 0)
            return from_rows(cnt_ref[...], jnp.sum)

        def key_value(k):
            return lax.bitcast_convert_type(jnp.where(k < 0, k ^ INT_MAX, k), F32)

        def value_key(v):
            bits = lax.bitcast_convert_type(v, I32)
            return jnp.where(bits < 0, bits ^ INT_MAX, bits)

        lo0 = from_rows(top2, jnp.min)
        top = from_rows(top1, jnp.max)
        hi0 = jnp.where(top == INT_MAX, INT_MAX, top + 1)
        log2_k = math.log2(k_f)

        def excess(cnt):
            return jnp.log2(jnp.maximum(cnt, 0.5)) - log2_k

        def resolved(lo, hi, c_lo):
            return jnp.logical_or(hi - 1 <= lo, c_lo == k_f)

        def probe(mid, lo, hi, c_lo, c_hi, f_lo, f_hi, last):
            cnt = count_ge(mid)
            ok = cnt >= k_f
            f_new = excess(cnt)
            f_hi = jnp.where(jnp.logical_and(ok, last > 0.0), 0.5 * f_hi, f_hi)
            f_lo = jnp.where(jnp.logical_and(jnp.logical_not(ok), last < 0.0), 0.5 * f_lo, f_lo)
            return (jnp.where(ok, mid, lo), jnp.where(ok, hi, mid), jnp.where(ok, cnt, c_lo), jnp.where(ok, c_hi, cnt),
                    jnp.where(ok, f_new, f_lo), jnp.where(ok, f_hi, f_new), jnp.where(ok, 1.0, -1.0))

        def search_step(state):
            it, lo, hi, c_lo, c_hi, f_lo, f_hi, last, _ = state
            done = resolved(lo, hi, c_lo)
            mid_b = (lo & hi) + ((lo ^ hi) >> 1)
            v_lo = key_value(lo)
            mid_i = jnp.clip(value_key(v_lo + f_lo / (f_lo - f_hi) * (key_value(hi) - v_lo)), lo + 1, hi - 1)
            mid = jnp.where(it % SEARCH_PERIOD == SEARCH_PERIOD - 1, mid_b, mid_i)
            mid = jnp.where(done, lo, mid)
            old = (lo, hi, c_lo, c_hi, f_lo, f_hi, last)
            new = tuple(jnp.where(done, o, u) for o, u in zip(old, probe(mid, *old)))
            pending = jnp.max(jnp.where(resolved(new[0], new[1], new[2]), 0.0, 1.0))
            return (it + 1,) + new + (pending,)

        c_nominal = jnp.full((nt, LANES), 8.0 * k_f, F32)
        zero_c = jnp.zeros((nt, LANES), F32)
        state = (lo0, hi0, c_nominal, zero_c, excess(c_nominal), excess(zero_c), zero_c)
        for z in (0, 1):
            lo, hi = state[0], state[1]
            zk = jnp.full((nt, LANES), z, I32)
            inside = jnp.logical_and(jnp.logical_and(zk > lo, zk < hi), jnp.logical_not(resolved(lo, hi, state[2])))
            new = probe(jnp.where(inside, zk, lo), *state)
            state = tuple(jnp.where(inside, upd, old) for old, upd in zip(state, new))
        pending0 = jnp.max(jnp.where(resolved(state[0], state[1], state[2]), 0.0, 1.0))
        state = lax.while_loop(lambda st: st[8] > 0.0, search_step, (jnp.int32(0),) + state + (pending0,))
        tau_c, hi_c, c_hi = state[1], state[2], state[4]
        need_c = jnp.where(hi_c - 1 <= tau_c, k_f - c_hi, float(seq))
        need = to_rows(jnp.where(tau_c == INT_MIN, 0.0, need_c))
        tau = to_rows(tau_c)
        upper = lax.broadcasted_iota(I32, (LANES, LANES), 0) <= lax.broadcasted_iota(I32, (LANES, LANES), 1)
        scan = jnp.concatenate([jnp.where(upper, 1.0, 0.0), jnp.ones((LANES, LANES), F32)], axis=1).astype(BF16)
        cnt_ref[...] = jnp.zeros((bq, LANES), F32)

        def build_mask(cc, carry):
            t = keys_ref[cc]
            cols = []
            seen = cnt_ref[...]
            for j in range(nlc):
                tj = lane_chunk(t, j)
                tie = tj == tau
                pt = jnp.dot(jnp.where(tie, 1.0, 0.0).astype(BF16), scan, preferred_element_type=F32)
                keep_tie = jnp.logical_and(tie, seen + pt[:, :LANES] <= need)
                cols.append(jnp.where(jnp.logical_or(tj > tau, keep_tie), 0.0, NEG_INF))
                seen = seen + pt[:, LANES:]
            cnt_ref[...] = seen
            keys_ref[cc] = lax.bitcast_convert_type(jnp.concatenate(cols, axis=1), I32)
            return carry

        lax.fori_loop(0, n_chunks, build_mask, 0)

        def mask_out(cc, carry):
            keys_ref[cc] = lax.bitcast_convert_type(jnp.full((bq, bk), NEG_INF, F32), I32)
            return carry

        lax.fori_loop(n_chunks, n_att * ATTN_SPAN, mask_out, 0)

        m_ref[...] = jnp.full(m_ref.shape, NEG_INF, F32)
        acc_ref[...] = jnp.zeros_like(acc_ref)

    _select()

    c1 = (ATT_HEAD_DIM ** -0.5) * LOG2E
    q = qa_ref[0]
    ones_v = jnp.ones((bka, ATT_HEAD_DIM), BF16)

    def heads(near, ca, kk, vv):
        for h in range(ATT_HEADS):
            hs = slice(h * ATT_HEAD_DIM, (h + 1) * ATT_HEAD_DIM)
            s_ref[...] = lax.dot_general(q[:, hs], kk[:, hs], (((1,), (1,)), ((), ())),
                                         preferred_element_type=F32)
            for r in range(bq // SOFTMAX_ROWS):
                rows = slice(r * SOFTMAX_ROWS, (r + 1) * SOFTMAX_ROWS)
                mask = jnp.concatenate([keys_ref[ca * ATTN_SPAN + i, rows, :] for i in range(ATTN_SPAN)], axis=1)
                t = s_ref[rows, :] * c1 + lax.bitcast_convert_type(mask, F32)
                if near:
                    t = t + bias_ref[h, rows, :]
                m_old = m_ref[h, rows, :]
                m_new = jnp.maximum(m_old, jnp.max(t, axis=1, keepdims=True))
                m_safe = jnp.where(m_new == NEG_INF, 0.0, m_new)
                alpha_ref[rows, :] = jnp.exp2(m_old - m_safe)
                p_ref[rows, :] = jnp.exp2(t - tile_lanes(m_safe, bka // LANES)).astype(BF16)
                m_ref[h, rows, :] = m_new
            v_aug = jnp.concatenate([vv[:, hs], ones_v], axis=1)
            acc_ref[h] = (tile_lanes(alpha_ref[...], 2) * acc_ref[h]
                          + jnp.dot(p_ref[...], v_aug, preferred_element_type=F32))

    def attend(ca, carry):
        slot = ca % KV_BUFFERS
        nxt = ca + (KV_BUFFERS - 1)

        @pl.when(nxt < n_att)
        def _():
            kv_copy(nxt, nxt % KV_BUFFERS).start()

        kv_copy(ca, slot).wait()
        kk = kv_buf[slot, :, KV_K:KV_K + ATT_W]
        vv = kv_buf[slot, :, KV_V:KV_V + ATT_W]
        kmax = kmax_ref[b * (seq // bk) + ca * ATTN_SPAN]
        for i in range(1, ATTN_SPAN):
            kmax = jnp.maximum(kmax, kmax_ref[b * (seq // bk) + ca * ATTN_SPAN + i])
        far = qmin_ref[b * (seq // bq) + qi] - kmax >= far_dist

        @pl.when(far)
        def _():
            heads(False, ca, kk, vv)

        @pl.when(jnp.logical_not(far))
        def _():
            nsub = seq // LANES
            for r in range(nsub_q):
                for j in range(ATTN_SPAN * nsub_k):
                    rows = slice(r * LANES, (r + 1) * LANES)
                    cols = slice(j * LANES, (j + 1) * LANES)
                    pcols = slice((j % nsub_k) * LANES, (j % nsub_k + 1) * LANES)
                    sub_far = (qmin_sub_ref[b * nsub + qi * nsub_q + r]
                               - kmax_sub_ref[b * nsub + ca * ATTN_SPAN * nsub_k + j]) >= far_dist

                    @pl.when(sub_far)
                    def _():
                        for h in range(ATT_HEADS):
                            bias_ref[h, rows, cols] = jnp.zeros((LANES, LANES), F32)

                    @pl.when(jnp.logical_not(sub_far))
                    def _():
                        half = LANES // 2
                        for hr in range(2):
                            rr = slice(r * LANES + hr * half, r * LANES + (hr + 1) * half)
                            dist = posq_ref[0, rr, :] - posk_ref[0, ca * ATTN_SPAN + j // nsub_k, :, pcols]
                            idx = jnp.clip(dist, 0, REL_MAX_DIST - 1)
                            for h in range(ATT_HEADS):
                                row = jnp.broadcast_to(btab_ref[h:h + 1, :], (half, LANES))
                                bias_ref[h, rr, cols] = jnp.take_along_axis(row, idx, axis=1)

            heads(True, ca, kk, vv)

        return carry

    lax.fori_loop(0, n_att, attend, 0)

    outs = []
    for h in range(ATT_HEADS):
        a = acc_ref[h]
        outs.append(a[:, :ATT_HEAD_DIM] / a[:, ATT_HEAD_DIM:])
    o_ref[0] = jnp.concatenate(outs, axis=1).astype(BF16)


def _attn(p_a13, p_qi3, p_kv3, positions, rel_table, *, bq, bk):
    b, s, _ = p_qi3.shape
    assert bq % LANES == 0 and bk % LANES == 0
    n_keep = min(TOPK_MAX, s // 4)
    assert n_keep <= 2 * LANES
    assert (s // bk) % ATTN_SPAN == 0
    bka = bk * ATTN_SPAN
    nq, nkc, nsub = s // bq, s // bk, s // LANES
    qmin =jnp.min(positions.reshape(b, nq, bq), axis=-1).reshape(-1)
    kmax = jnp.max(positions.reshape(b, nkc, bk), axis=-1).reshape(-1)
    qmin_sub = jnp.min(positions.reshape(b, nsub, LANES), axis=-1).reshape(-1)
    kmax_sub = jnp.max(positions.reshape(b, nsub, LANES), axis=-1).reshape(-1)
    buckets = _t5_bucket_table()
    far_dist = REL_MAX_DIST - 1
    assert np.all(np.diff(buckets) >= 0) and buckets[far_dist] == REL_BUCKETS - 1
    relb = ((rel_table - rel_table[REL_BUCKETS - 1:REL_BUCKETS, :]) * LOG2E).astype(F32)
    btab = jnp.pad(jnp.transpose(relb[buckets, :]), ((0, 8 - ATT_HEADS), (0, 0)))

    grid_spec = pltpu.PrefetchScalarGridSpec(
        num_scalar_prefetch=4,
        grid=(b, nq),
        in_specs=[
            pl.BlockSpec((1, bq, ATT_W), lambda i, t, *_: (i, t, A1_QA // ATT_W)),
            pl.BlockSpec((1, bq, QI_W), lambda i, t, *_: (i, t, 0)),
            pl.BlockSpec((1, s, LANES), lambda i, t, *_: (i, 0, QI_KIW // LANES)),
            pl.BlockSpec(memory_space=pl.ANY),
            pl.BlockSpec((1, bq, 1), lambda i, t, *_: (i, t, 0)),
            pl.BlockSpec((1, nkc, 1, bk), lambda i, t, *_: (i, 0, 0, 0)),
            pl.BlockSpec((8, LANES), lambda i, t, *_: (0, 0)),
        ],
        out_specs=pl.BlockSpec((1, bq, ATT_W), lambda i, t, *_: (i, t, 0)),
        scratch_shapes=[
            pltpu.VMEM((KV_BUFFERS, bka, KV_W), BF16),
            pltpu.SemaphoreType.DMA((KV_BUFFERS,)),
            pltpu.VMEM((nkc, bq, bk), I32),
            pltpu.VMEM((bq, LANES), I32),
            pltpu.VMEM((bq, LANES), I32),
            pltpu.VMEM((bq, LANES), F32),
            pltpu.VMEM((ATT_HEADS, bq, LANES), F32),
            pltpu.VMEM((ATT_HEADS, bq, 2 * ATT_HEAD_DIM), F32),
            pltpu.VMEM((ATT_HEADS, bq, bka), F32),
            pltpu.VMEM((bq, bka), F32),
            pltpu.VMEM((bq, bka), BF16),
            pltpu.VMEM((bq, LANES), F32),
        ],
    )
    kern = functools.partial(_attn_kernel, bq=bq, bk=bk, n_keep=n_keep, seq=s,
                             far_dist=far_dist)
    return pl.pallas_call(
        kern,
        out_shape=jax.ShapeDtypeStruct((b, s, ATT_W), BF16),
        grid_spec=grid_spec,
        compiler_params=_cparams(("parallel", "arbitrary")),
        name="sparse_attn",
    )(qmin, kmax, qmin_sub, kmax_sub,
      p_a13, p_qi3, p_qi3, p_kv3, positions.reshape(b, s, 1), positions.reshape(b, nkc, 1, bk), btab)


def _merge_kernel(or_ref, oa_ref, os_ref, gr_ref, ga_ref, gs_ref, wr_ref, wa_ref, ws_ref, wo_ref, x_ref, o_ref):
    j = pl.program_id(1)

    @pl.when(j == 0)
    def _():
        o_ref[...] = jnp.zeros_like(o_ref)

    def branch(o, w, g):
        return jax.nn.sigmoid(g[0].astype(F32)) * jnp.dot(o[...], w[...], preferred_element_type=F32)

    merged = branch(or_ref, wr_ref, gr_ref) + branch(oa_ref, wa_ref, ga_ref) + branch(os_ref, ws_ref, gs_ref)
    o_ref[...] += jnp.dot(merged.astype(BF16), wo_ref[...], preferred_element_type=F32)

    @pl.when(j == pl.num_programs(1) - 1)
    def _():
        o_ref[...] = x_ref[...] + o_ref[...]


def _merge(o_r, o_a, o_s, p_g, wr, wa, ws, wo, x2, *, tm):
    t, d = x2.shape
    tn = PROJ_TILE
    per_gate = d // tn

    def gspec(g):
        return pl.BlockSpec((1, tm, tn), lambda i, j: (g * per_gate + j, i, 0))

    return pl.pallas_call(
        _merge_kernel,
        out_shape=jax.ShapeDtypeStruct((t, d), F32),
        grid=(t // tm, per_gate),
        in_specs=[
            pl.BlockSpec((tm, RET_V_W), lambda i, j: (i, 0)),
            pl.BlockSpec((tm, ATT_W), lambda i, j: (i, 0)),
            pl.BlockSpec((tm, SG_W), lambda i, j: (i, 0)),
            gspec(0), gspec(1), gspec(2),
            pl.BlockSpec((RET_V_W, tn), lambda i, j: (0, j)),
            pl.BlockSpec((ATT_W, tn), lambda i, j: (0, j)),
            pl.BlockSpec((SG_W, tn), lambda i, j: (0, j)),
            pl.BlockSpec((tn, d), lambda i, j: (j, 0)),
            pl.BlockSpec((tm, d), lambda i, j: (i, 0)),
        ],
        out_specs=pl.BlockSpec((tm, d), lambda i, j: (i, 0)),
        compiler_params=_cparams(("parallel", "arbitrary")),
        name="merge",
    )(o_r, o_a, o_s, p_g, p_g, p_g, wr, wa, ws, wo, x2)


def _split_w_in_kernel(w_ref, wa_ref, wb_ref):
    w = w_ref[0]
    wa_ref[:, :W_A_COLS] = w[:, :W_A_COLS].astype(BF16)
    wa_ref[:, W_A_COLS:] = jnp.zeros((w.shape[0], W_A_PAD - W_A_COLS), BF16)
    wb_ref[...] = w[:, W_A_COLS:].astype(BF16)


def _split_w_in(w_in, layer, d_model, *, tr=64):
    _, rows, cols = w_in.shape
    assert cols == W_A_COLS + 2 * SG_W + 3 * d_model and A1_W + KV_W + QI_W == W_A_PAD
    w_b_cols = cols - W_A_COLS
    assert w_b_cols % PROJ_TILE == 0 and rows % tr == 0
    return pl.pallas_call(
        _split_w_in_kernel,
        out_shape=(jax.ShapeDtypeStruct((rows, W_A_PAD), BF16), jax.ShapeDtypeStruct((rows, w_b_cols), BF16)),
        grid=(rows // tr,),
        in_specs=[pl.BlockSpec((1, tr, cols), lambda i: (layer, i, 0))],
        out_specs=(pl.BlockSpec((tr, W_A_PAD), lambda i: (i, 0)), pl.BlockSpec((tr, w_b_cols), lambda i: (i, 0))),
        compiler_params=_cparams(("parallel",)),
        name="split_w_in",
    )(w_in)


def _pick(n, prefs):
    for p in prefs:
        if n % p == 0:
            return p
    return n


def kernel(x, positions, rel_table, ffn1_norm, ffn1_w_gate, ffn1_w_up, ffn1_w_down, mix_norm, w_in, ret_norm, sg_norm, sg_w, sg_b, w_br_ret, w_br_att, w_br_sg, w_out, ffn2_norm, ffn2_w_gate, ffn2_w_up, ffn2_w_down, final_norm):
    b, s, d = x.shape
    t = b * s
    depth = w_in.shape[0]
    d_ff = ffn1_w_gate.shape[2]
    assert s % RET_CHUNK == 0 and d % PROJ_TILE == 0

    tm = _pick(t, (512, 256, 128))
    tm_proj = _pick(t, (1024, 512, 256, 128))
    tm_ffn = _pick(t, (512, 256, 128))
    tf = _pick(d_ff, (512, 256, 128))
    bq = _pick(s, (256, 128))
    bk = _pick(s, (512, 256, 128))

    cos_t, sin_t = _rope_tables(positions, tm=_pick(s, (2048, 1024, 512, 256, 128)))
    consts = _retention_constants()
    x2 = x.reshape(t, d)
    for l in range(depth):
        x2 = _ffn(x2, ffn1_norm[l], ffn1_w_gate[l].astype(BF16), ffn1_w_up[l].astype(BF16),
                  ffn1_w_down[l].astype(BF16), tm=tm_ffn, tf=tf)
        p_a1, p_kv, p_qi, p_sg, p_g = _proj(x2, mix_norm[l], *_split_w_in(w_in, l, d), tm=tm_proj)
        p_a13 = p_a1.reshape(b, s, -1)
        o_r, o_s = _local(p_a13, p_sg.reshape(b, s, -1), cos_t, sin_t, ret_norm[l], sg_norm[l], sg_w[l], sg_b[l],
                          consts)
        o_a = _attn(p_a13, p_qi.reshape(b, s, -1), p_kv.reshape(b, s, -1), positions, rel_table, bq=bq, bk=bk)
        x2 = _merge(o_r.reshape(t, -1), o_a.reshape(t, -1), o_s.reshape(t, -1), p_g,
                    w_br_ret[l].astype(BF16), w_br_att[l].astype(BF16), w_br_sg[l].astype(BF16),
                    w_out[l].astype(BF16), x2, tm=tm)
        x2 = _ffn(x2, ffn2_norm[l], ffn2_w_gate[l].astype(BF16), ffn2_w_up[l].astype(BF16),
                  ffn2_w_down[l].astype(BF16), final_norm if l == depth - 1 else None, tm=tm_ffn, tf=tf)
    return x2.reshape(b, s, d)
```

```python
import functools
import math

import numpy as np
import jax
import jax.numpy as jnp
from jax import lax
from jax.experimental import pallas as pl
from jax.experimental.pallas import tpu as pltpu

F32 = jnp.float32
BF16 = jnp.bfloat16
I32 = jnp.int32

RET_HEADS, RET_QK_DIM, RET_V_DIM, RET_CHUNK = 6, 64, 128, 128
ATT_HEADS, ATT_HEAD_DIM = 6, 128
IDX_HEADS, IDX_DIM = 4, 64
TOPK_MAX = 256
SG_GROUPS, SG_GROUP_DIM, SG_CHUNK = 4, 128, 128
REL_BUCKETS, REL_MAX_DIST = 32, 128
ROPE_BASE = 10000.0
EPS = 1e-6

RET_QK_W = RET_HEADS * RET_QK_DIM
RET_V_W = RET_HEADS * RET_V_DIM
ATT_W = ATT_HEADS * ATT_HEAD_DIM
IDX_Q_W = IDX_HEADS * IDX_DIM
SG_W = SG_GROUPS * SG_GROUP_DIM

LANES = 128
COUNT_ROWS = 64
KV_BUFFERS = 3
SEARCH_PERIOD = 8
ATTN_SPAN = 1
SOFTMAX_ROWS = 64
VMEM_LIMIT_BYTES = 56 * 1024 * 1024

PROJ_TILE = 512
W_A_COLS = 2 * RET_QK_W + 2 * RET_V_W + 3 * ATT_W + IDX_Q_W + IDX_DIM + IDX_HEADS
W_A_PAD = 5120
A1_QR, A1_KR, A1_VR, A1_GR, A1_QA = 0, 384, 768, 1536, 2304
A1_W = 3072
KV_K, KV_V = 0, 768
KV_W = 1536
QI_Q, QI_KIW = 0, 256
QI_W = 512
SG_U, SG_V = 0, 512
SG_PW = 1024

INT_MIN = np.int32(-2 ** 31)
INT_MAX = np.int32(2 ** 31 - 1)
NEG_INF = float("-inf")
LOG2E = math.log2(math.e)


def _cparams(sem):
    return pltpu.CompilerParams(dimension_semantics=sem, vmem_limit_bytes=VMEM_LIMIT_BYTES)


def _rms(x, g):
    return x * lax.rsqrt(jnp.mean(x * x, axis=-1, keepdims=True) + EPS) * g


def _ffn_kernel(x_ref, g_ref, wg_ref, wu_ref, wd_ref, *rest, final_norm):
    if final_norm:
        fg_ref, o_ref, xn_ref = rest
    else:
        o_ref, xn_ref = rest
    j = pl.program_id(1)

    @pl.when(j == 0)
    def _():
        xn_ref[...] = _rms(x_ref[...], g_ref[...]).astype(BF16)
        o_ref[...] = jnp.zeros_like(o_ref)

    xn = xn_ref[...]
    a = jnp.dot(xn, wg_ref[...], preferred_element_type=F32)
    b = jnp.dot(xn, wu_ref[...], preferred_element_type=F32)
    h = (a * jax.nn.sigmoid(a) * b).astype(BF16)
    o_ref[...] += jnp.dot(h, wd_ref[...], preferred_element_type=F32)

    @pl.when(j == pl.num_programs(1) - 1)
    def _():
        y = x_ref[...] + 0.5 * o_ref[...]
        if final_norm:
            y = _rms(y, fg_ref[...])
        o_ref[...] = y


def _ffn(x2, gain, wg, wu, wd, final_gain=None, *, tm, tf):
    t, d = x2.shape
    f = wg.shape[1]
    final_norm = final_gain is not None
    in_specs = [
        pl.BlockSpec((tm, d), lambda i, j: (i, 0)),
        pl.BlockSpec((1, d), lambda i, j: (0, 0)),
        pl.BlockSpec((d, tf), lambda i, j: (0, j)),
        pl.BlockSpec((d, tf), lambda i, j: (0, j)),
        pl.BlockSpec((tf, d), lambda i, j: (j, 0)),
    ]
    args = [x2, gain.reshape(1, d), wg, wu, wd]
    if final_norm:
        in_specs.append(pl.BlockSpec((1, d), lambda i, j: (0, 0)))
        args.append(final_gain.reshape(1, d))
    return pl.pallas_call(
        functools.partial(_ffn_kernel, final_norm=final_norm),
        out_shape=jax.ShapeDtypeStruct((t, d), F32),
        grid=(t // tm, f // tf),
        in_specs=in_specs,
        out_specs=pl.BlockSpec((tm, d), lambda i, j: (i, 0)),
        scratch_shapes=[pltpu.VMEM((tm, d), BF16)],
        compiler_params=_cparams(("parallel", "arbitrary")),
        name="ffn",
    )(*args)


def _proj_kernel(x_ref, g_ref, wa_ref, wb_ref, oa1_ref, okv_ref, oqi_ref, osg_ref, og_ref, xn_ref, *, ends):
    j = pl.program_id(1)
    e_a1, e_kv, e_qi, e_sg = ends

    @pl.when(j == 0)
    def _():
        xn_ref[...] = _rms(x_ref[...], g_ref[...]).astype(BF16)

    def tile(w_ref):
        return jnp.dot(xn_ref[...], w_ref[...], preferred_element_type=F32).astype(BF16)

    @pl.when(j < e_a1)
    def _():
        oa1_ref[...] = tile(wa_ref)

    @pl.when(jnp.logical_and(j >= e_a1, j < e_kv))
    def _():
        okv_ref[...] = tile(wa_ref)

    @pl.when(jnp.logical_and(j >= e_kv, j < e_qi))
    def _():
        oqi_ref[...] = tile(wa_ref)

    @pl.when(jnp.logical_and(j >= e_qi, j < e_sg))
    def _():
        osg_ref[...] = tile(wb_ref)

    @pl.when(j >= e_sg)
    def _():
        og_ref[0] = tile(wb_ref)


def _proj(x2, gain, w_a, w_b, *, tm):
    t, d = x2.shape
    tn = PROJ_TILE
    n_a, n_b = w_a.shape[1] // tn, w_b.shape[1] // tn
    e_a1 = A1_W // tn
    e_kv = e_a1 + KV_W // tn
    e_qi = e_kv + QI_W // tn
    e_sg = e_qi + SG_PW // tn
    assert e_qi == n_a
    t_g = n_a + n_b - e_sg

    def clampspec(first, count):
        return pl.BlockSpec((tm, tn), lambda i, j: (i, jnp.clip(j - first, 0, count - 1)))

    return pl.pallas_call(
        functools.partial(_proj_kernel, ends=(e_a1, e_kv, e_qi, e_sg)),
        out_shape=(jax.ShapeDtypeStruct((t, A1_W), BF16),
                   jax.ShapeDtypeStruct((t, KV_W), BF16),
                   jax.ShapeDtypeStruct((t, QI_W), BF16),
                   jax.ShapeDtypeStruct((t, SG_PW), BF16),
                   jax.ShapeDtypeStruct((t_g, t, tn), BF16)),
        grid=(t // tm, n_a + n_b),
        in_specs=[
            pl.BlockSpec((tm, d), lambda i, j: (i, 0)),
            pl.BlockSpec((1, d), lambda i, j: (0, 0)),
            pl.BlockSpec((d, tn), lambda i, j: (0, jnp.clip(j, 0, n_a - 1))),
            pl.BlockSpec((d, tn), lambda i, j: (0, jnp.clip(j - n_a, 0, n_b - 1))),
        ],
        out_specs=(clampspec(0, e_a1), clampspec(e_a1, e_kv - e_a1), clampspec(e_kv, e_qi - e_kv),
                   clampspec(e_qi, e_sg - e_qi),
                   pl.BlockSpec((1, tm, tn), lambda i, j: (jnp.clip(j - e_sg, 0, t_g - 1), i, 0))),
        scratch_shapes=[pltpu.VMEM((tm, d), BF16)],
        compiler_params=_cparams(("arbitrary", "arbitrary")),
        name="proj",
    )(x2, gain.reshape(1, d), w_a, w_b)


def _rope_table_kernel(pos_ref, invf_ref, sign_ref, c_ref, s_ref):
    ang = pos_ref[0].astype(F32) * invf_ref[...]
    c_ref[0] = jnp.cos(ang)
    s_ref[0] = jnp.sin(ang) * sign_ref[...]


def _rope_tables(positions, *, tm):
    b, s = positions.shape
    half = RET_QK_DIM // 2
    inv_freq = ROPE_BASE ** (-jnp.arange(0, RET_QK_DIM, 2, dtype=F32) / RET_QK_DIM)
    lane = np.arange(LANES)
    invf = inv_freq[lane % half].reshape(1, LANES)
    sign = jnp.asarray(np.where(lane % RET_QK_DIM < half, -1.0, 1.0), F32).reshape(1, LANES)
    out = jax.ShapeDtypeStruct((b, s, LANES), F32)
    return pl.pallas_call(
        _rope_table_kernel,
        out_shape=(out, out),
        grid=(b, s // tm),
        in_specs=[
            pl.BlockSpec((1, tm, 1), lambda i, j: (i, j, 0)),
            pl.BlockSpec((1, LANES), lambda i, j: (0, 0)),
            pl.BlockSpec((1, LANES), lambda i, j: (0, 0)),
        ],
        out_specs=(pl.BlockSpec((1, tm, LANES), lambda i, j: (i, j, 0)),
                   pl.BlockSpec((1, tm, LANES), lambda i, j: (i, j, 0))),
        compiler_params=_cparams(("parallel", "parallel")),
        name="rope_tables",
    )(positions.reshape(b, s, 1), invf, sign)


def _retention_constants():
    c = RET_CHUNK
    h = np.arange(RET_HEADS, dtype=np.float64)
    log_g = np.log(1.0 - 2.0 ** (-5.0 - h))
    idx = np.arange(c, dtype=np.float64)
    rel = idx[:, None] - idx[None, :]
    dmat = np.where(rel >= 0, np.exp(np.maximum(rel, 0.0) * log_g[:, None, None]), 0.0)
    k_decay = np.exp((c - 1 - idx)[None, :] * log_g[:, None])
    q_decay = np.exp((idx + 1)[None, :] * log_g[:, None])
    chunk_decay = np.exp(c * log_g)
    kd = np.repeat(k_decay.T, RET_QK_DIM, axis=1)
    qd = np.repeat(q_decay.T, RET_QK_DIM, axis=1) * RET_QK_DIM ** -0.5
    lane = np.arange(RET_QK_W)
    src = (lane // RET_QK_DIM) * RET_QK_DIM + (lane % RET_QK_DIM + RET_QK_DIM // 2) % RET_QK_DIM
    perm = np.zeros((RET_QK_W, RET_QK_W), np.float32)
    perm[src, lane] = 1.0
    return (jnp.asarray(dmat, F32), jnp.asarray(qd, F32), jnp.asarray(kd, F32),
            [float(np.float32(v)) for v in chunk_decay], jnp.asarray(perm, BF16))


def _gelu_tanh(x):
    return 0.5 * x * (1.0 + jnp.tanh(math.sqrt(2.0 / math.pi) * (x + 0.044715 * (x * x * x))))


def _local_kernel(p_ref, psg_ref, cos_ref, sin_ref,
                  perm_ref, dmat_ref, qd_ref, kd_ref, rn_ref, sn_ref, sgw_ref, sgb_ref,
                  or_ref, os_ref, state_ref, *, chunk_decay):
    @pl.when(pl.program_id(1) == 0)
    def _():
        state_ref[...] = jnp.zeros_like(state_ref)

    n_rep = RET_QK_W // LANES
    cos3 = jnp.concatenate([cos_ref[0]] * n_rep, axis=1)
    sin3 = jnp.concatenate([sin_ref[0]] * n_rep, axis=1)
    q = p_ref[0, :, A1_QR:A1_QR + RET_QK_W]
    k = p_ref[0, :, A1_KR:A1_KR + RET_QK_W]
    perm = perm_ref[...]
    qr = q.astype(F32) * cos3 + jnp.dot(q, perm, preferred_element_type=F32) * sin3
    kr = k.astype(F32) * cos3 + jnp.dot(k, perm, preferred_element_type=F32) * sin3
    q_in = (qr * (RET_QK_DIM ** -0.5)).astype(BF16)
    q_cr = (qr * qd_ref[...]).astype(BF16)
    k_b = kr.astype(BF16)
    k_dec_t = jnp.transpose(kr * kd_ref[...]).astype(BF16)
    v = p_ref[0, :, A1_VR:A1_VR + RET_V_W]
    outs = []
    for h in range(RET_HEADS):
        qs = slice(h * RET_QK_DIM, (h + 1) * RET_QK_DIM)
        vh = v[:, h * RET_V_DIM:(h + 1) * RET_V_DIM]
        a = lax.dot_general(q_in[:, qs], k_b[:, qs], (((1,), (1,)), ((), ())),
                            preferred_element_type=F32) * dmat_ref[h]
        prev = state_ref[h]
        o = (jnp.dot(a.astype(BF16), vh, preferred_element_type=F32)
             + jnp.dot(q_cr[:, qs], prev.astype(BF16), preferred_element_type=F32))
        state_ref[h] = prev * chunk_decay[h] + jnp.dot(k_dec_t[qs, :], vh, preferred_element_type=F32)
        mu = jnp.mean(o, axis=-1, keepdims=True)
        d = o - mu
        var = jnp.mean(d * d, axis=-1, keepdims=True)
        outs.append(d * lax.rsqrt(var + EPS))
    g = p_ref[0, :, A1_GR:A1_GR + RET_V_W].astype(F32)
    y = jnp.concatenate(outs, axis=1) * rn_ref[...]
    or_ref[0] = (y * (g * jax.nn.sigmoid(g))).astype(BF16)

    u = _gelu_tanh(psg_ref[0, :, SG_U:SG_U + SG_W].astype(F32))
    vn = _rms(_gelu_tanh(psg_ref[0, :, SG_V:SG_V + SG_W].astype(F32)), sn_ref[...])
    vn_b = vn.astype(BF16)
    c = SG_CHUNK
    tril = lax.broadcasted_iota(I32, (c, c), 0) >= lax.broadcasted_iota(I32, (c, c), 1)
    bias = sgb_ref[...]
    mixed = []
    for gi in range(SG_GROUPS):
        w = jnp.where(tril, sgw_ref[gi], jnp.zeros((), BF16))
        m = jnp.dot(w, vn_b[:, gi * SG_GROUP_DIM:(gi + 1) * SG_GROUP_DIM], preferred_element_type=F32)
        mixed.append(m + bias[:, gi:gi + 1])
    os_ref[0] = (u * jnp.concatenate(mixed, axis=1)).astype(BF16)


def _local(p_a13, p_sg3, cos_t, sin_t, ret_norm, sg_norm, sg_w, sg_b, consts):
    b, s, _ = p_a13.shape
    c = RET_CHUNK
    dmat, qd, kd, chunk_decay, perm = consts

    def whole(arr):
        nd = arr.ndim
        return pl.BlockSpec(arr.shape, lambda i, j: (0,) * nd)

    rn = ret_norm.reshape(1, RET_V_W)
    sn = sg_norm.reshape(1, SG_W)
    sgw = sg_w.astype(BF16)
    sgb_t = jnp.transpose(sg_b)
    tab = pl.BlockSpec((1, c, LANES), lambda i, j: (i, j, 0))
    return pl.pallas_call(
        functools.partial(_local_kernel, chunk_decay=chunk_decay),
        out_shape=(jax.ShapeDtypeStruct((b, s, RET_V_W), BF16),
                   jax.ShapeDtypeStruct((b, s, SG_W), BF16)),
        grid=(b, s // c),
        in_specs=[
            pl.BlockSpec((1, c, A1_W), lambda i, j: (i, j, 0)),
            pl.BlockSpec((1, c, SG_PW), lambda i, j: (i, j, 0)),
            tab, tab,
            whole(perm), whole(dmat), whole(qd), whole(kd), whole(rn), whole(sn), whole(sgw), whole(sgb_t),
        ],
        out_specs=(pl.BlockSpec((1, c, RET_V_W), lambda i, j: (i, j, 0)),
                   pl.BlockSpec((1, c, SG_W), lambda i, j: (i, j, 0))),
        scratch_shapes=[pltpu.VMEM((RET_HEADS, RET_QK_DIM, RET_V_DIM), F32)],
        compiler_params=_cparams(("parallel", "arbitrary")),
        name="local_mixers",
    )(p_a13, p_sg3, cos_t, sin_t, perm, dmat, qd, kd, rn, sn, sgw, sgb_t)


def _t5_bucket_table():
    max_exact = REL_BUCKETS // 2
    d = np.arange(REL_MAX_DIST)
    df = np.maximum(d, 1).astype(np.float32)
    large = max_exact + (np.log(df / max_exact) / np.float32(math.log(REL_MAX_DIST / max_exact))
                         * (REL_BUCKETS - max_exact)).astype(np.int32)
    large = np.minimum(large, REL_BUCKETS - 1)
    return np.where(d < max_exact, d, large)


def _attn_kernel(qmin_ref, kmax_ref, qmin_sub_ref, kmax_sub_ref,
                 qa_ref, pqi_ref, kiw_ref, kv_hbm, posq_ref, posk_ref, btab_ref,
                 o_ref, kv_buf, kv_sem, keys_ref, thr_ref, aux_ref, cnt_ref, m_ref, acc_ref, bias_ref,
                 s_ref, p_ref, alpha_ref,
                 *, bq, bk, n_keep, seq, far_dist):
    b = pl.program_id(0)
    qi = pl.program_id(1)
    n_chunks = (qi * bq + bq - 1) // bk + 1
    nlc = bk // LANES
    bka = bk * ATTN_SPAN
    n_att = (n_chunks + ATTN_SPAN - 1) // ATTN_SPAN

    def kv_copy(ca, slot):
        return pltpu.make_async_copy(kv_hbm.at[b, pl.ds(ca * bka, bka), :], kv_buf.at[slot], kv_sem.at[slot])

    for i in range(KV_BUFFERS - 1):
        @pl.when(i < n_att)
        def _():
            kv_copy(i, i).start()
    nsub_q, nsub_k = bq // LANES, bk // LANES
    k_f = float(n_keep)

    def tile_lanes(x, n=None):
        return jnp.concatenate([x] * (nlc if n is None else n), axis=1)

    def lane_chunk(x, j):
        return x[:, j * LANES:(j + 1) * LANES]

    def _select():
        qiv = pqi_ref[0, :, QI_Q:QI_Q + IDX_Q_W]
        zpad = jnp.zeros((bq, LANES - IDX_DIM), BF16)
        q_heads = [jnp.concatenate([qiv[:, h * IDX_DIM:(h + 1) * IDX_DIM], zpad], axis=1)
                   for h in range(IDX_HEADS)]
        kiw_q = pqi_ref[0, :, QI_KIW:QI_KIW + LANES]
        w = kiw_q[:, IDX_DIM:IDX_DIM + IDX_HEADS].astype(F32)
        w_bc = [jnp.broadcast_to(w[:, h:h + 1], (bq, bk)) for h in range(IDX_HEADS)]
        row_t = qi * bq + lax.broadcasted_iota(I32, (bq, bk), 0)
        lane_s = lax.broadcasted_iota(I32, (bq, bk), 1)

        def score_body(cc, carry, diagonal):
            off = pl.multiple_of(cc * bk, bk)
            kc = kiw_ref[0, pl.ds(off, bk), :]
            sc = jnp.zeros((bq, bk), F32)
            for h in range(IDX_HEADS):
                sh = lax.dot_general(q_heads[h], kc, (((1,), (1,)), ((), ())), preferred_element_type=F32)
                sc = sc + w_bc[h] * jnp.maximum(sh, 0.0)
            sc = jnp.where(sc == 0.0, 0.0, sc)
            bits = lax.bitcast_convert_type(sc, I32)
            key = jnp.where(bits < 0, bits ^ INT_MAX, bits)
            if diagonal:
                key = jnp.where(cc * bk + lane_s <= row_t, key, INT_MIN)
            keys_ref[cc] = key
            for r in range(bq // COUNT_ROWS):
                rows = slice(r * COUNT_ROWS, (r + 1) * COUNT_ROWS)
                m1 = thr_ref[rows, :]
                m2 = aux_ref[rows, :]
                for j in range(nlc):
                    x = key[rows, j * LANES:(j + 1) * LANES]
                    m2 = jnp.maximum(m2, jnp.minimum(m1, x))
                    m1 = jnp.maximum(m1, x)
                thr_ref[rows, :] = m1
                aux_ref[rows, :] = m2
            return carry

        thr_ref[...] = jnp.full((bq, LANES), INT_MIN, I32)
        aux_ref[...] = jnp.full((bq, LANES), INT_MIN, I32)
        n_below = jnp.minimum((qi * bq + 1) // bk, n_chunks)
        lax.fori_loop(0, n_below, functools.partial(score_body, diagonal=False), 0)
        lax.fori_loop(n_below, n_chunks, functools.partial(score_body, diagonal=True), 0)
        top1, top2 = thr_ref[...], aux_ref[...]

        nt = bq // LANES

        def to_rows(c):
            return jnp.concatenate(
                [jnp.transpose(jnp.broadcast_to(c[k:k + 1, :], (LANES, LANES))) for k in range(nt)], axis=0)

        def from_rows(x, op):
            return jnp.concatenate(
                [op(jnp.transpose(x[k * LANES:(k + 1) * LANES, :]), axis=0, keepdims=True) for k in range(nt)], axis=0)

        def count_ge(thr):
            thr_ref[...] = to_rows(thr)
            cnt_ref[...] = jnp.zeros((bq, LANES), F32)

            def body(cc, carry):
                for r in range(bq // COUNT_ROWS):
                    rows = pl.ds(r * COUNT_ROWS, COUNT_ROWS)
                    t = keys_ref[cc, rows, :]
                    th = thr_ref[rows, :]
                    acc = cnt_ref[rows, :]
                    for j in range(nlc):
                        acc = acc + jnp.where(lane_chunk(t, j) >= th, 1.0, 0.0)
                    cnt_ref[rows, :] = acc
                return carry

            lax.fori_loop(0, n_chunks, body, 0)
            return from_rows(cnt_ref[...], jnp.sum)

        def key_value(k):
            return lax.bitcast_convert_type(jnp.where(k < 0, k ^ INT_MAX, k), F32)

        def value_key(v):
            bits = lax.bitcast_convert_type(v, I32)
            return jnp.where(bits < 0, bits ^ INT_MAX, bits)

        lo0 = from_rows(top2, jnp.min)
        top = from_rows(top1, jnp.max)
        hi0 = jnp.where(top == INT_MAX, INT_MAX, top + 1)
        log2_k = math.log2(k_f)

        def excess(cnt):
            return jnp.log2(jnp.maximum(cnt, 0.5)) - log2_k

        def resolved(lo, hi, c_lo):
            return jnp.logical_or(hi - 1 <= lo, c_lo == k_f)

        def probe(mid, lo, hi, c_lo, c_hi, f_lo, f_hi, last):
            cnt = count_ge(mid)
            ok = cnt >= k_f
            f_new = excess(cnt)
            f_hi = jnp.where(jnp.logical_and(ok, last > 0.0), 0.5 * f_hi, f_hi)
            f_lo = jnp.where(jnp.logical_and(jnp.logical_not(ok), last < 0.0), 0.5 * f_lo, f_lo)
            return (jnp.where(ok, mid, lo), jnp.where(ok, hi, mid), jnp.where(ok, cnt, c_lo), jnp.where(ok, c_hi, cnt),
                    jnp.where(ok, f_new, f_lo), jnp.where(ok, f_hi, f_new), jnp.where(ok, 1.0, -1.0))

        def search_step(state):
            it, lo, hi, c_lo, c_hi, f_lo, f_hi, last, _ = state
            done = resolved(lo, hi, c_lo)
            mid_b = (lo & hi) + ((lo ^ hi) >> 1)
            v_lo = key_value(lo)
            mid_i = jnp.clip(value_key(v_lo + f_lo / (f_lo - f_hi) * (key_value(hi) - v_lo)), lo + 1, hi - 1)
            mid = jnp.where(it % SEARCH_PERIOD == SEARCH_PERIOD - 1, mid_b, mid_i)
            mid = jnp.where(done, lo, mid)
            old = (lo, hi, c_lo, c_hi, f_lo, f_hi, last)
            new = tuple(jnp.where(done, o, u) for o, u in zip(old, probe(mid, *old)))
            pending = jnp.max(jnp.where(resolved(new[0], new[1], new[2]), 0.0, 1.0))
            return (it + 1,) + new + (pending,)

        c_nominal = jnp.full((nt, LANES), 8.0 * k_f, F32)
        zero_c = jnp.zeros((nt, LANES), F32)
        state = (lo0, hi0, c_nominal, zero_c, excess(c_nominal), excess(zero_c), zero_c)
        for z in (0, 1):
            lo, hi = state[0], state[1]
            zk = jnp.full((nt, LANES), z, I32)
            inside = jnp.logical_and(jnp.logical_and(zk > lo, zk < hi), jnp.logical_not(resolved(lo, hi, state[2])))
            new = probe(jnp.where(inside, zk, lo), *state)
            state = tuple(jnp.where(inside, upd, old) for old, upd in zip(state, new))
        pending0 = jnp.max(jnp.where(resolved(state[0], state[1], state[2]), 0.0, 1.0))
        state = lax.while_loop(lambda st: st[8] > 0.0, search_step, (jnp.int32(0),) + state + (pending0,))
        tau_c, hi_c, c_hi = state[1], state[2], state[4]
        need_c = jnp.where(hi_c - 1 <= tau_c, k_f - c_hi, float(seq))
        need = to_rows(jnp.where(tau_c == INT_MIN, 0.0, need_c))
        tau = to_rows(tau_c)
        upper = lax.broadcasted_iota(I32, (LANES, LANES), 0) <= lax.broadcasted_iota(I32, (LANES, LANES), 1)
        scan = jnp.concatenate([jnp.where(upper, 1.0, 0.0), jnp.ones((LANES, LANES), F32)], axis=1).astype(BF16)
        cnt_ref[...] = jnp.zeros((bq, LANES), F32)

        def build_mask(cc, carry):
            t = keys_ref[cc]
            cols = []
            seen = cnt_ref[...]
            for j in range(nlc):
                tj = lane_chunk(t, j)
                tie = tj == tau
                pt = jnp.dot(jnp.where(tie, 1.0, 0.0).astype(BF16), scan, preferred_element_type=F32)
                keep_tie = jnp.logical_and(tie, seen + pt[:, :LANES] <= need)
                cols.append(jnp.where(jnp.logical_or(tj > tau, keep_tie), 0.0, NEG_INF))
                seen = seen + pt[:, LANES:]
            cnt_ref[...] = seen
            keys_ref[cc] = lax.bitcast_convert_type(jnp.concatenate(cols, axis=1), I32)
            return carry

        lax.fori_loop(0, n_chunks, build_mask, 0)

        def mask_out(cc, carry):
            keys_ref[cc] = lax.bitcast_convert_type(jnp.full((bq, bk), NEG_INF, F32), I32)
            return carry

        lax.fori_loop(n_chunks, n_att * ATTN_SPAN, mask_out, 0)

        m_ref[...] = jnp.full(m_ref.shape, NEG_INF, F32)
        acc_ref[...] = jnp.zeros_like(acc_ref)

    _select()

    c1 = (ATT_HEAD_DIM ** -0.5) * LOG2E
    q = (qa_ref[0].astype(F32) * c1).astype(BF16)
    ones_v = jnp.ones((bka, ATT_HEAD_DIM), BF16)

    def heads(near, ca, kk, vv):
        for h in range(ATT_HEADS):
            hs = slice(h * ATT_HEAD_DIM, (h + 1) * ATT_HEAD_DIM)
            s_ref[...] = lax.dot_general(q[:, hs], kk[:, hs], (((1,), (1,)), ((), ())),
                                         preferred_element_type=F32)
            for r in range(bq // SOFTMAX_ROWS):
                rows = slice(r * SOFTMAX_ROWS, (r + 1) * SOFTMAX_ROWS)
                mask = jnp.concatenate([keys_ref[ca * ATTN_SPAN + i, rows, :] for i in range(ATTN_SPAN)], axis=1)
                t = s_ref[rows, :] + lax.bitcast_convert_type(mask, F32)
                if near:
                    t = t + bias_ref[h, rows, :]
                m_old = m_ref[h, rows, :]
                m_new = jnp.maximum(m_old, jnp.max(t, axis=1, keepdims=True))
                m_safe = jnp.where(m_new == NEG_INF, 0.0, m_new)
                alpha_ref[rows, :] = jnp.exp2(m_old - m_safe)
                p_ref[rows, :] = jnp.exp2(t - tile_lanes(m_safe, bka // LANES)).astype(BF16)
                m_ref[h, rows, :] = m_new
            v_aug = jnp.concatenate([vv[:, hs], ones_v], axis=1)
            acc_ref[h] = (tile_lanes(alpha_ref[...], 2) * acc_ref[h]
                          + jnp.dot(p_ref[...], v_aug, preferred_element_type=F32))

    def attend(ca, carry):
        slot = ca % KV_BUFFERS
        nxt = ca + (KV_BUFFERS - 1)

        @pl.when(nxt < n_att)
        def _():
            kv_copy(nxt, nxt % KV_BUFFERS).start()

        kv_copy(ca, slot).wait()
        kk = kv_buf[slot, :, KV_K:KV_K + ATT_W]
        vv = kv_buf[slot, :, KV_V:KV_V + ATT_W]
        kmax = kmax_ref[b * (seq // bk) + ca * ATTN_SPAN]
        for i in range(1, ATTN_SPAN):
            kmax = jnp.maximum(kmax, kmax_ref[b * (seq // bk) + ca * ATTN_SPAN + i])
        far = qmin_ref[b * (seq // bq) + qi] - kmax >= far_dist

        @pl.when(far)
        def _():
            heads(False, ca, kk, vv)

        @pl.when(jnp.logical_not(far))
        def _():
            nsub = seq // LANES
            for r in range(nsub_q):
                for j in range(ATTN_SPAN * nsub_k):
                    rows = slice(r * LANES, (r + 1) * LANES)
                    cols = slice(j * LANES, (j + 1) * LANES)
                    pcols = slice((j % nsub_k) * LANES, (j % nsub_k + 1) * LANES)
                    sub_far = (qmin_sub_ref[b * nsub + qi * nsub_q + r]
                               - kmax_sub_ref[b * nsub + ca * ATTN_SPAN * nsub_k + j]) >= far_dist

                    @pl.when(sub_far)
                    def _():
                        for h in range(ATT_HEADS):
                            bias_ref[h, rows, cols] = jnp.zeros((LANES, LANES), F32)

                    @pl.when(jnp.logical_not(sub_far))
                    def _():
                        half = LANES // 2
                        for hr in range(2):
                            rr = slice(r * LANES + hr * half, r * LANES + (hr + 1) * half)
                            dist = posq_ref[0, rr, :] - posk_ref[0, ca * ATTN_SPAN + j // nsub_k, :, pcols]
                            idx = jnp.clip(dist, 0, REL_MAX_DIST - 1)
                            for h in range(ATT_HEADS):
                                row = jnp.broadcast_to(btab_ref[h:h + 1, :], (half, LANES))
                                bias_ref[h, rr, cols] = jnp.take_along_axis(row, idx, axis=1)

            heads(True, ca, kk, vv)

        return carry

    lax.fori_loop(0, n_att, attend, 0)

    outs = []
    for h in range(ATT_HEADS):
        a = acc_ref[h]
        outs.append(a[:, :ATT_HEAD_DIM] / a[:, ATT_HEAD_DIM:])
    o_ref[0] = jnp.concatenate(outs, axis=1).astype(BF16)


def _attn(p_a13, p_qi3, p_kv3, positions, rel_table, *, bq, bk):
    b, s, _ = p_qi3.shape
    assert bq % LANES == 0 and bk % LANES == 0
    n_keep = min(TOPK_MAX, s // 4)
    assert n_keep <= 2 * LANES
    assert (s // bk) % ATTN_SPAN == 0
    bka = bk * ATTN_SPAN
    nq, nkc, nsub = s // bq, s // bk, s // LANES
    qmin =jnp.min(positions.reshape(b, nq, bq), axis=-1).reshape(-1)
    kmax = jnp.max(positions.reshape(b, nkc, bk), axis=-1).reshape(-1)
    qmin_sub = jnp.min(positions.reshape(b, nsub, LANES), axis=-1).reshape(-1)
    kmax_sub = jnp.max(positions.reshape(b, nsub, LANES), axis=-1).reshape(-1)
    buckets = _t5_bucket_table()
    far_dist = REL_MAX_DIST - 1
    assert np.all(np.diff(buckets) >= 0) and buckets[far_dist] == REL_BUCKETS - 1
    relb = ((rel_table - rel_table[REL_BUCKETS - 1:REL_BUCKETS, :]) * LOG2E).astype(F32)
    btab = jnp.pad(jnp.transpose(relb[buckets, :]), ((0, 8 - ATT_HEADS), (0, 0)))

    grid_spec = pltpu.PrefetchScalarGridSpec(
        num_scalar_prefetch=4,
        grid=(b, nq),
        in_specs=[
            pl.BlockSpec((1, bq, ATT_W), lambda i, t, *_: (i, t, A1_QA // ATT_W)),
            pl.BlockSpec((1, bq, QI_W), lambda i, t, *_: (i, t, 0)),
            pl.BlockSpec((1, s, LANES), lambda i, t, *_: (i, 0, QI_KIW // LANES)),
            pl.BlockSpec(memory_space=pl.ANY),
            pl.BlockSpec((1, bq, 1), lambda i, t, *_: (i, t, 0)),
            pl.BlockSpec((1, nkc, 1, bk), lambda i, t, *_: (i, 0, 0, 0)),
            pl.BlockSpec((8, LANES), lambda i, t, *_: (0, 0)),
        ],
        out_specs=pl.BlockSpec((1, bq, ATT_W), lambda i, t, *_: (i, t, 0)),
        scratch_shapes=[
            pltpu.VMEM((KV_BUFFERS, bka, KV_W), BF16),
            pltpu.SemaphoreType.DMA((KV_BUFFERS,)),
            pltpu.VMEM((nkc, bq, bk), I32),
            pltpu.VMEM((bq, LANES), I32),
            pltpu.VMEM((bq, LANES), I32),
            pltpu.VMEM((bq, LANES), F32),
            pltpu.VMEM((ATT_HEADS, bq, LANES), F32),
            pltpu.VMEM((ATT_HEADS, bq, 2 * ATT_HEAD_DIM), F32),
            pltpu.VMEM((ATT_HEADS, bq, bka), F32),
            pltpu.VMEM((bq, bka), F32),
            pltpu.VMEM((bq, bka), BF16),
            pltpu.VMEM((bq, LANES), F32),
        ],
    )
    kern = functools.partial(_attn_kernel, bq=bq, bk=bk, n_keep=n_keep, seq=s,
                             far_dist=far_dist)
    return pl.pallas_call(
        kern,
        out_shape=jax.ShapeDtypeStruct((b, s, ATT_W), BF16),
        grid_spec=grid_spec,
        compiler_params=_cparams(("parallel", "arbitrary")),
        name="sparse_attn",
    )(qmin, kmax, qmin_sub, kmax_sub,
      p_a13, p_qi3, p_qi3, p_kv3, positions.reshape(b, s, 1), positions.reshape(b, nkc, 1, bk), btab)


def _merge_kernel(or_ref, oa_ref, os_ref, gr_ref, ga_ref, gs_ref, wr_ref, wa_ref, ws_ref, wo_ref, x_ref, o_ref):
    j = pl.program_id(1)

    @pl.when(j == 0)
    def _():
        o_ref[...] = jnp.zeros_like(o_ref)

    def branch(o, w, g):
        return jax.nn.sigmoid(g[0].astype(F32)) * jnp.dot(o[...], w[...], preferred_element_type=F32)

    merged = branch(or_ref, wr_ref, gr_ref) + branch(oa_ref, wa_ref, ga_ref) + branch(os_ref, ws_ref, gs_ref)
    o_ref[...] += jnp.dot(merged.astype(BF16), wo_ref[...], preferred_element_type=F32)

    @pl.when(j == pl.num_programs(1) - 1)
    def _():
        o_ref[...] = x_ref[...] + o_ref[...]


def _merge(o_r, o_a, o_s, p_g, wr, wa, ws, wo, x2, *, tm):
    t, d = x2.shape
    tn = PROJ_TILE
    per_gate = d // tn

    def gspec(g):
        return pl.BlockSpec((1, tm, tn), lambda i, j: (g * per_gate + j, i, 0))

    return pl.pallas_call(
        _merge_kernel,
        out_shape=jax.ShapeDtypeStruct((t, d), F32),
        grid=(t // tm, per_gate),
        in_specs=[
            pl.BlockSpec((tm, RET_V_W), lambda i, j: (i, 0)),
            pl.BlockSpec((tm, ATT_W), lambda i, j: (i, 0)),
            pl.BlockSpec((tm, SG_W), lambda i, j: (i, 0)),
            gspec(0), gspec(1), gspec(2),
            pl.BlockSpec((RET_V_W, tn), lambda i, j: (0, j)),
            pl.BlockSpec((ATT_W, tn), lambda i, j: (0, j)),
            pl.BlockSpec((SG_W, tn), lambda i, j: (0, j)),
            pl.BlockSpec((tn, d), lambda i, j: (j, 0)),
            pl.BlockSpec((tm, d), lambda i, j: (i, 0)),
        ],
        out_specs=pl.BlockSpec((tm, d), lambda i, j: (i, 0)),
        compiler_params=_cparams(("parallel", "arbitrary")),
        name="merge",
    )(o_r, o_a, o_s, p_g, p_g, p_g, wr, wa, ws, wo, x2)


def _split_w_in_kernel(w_ref, wa_ref, wb_ref):
    w = w_ref[0]
    wa_ref[:, :W_A_COLS] = w[:, :W_A_COLS].astype(BF16)
    wa_ref[:, W_A_COLS:] = jnp.zeros((w.shape[0], W_A_PAD - W_A_COLS), BF16)
    wb_ref[...] = w[:, W_A_COLS:].astype(BF16)


def _split_w_in(w_in, layer, d_model, *, tr=64):
    _, rows, cols = w_in.shape
    assert cols == W_A_COLS + 2 * SG_W + 3 * d_model and A1_W + KV_W + QI_W == W_A_PAD
    w_b_cols = cols - W_A_COLS
    assert w_b_cols % PROJ_TILE == 0 and rows % tr == 0
    return pl.pallas_call(
        _split_w_in_kernel,
        out_shape=(jax.ShapeDtypeStruct((rows, W_A_PAD), BF16), jax.ShapeDtypeStruct((rows, w_b_cols), BF16)),
        grid=(rows // tr,),
        in_specs=[pl.BlockSpec((1, tr, cols), lambda i: (layer, i, 0))],
        out_specs=(pl.BlockSpec((tr, W_A_PAD), lambda i: (i, 0)), pl.BlockSpec((tr, w_b_cols), lambda i: (i, 0))),
        compiler_params=_cparams(("parallel",)),
        name="split_w_in",
    )(w_in)


def _pick(n, prefs):
    for p in prefs:
        if n % p == 0:
            return p
    return n


def kernel(x, positions, rel_table, ffn1_norm, ffn1_w_gate, ffn1_w_up, ffn1_w_down, mix_norm, w_in, ret_norm, sg_norm, sg_w, sg_b, w_br_ret, w_br_att, w_br_sg, w_out, ffn2_norm, ffn2_w_gate, ffn2_w_up, ffn2_w_down, final_norm):
    b, s, d = x.shape
    t = b * s
    depth = w_in.shape[0]
    d_ff = ffn1_w_gate.shape[2]
    assert s % RET_CHUNK == 0 and d % PROJ_TILE == 0

    tm = _pick(t, (512, 256, 128))
    tm_proj = _pick(t, (1024, 512, 256, 128))
    tm_ffn = _pick(t, (512, 256, 128))
    tf = _pick(d_ff, (512, 256, 128))
    bq = _pick(s, (256, 128))
    bk = _pick(s, (512, 256, 128))

    cos_t, sin_t = _rope_tables(positions, tm=_pick(s, (2048, 1024, 512, 256, 128)))
    consts = _retention_constants()
    x2 = x.reshape(t, d)
    for l in range(depth):
        x2 = _ffn(x2, ffn1_norm[l], ffn1_w_gate[l].astype(BF16), ffn1_w_up[l].astype(BF16),
                  ffn1_w_down[l].astype(BF16), tm=tm_ffn, tf=tf)
        p_a1, p_kv, p_qi, p_sg, p_g = _proj(x2, mix_norm[l], *_split_w_in(w_in, l, d), tm=tm_proj)
        p_a13 = p_a1.reshape(b, s, -1)
        o_r, o_s = _local(p_a13, p_sg.reshape(b, s, -1), cos_t, sin_t, ret_norm[l], sg_norm[l], sg_w[l], sg_b[l],
                          consts)
        o_a = _attn(p_a13, p_qi.reshape(b, s, -1), p_kv.reshape(b, s, -1), positions, rel_table, bq=bq, bk=bk)
        x2 = _merge(o_r.reshape(t, -1), o_a.reshape(t, -1), o_s.reshape(t, -1), p_g,
                    w_br_ret[l].astype(BF16), w_br_att[l].astype(BF16), w_br_sg[l].astype(BF16),
                    w_out[l].astype(BF16), x2, tm=tm)
        x2 = _ffn(x2, ffn2_norm[l], ffn2_w_gate[l].astype(BF16), ffn2_w_up[l].astype(BF16),
                  ffn2_w_down[l].astype(BF16), final_norm if l == depth - 1 else None, tm=tm_ffn, tf=tf)
    return x2.reshape(b, s, d)
```
